```python
import math
import jax, jax.numpy as jnp
from jax import lax
import numpy as np

D_MODEL = 1024
BATCH = 8
SEQ = 4096
DEPTH = 4

GRID_W = 64
CTX_LEN = 256
BRANCH_W = 512
N_BRANCH = 3
ML_HEADS = 4
ML_DH = 128
ML_W = ML_HEADS * ML_DH
ML_CHUNK = 64
ML_CONV = 3
DF_HEADS = 4
DF_DQK = 64
DF_DV = 128
DF_W = DF_HEADS * DF_DV
DF_QBLOCK = 128
ROPE_BASE = 10000.0
NA_HEADS = 8
NA_DH = 64
NA_W = NA_HEADS * NA_DH
NA_KH = 8
NA_KW = 16
NA_KBW = 2 * NA_KW
D_FF = 2816
EPS = 1e-6
IN_SIZES = (ML_W, ML_W, ML_W, ML_W, 4 * ML_HEADS,
            2 * DF_HEADS * DF_DQK, 2 * DF_HEADS * DF_DQK, DF_W,
            NA_W, NA_W, NA_W,
            N_BRANCH * D_MODEL)
D_IN = sum(IN_SIZES)

kernel_name = "hybrid_mlstm_diffattn_natten_macaron_dit"


def rmsnorm(x, g):
    xf = x.astype(jnp.float32)
    y = xf * lax.rsqrt(jnp.mean(xf * xf, axis=-1, keepdims=True) + EPS)
    return (y * g.astype(jnp.float32)).astype(x.dtype)


def modulate(h, shift, scale):
    return h * (1 + scale) + shift


def swiglu(h, w1, w2):
    a, b = jnp.split(h @ w1, 2, axis=-1)
    return (jax.nn.silu(a) * b) @ w2


def ffn_half(x, mod, g, w1, w2):
    shift, scale, gate = mod
    return x + 0.5 * gate * swiglu(modulate(rmsnorm(x, g), shift, scale), w1, w2)


def head_rmsnorm(h, g):
    B, T, H, dh = h.shape
    return rmsnorm(h, g.reshape(H, dh)).reshape(B, T, H * dh)


def dwconv_centred(x, w, b):
    C = x.shape[-1]
    y = lax.conv_general_dilated(x, w[:, None, :].astype(x.dtype), window_strides=(1,),
                                 padding='SAME', dimension_numbers=('NWC', 'WIO', 'NWC'),
                                 feature_group_count=C)
    return y + b


def rope_2d(S):
    t = jnp.arange(S)
    row = (t // GRID_W).astype(jnp.float32)
    col = (t % GRID_W).astype(jnp.float32)
    half = DF_DQK // 2
    freqs = ROPE_BASE ** (-jnp.arange(0, half, 2, dtype=jnp.float32) / half)
    ang = jnp.concatenate([row[:, None] * freqs, col[:, None] * freqs], axis=-1)
    return jnp.cos(ang), jnp.sin(ang)


def apply_rope(x, cos, sin):
    half = DF_DQK // 2
    x1, x2 = x[..., :half], x[..., half:]
    return jnp.concatenate([x1 * cos - x2 * sin, x1 * sin + x2 * cos], axis=-1).astype(x.dtype)


def mlstm_chunk_scan(q, k, v, log_i, log_f, state):
    B, H, T, dh = q.shape
    L = ML_CHUNK
    nc = T // L

    def to_chunks(a):
        return jnp.moveaxis(a.reshape(B, H, nc, L, *a.shape[3:]), 2, 0)

    xs = tuple(to_chunks(a) for a in (q, k, v, log_i, log_f))
    tri = jnp.tril(jnp.ones((L, L), dtype=bool))

    def step(carry, inp):
        C, n, m = carry
        qc, kc, vc, li, lf = inp
        b = jnp.cumsum(lf, axis=-1)
        dmat = jnp.where(tri, b[..., :, None] - b[..., None, :] + li[..., None, :], -jnp.inf)
        inter = b + m[..., None]
        m_t = jnp.maximum(inter, jnp.max(dmat, axis=-1))
        w = jnp.exp(dmat - m_t[..., None])
        s_inter = jnp.exp(inter - m_t)
        qk = jnp.einsum('bhtd,bhsd->bhts', qc, kc) * w
        num = (jnp.einsum('bhts,bhse->bhte', qk, vc)
               + s_inter[..., None] * jnp.einsum('bhed,bhtd->bhte', C, qc))
        den = jnp.sum(qk, axis=-1) + s_inter * jnp.einsum('bhd,bhtd->bht', n, qc)
        h = num / jnp.maximum(jnp.abs(den), jnp.exp(-m_t))[..., None]
        b_last = b[..., -1]
        wk = b_last[..., None] - b + li
        m_new = jnp.maximum(b_last + m, jnp.max(wk, axis=-1))
        decay = jnp.exp(b_last + m - m_new)
        wk = jnp.exp(wk - m_new[..., None])
        C_new = decay[..., None, None] * C + jnp.einsum('bhs,bhse,bhsd->bhed', wk, vc, kc)
        n_new = decay[..., None] * n + jnp.einsum('bhs,bhsd->bhd', wk, kc)
        return (C_new, n_new, m_new), h

    state, hs = lax.scan(step, state, xs)
    return jnp.moveaxis(hs, 0, 2).reshape(B, H, T, dh), state


def mlstm_inputs(q, k, v, o, g, conv_w, conv_b, gate_b):
    qk = jax.nn.silu(dwconv_centred(jnp.concatenate([q, k], axis=-1), conv_w, conv_b))
    q, k = jnp.split(qk, 2, axis=-1)
    B, T, _ = q.shape

    def heads(a):
        return a.reshape(B, T, ML_HEADS, ML_DH).transpose(0, 2, 1, 3).astype(jnp.float32)

    gates = (g + gate_b).astype(jnp.float32).reshape(B, T, 2, 2, ML_HEADS).transpose(2, 3, 0, 4, 1)
    log_i = gates[:, 0]
    log_f = jax.nn.log_sigmoid(gates[:, 1])
    return heads(q) * ML_DH ** -0.5, heads(k), heads(v), jax.nn.sigmoid(o), log_i, log_f


def mlstm_mixer(lat, ctx, conv_w, conv_b, gate_b, norm_g, ctx_out):
    ql, kl, vl, ol, il, fl = mlstm_inputs(*lat, conv_w, conv_b, gate_b)
    qc, kc, vc, oc, ic, fc = mlstm_inputs(*ctx, conv_w, conv_b, gate_b)
    B = ql.shape[0]
    zero = (jnp.zeros((B, ML_HEADS, ML_DH, ML_DH), jnp.float32),
            jnp.zeros((B, ML_HEADS, ML_DH), jnp.float32),
            jnp.zeros((B, ML_HEADS), jnp.float32))

    def flip(a):
        return jnp.flip(a, axis=2)

    hcf, st_f = mlstm_chunk_scan(qc, kc, vc, ic[0], fc[0], zero)
    hlf, _ = mlstm_chunk_scan(ql, kl, vl, il[0], fl[0], st_f)
    hcb, st_b = mlstm_chunk_scan(flip(qc), flip(kc), flip(vc), flip(ic[1]), flip(fc[1]), zero)
    hlb, _ = mlstm_chunk_scan(flip(ql), flip(kl), flip(vl), flip(il[1]), flip(fl[1]), st_b)

    def finish(h, o):
        return (o * head_rmsnorm(h.transpose(0, 2, 1, 3).astype(o.dtype), norm_g)).astype(o.dtype)

    y_lat = finish(hlf + flip(hlb), ol)
    y_ctx = finish(hcf + flip(hcb), oc) if ctx_out else None
    return y_lat, y_ctx


def diff_mixer(q_lat, k_lat, v_lat, q_ctx, k_ctx, v_ctx, lam_vecs, norm_g, lam_init, ctx_out):
    B, S, _ = q_lat.shape
    scale = DF_DQK ** -0.5

    def qk_heads(a):
        T = a.shape[1]
        a = a.reshape(B, T, DF_HEADS, 2, DF_DQK).transpose(3, 0, 2, 1, 4)
        return a[0], a[1]

    def v_heads(a):
        return a.reshape(B, a.shape[1], DF_HEADS, DF_DV).transpose(0, 2, 1, 3)

    cos, sin = rope_2d(S)
    q1l, q2l = [apply_rope(t, cos, sin) for t in qk_heads(q_lat)]
    k1l, k2l = [apply_rope(t, cos, sin) for t in qk_heads(k_lat)]
    q1c, q2c = qk_heads(q_ctx)
    k1c, k2c = qk_heads(k_ctx)
    vl, vc = v_heads(v_lat), v_heads(v_ctx)
    lv = lam_vecs.astype(jnp.float32)
    lam = jnp.exp(jnp.sum(lv[0] * lv[1])) - jnp.exp(jnp.sum(lv[2] * lv[3])) + lam_init

    def attend(q1, q2, k1, k2, v):
        s1 = (jnp.einsum('bhqd,bhkd->bhqk', q1, k1) * scale).astype(jnp.float32)
        s2 = (jnp.einsum('bhqd,bhkd->bhqk', q2, k2) * scale).astype(jnp.float32)
        p = jax.nn.softmax(s1, axis=-1) - lam * jax.nn.softmax(s2, axis=-1)
        return jnp.einsum('bhqk,bhkd->bhqd', p.astype(v.dtype), v)

    k1a = jnp.concatenate([k1l, k1c], axis=2)
    k2a = jnp.concatenate([k2l, k2c], axis=2)
    va = jnp.concatenate([vl, vc], axis=2)
    nqb = S // DF_QBLOCK

    def blocks(a):
        return jnp.moveaxis(a.reshape(B, DF_HEADS, nqb, DF_QBLOCK, DF_DQK), 2, 0)

    o = lax.map(lambda qq: attend(qq[0], qq[1], k1a, k2a, va), (blocks(q1l), blocks(q2l)))
    o = jnp.moveaxis(o, 0, 2).reshape(B, DF_HEADS, S, DF_DV)

    def finish(h):
        return head_rmsnorm(h.transpose(0, 2, 1, 3), norm_g) * (1.0 - lam_init)

    y_lat = finish(o)
    y_ctx = finish(attend(q1c, q2c, k1c, k2c, vc)) if ctx_out else None
    return y_lat, y_ctx


def na_mixer(q_lat, k_lat, v_lat, q_ctx, k_ctx, v_ctx, rel_bias, ctx_out):
    B, S, _ = q_lat.shape
    rows = S // GRID_W
    kh = min(NA_KH, rows)
    nb = GRID_W // NA_KW
    scale = NA_DH ** -0.5

    def grid(a):
        return a.reshape(B, rows, GRID_W, NA_HEADS, NA_DH)

    qg = jnp.moveaxis(grid(q_lat), 1, 0)
    kg, vg = grid(k_lat), grid(v_lat)
    kc = k_ctx.reshape(B, -1, NA_HEADS, NA_DH)
    vc = v_ctx.reshape(B, -1, NA_HEADS, NA_DH)
    qcol = np.arange(GRID_W).reshape(nb, NA_KW)
    qstart = np.clip(qcol - NA_KW // 2, 0, GRID_W - NA_KW)
    kstart = np.clip(np.arange(nb) * NA_KW - NA_KW // 2, 0, GRID_W - NA_KBW)
    kcol = kstart[:, None] + np.arange(NA_KBW)
    col_ok = ((kcol[:, None, :] >= qstart[:, :, None])
              & (kcol[:, None, :] < qstart[:, :, None] + NA_KW))
    dc_idx = np.clip(kcol[:, None, :] - qcol[:, :, None] + NA_KW - 1, 0, 2 * NA_KW - 2)
    bias_cols = rel_bias[:, :, dc_idx].astype(jnp.float32)
    n_lat = kh * NA_KBW

    def row_fn(args):
        r, q_row = args
        rs = jnp.clip(r - kh // 2, 0, rows - kh)
        kb = lax.dynamic_slice_in_dim(kg, rs, kh, axis=1)[:, :, kcol]
        vb = lax.dynamic_slice_in_dim(vg, rs, kh, axis=1)[:, :, kcol]
        qb = q_row.reshape(B, nb, NA_KW, NA_HEADS, NA_DH)
        dr_idx = rs + jnp.arange(kh) - r + NA_KH - 1
        bias = jnp.take(bias_cols, dr_idx, axis=1).transpose(0, 2, 3, 1, 4)
        s_lat = (jnp.einsum('bnqhd,bknwhd->bhnqkw', qb, kb) * scale).astype(jnp.float32) + bias
        s_lat = jnp.where(col_ok[:, :, None, :], s_lat, -jnp.inf)
        s_ctx = (jnp.einsum('bnqhd,bchd->bhnqc', qb, kc) * scale).astype(jnp.float32)
        s = jnp.concatenate([s_lat.reshape(B, NA_HEADS, nb, NA_KW, n_lat), s_ctx], axis=-1)
        p = jax.nn.softmax(s, axis=-1).astype(vb.dtype)
        p_lat = p[..., :n_lat].reshape(B, NA_HEADS, nb, NA_KW, kh, NA_KBW)
        o = (jnp.einsum('bhnqkw,bknwhd->bnqhd', p_lat, vb)
             + jnp.einsum('bhnqc,bchd->bnqhd', p[..., n_lat:], vc))
        return o.reshape(B, GRID_W, NA_W)

    out = lax.map(row_fn, (jnp.arange(rows), qg))
    y_lat = jnp.moveaxis(out, 0, 1).reshape(B, S, NA_W)
    y_ctx = None
    if ctx_out:
        qc = q_ctx.reshape(B, -1, NA_HEADS, NA_DH)
        s = (jnp.einsum('bqhd,bkhd->bhqk', qc, kc) * scale).astype(jnp.float32)
        p = jax.nn.softmax(s, axis=-1).astype(vc.dtype)
        y_ctx = jnp.einsum('bhqk,bkhd->bqhd', p, vc).reshape(B, -1, NA_W)
    return y_lat, y_ctx


def merge_branches(gate_pre, outs, w_branch, w_out):
    gates = jnp.split(jax.nn.sigmoid(gate_pre.astype(jnp.float32)).astype(gate_pre.dtype), N_BRANCH, axis=-1)
    y = sum(g * (o @ w_branch[i]) for i, (g, o) in enumerate(zip(gates, outs)))
    return y @ w_out


def token_mixers(h_lat, h_ctx, ml_conv_w, ml_conv_b, ml_gate_b, ml_norm_g, df_lambda, df_norm_g,
                 na_rel_bias, w_branch, w_out, lam_init, ctx_out):
    offs = np.cumsum(IN_SIZES)[:-1].tolist()
    (ml_q, ml_k, ml_v, ml_o, ml_g, df_q, df_k, df_v, na_q, na_k, na_v, gate_l) = jnp.split(h_lat, offs, axis=-1)
    (mc_q, mc_k, mc_v, mc_o, mc_g, dc_q, dc_k, dc_v, nc_q, nc_k, nc_v, gate_c) = jnp.split(h_ctx, offs, axis=-1)
    a_lat, a_ctx = mlstm_mixer((ml_q, ml_k, ml_v, ml_o, ml_g), (mc_q, mc_k, mc_v, mc_o, mc_g),
                               ml_conv_w, ml_conv_b, ml_gate_b, ml_norm_g, ctx_out)
    b_lat, b_ctx = diff_mixer(df_q, df_k, df_v, dc_q, dc_k, dc_v, df_lambda, df_norm_g, lam_init, ctx_out)
    c_lat, c_ctx = na_mixer(na_q, na_k, na_v, nc_q, nc_k, nc_v, na_rel_bias, ctx_out)
    y_lat = merge_branches(gate_l, (a_lat, b_lat, c_lat), w_branch, w_out)
    y_ctx = merge_branches(gate_c, (a_ctx, b_ctx, c_ctx), w_branch, w_out) if ctx_out else None
    return y_lat, y_ctx


def setup_inputs(seed: int = 0) -> dict:
    key = jax.random.key(seed)
    ks = jax.random.split(key, 24)
    D = D_MODEL

    def nrm(k, shape, s):
        return jax.random.normal(k, shape, jnp.float32) * s

    ib = nrm(ks[12], (DEPTH, 2, 1, ML_HEADS), 0.1)
    fb = 3.0 + 3.0 * jax.random.uniform(ks[13], (DEPTH, 2, 1, ML_HEADS), jnp.float32)
    return {
        'x': nrm(ks[0], (BATCH, SEQ, D), 1.0),
        'c': nrm(ks[1], (BATCH, D), 1.0),
        'ctx': nrm(ks[2], (BATCH, CTX_LEN, D), 1.0),
        'c_ctx': nrm(ks[3], (D,), 1.0),
        'w_ada': nrm(ks[4], (DEPTH, D, 9 * D), D ** -0.5),
        'b_ada': nrm(ks[5], (DEPTH, 9 * D), 0.02),
        'norm_g': 1.0 + nrm(ks[6], (DEPTH, 3, D), 0.05),
        'ffn_w1': nrm(ks[7], (DEPTH, 2, D, 2 * D_FF), D ** -0.5),
        'ffn_w2': nrm(ks[8], (DEPTH, 2, D_FF, D), D_FF ** -0.5),
        'w_in': nrm(ks[9], (DEPTH, D, D_IN), D ** -0.5),
        'ml_conv_w': nrm(ks[10], (DEPTH, ML_CONV, 2 * ML_W), ML_CONV ** -0.5),
        'ml_conv_b': nrm(ks[11], (DEPTH, 2 * ML_W), 0.02),
        'ml_gate_b': jnp.concatenate([ib, fb], axis=2).reshape(DEPTH, 4 * ML_HEADS),
        'ml_norm_g': 1.0 + nrm(ks[14], (DEPTH, ML_W), 0.05),
        'df_lambda': nrm(ks[15], (DEPTH, 4, DF_DQK), 0.1),
        'df_norm_g': 1.0 + nrm(ks[16], (DEPTH, DF_W), 0.05),
        'na_rel_bias': nrm(ks[17], (DEPTH, NA_HEADS, 2 * NA_KH - 1, 2 * NA_KW - 1), 0.1),
        'w_branch': nrm(ks[18], (DEPTH, N_BRANCH, BRANCH_W, D), BRANCH_W ** -0.5),
        'w_out': nrm(ks[19], (DEPTH, D, D), D ** -0.5),
        'final_g': 1.0 + nrm(ks[20], (D,), 0.05),
    }


def reference(x, c, ctx, c_ctx, w_ada, b_ada, norm_g, ffn_w1, ffn_w2, w_in, ml_conv_w, ml_conv_b,
              ml_gate_b, ml_norm_g, df_lambda, df_norm_g, na_rel_bias, w_branch, w_out, final_g):
    for l in range(DEPTH):
        ctx_out = l < DEPTH - 1
        lam_init = 0.8 - 0.6 * math.exp(-0.3 * l)
        m_lat = jnp.split((jax.nn.silu(c) @ w_ada[l] + b_ada[l])[:, None, :], 9, axis=-1)
        m_ctx = jnp.split((jax.nn.silu(c_ctx) @ w_ada[l] + b_ada[l])[None, None, :], 9, axis=-1)
        x = ffn_half(x, m_lat[0:3], norm_g[l, 0], ffn_w1[l, 0], ffn_w2[l, 0])
        ctx = ffn_half(ctx, m_ctx[0:3], norm_g[l, 0], ffn_w1[l, 0], ffn_w2[l, 0])
        h_lat = modulate(rmsnorm(x, norm_g[l, 1]), m_lat[3], m_lat[4]) @ w_in[l]
        h_ctx = modulate(rmsnorm(ctx, norm_g[l, 1]), m_ctx[3], m_ctx[4]) @ w_in[l]
        y_lat, y_ctx = token_mixers(h_lat, h_ctx, ml_conv_w[l], ml_conv_b[l], ml_gate_b[l], ml_norm_g[l],
                                    df_lambda[l], df_norm_g[l], na_rel_bias[l], w_branch[l], w_out[l],
                                    lam_init, ctx_out)
        x = x + m_lat[5] * y_lat
        x = ffn_half(x, m_lat[6:9], norm_g[l, 2], ffn_w1[l, 1], ffn_w2[l, 1])
        if ctx_out:
            ctx = ctx + m_ctx[5] * y_ctx
            ctx = ffn_half(ctx, m_ctx[6:9], norm_g[l, 2], ffn_w1[l, 1], ffn_w2[l, 1])
    return rmsnorm(x, final_g)
```

```python
import functools
import math

import numpy as np
import jax
import jax.numpy as jnp
from jax import lax
from jax.experimental import pallas as pl
from jax.experimental.pallas import tpu as pltpu

F32 = jnp.float32
BF16 = jnp.bfloat16

D_MODEL = 1024
GRID_W = 64
N_BRANCH = 3
BRANCH_W = 512
ML_HEADS = 4
ML_DH = 128
ML_W = ML_HEADS * ML_DH
DF_HEADS = 4
DF_DQK = 64
DF_DV = 128
DF_W = DF_HEADS * DF_DV
ROPE_BASE = 10000.0
NA_HEADS = 8
NA_DH = 64
NA_W = NA_HEADS * NA_DH
NA_KH = 8
NA_KW = 16
D_FF = 2816
EPS = 1e-6
GATE_W = 4 * ML_HEADS
GATE_PAD = 128
OFF_ML = 0
OFF_MLG = 4 * ML_W
OFF_DF = OFF_MLG + GATE_W
OFF_GATE = OFF_DF + 3 * DF_W + 3 * NA_W
N_MAIN = 4 * ML_W + 3 * DF_W + 3 * NA_W

TT = 256
NA_QROWS = 4
NA_KROWS = 12
NEG = -1e30
VMEM_LIMIT = 56 * 1024 * 1024


def _cparams(sem):
    return pltpu.CompilerParams(dimension_semantics=sem, vmem_limit_bytes=VMEM_LIMIT)


def _norm_mod(x, g, shift, scale):
    var = jnp.mean(x * x, axis=-1, keepdims=True)
    return x * lax.rsqrt(var + EPS) * g * (1.0 + scale) + shift


def _sigmoid(x):
    return 1.0 / (1.0 + jnp.exp(-x))


def _ada_kernel(c_ref, w_ref, b_ref, o_ref):
    c = c_ref[...]
    h = (c * _sigmoid(c)).astype(BF16)
    o_ref[...] = jnp.dot(h, w_ref[...].astype(BF16), preferred_element_type=F32) + b_ref[...]


def _ada_call(cvec, w_ada, b_ada):
    L, D, N = w_ada.shape
    tn = N // 8
    return pl.pallas_call(
        _ada_kernel,
        grid=(L, N // tn),
        in_specs=[
            pl.BlockSpec(cvec.shape, lambda l, j: (0, 0)),
            pl.BlockSpec((None, D, tn), lambda l, j: (l, 0, j)),
            pl.BlockSpec((None, 1, tn), lambda l, j: (l, 0, j)),
        ],
        out_specs=pl.BlockSpec((None, cvec.shape[0], tn), lambda l, j: (l, 0, j)),
        out_shape=jax.ShapeDtypeStruct((L, cvec.shape[0], N), F32),
        compiler_params=_cparams(("parallel", "parallel")),
        name="adaln",
    )(cvec, w_ada, b_ada.reshape(L, 1, N))


def _ffn_kernel(x_ref, mod_ref, g_ref, w1_ref, w2_ref, o_ref, *, k0, n_chunks):
    x = x_ref[...]
    h = _norm_mod(x, g_ref[...], mod_ref[k0:k0 + 1, :], mod_ref[k0 + 1:k0 + 2, :]).astype(BF16)
    fc = D_FF // n_chunks
    acc = jnp.zeros(x.shape, F32)
    for c in range(n_chunks):
        a = jnp.dot(h, w1_ref[:, c * fc:(c + 1) * fc], preferred_element_type=F32)
        b = jnp.dot(h, w1_ref[:, D_FF + c * fc:D_FF + (c + 1) * fc], preferred_element_type=F32)
        u = (a * _sigmoid(a) * b).astype(BF16)
        acc = acc + jnp.dot(u, w2_ref[c * fc:(c + 1) * fc, :], preferred_element_type=F32)
    o_ref[...] = x + 0.5 * mod_ref[k0 + 2:k0 + 3, :] * acc


def _mod_spec():
    return pl.BlockSpec((None, None, 9, D_MODEL), lambda b, t: (b, jnp.minimum(t, 1), 0, 0))


def _ffn_call(x, modtab, g, w1, w2, l, i, k0):
    B, T, D = x.shape
    kern = functools.partial(_ffn_kernel, k0=k0, n_chunks=2)
    return pl.pallas_call(
        kern,
        grid=(B, T // TT),
        in_specs=[
            pl.BlockSpec((None, TT, D), lambda b, t: (b, t, 0)),
            _mod_spec(),
            pl.BlockSpec((1, D), lambda b, t: (0, 0)),
            pl.BlockSpec((None, None, D, 2 * D_FF), lambda b, t: (l, i, 0, 0)),
            pl.BlockSpec((None, None, D_FF, D), lambda b, t: (l, i, 0, 0)),
        ],
        out_specs=pl.BlockSpec((None, TT, D), lambda b, t: (b, t, 0)),
        out_shape=jax.ShapeDtypeStruct(x.shape, F32),
        compiler_params=_cparams(("parallel", "parallel")),
        name="ffn_half",
    )(x, modtab, g, w1, w2)


def _inproj_kernel(x_ref, mod_ref, g_ref, w_ref, cos_ref, sin_ref,
                   mlf_ref, dfq_ref, dfk_ref, dfv_ref, naq_ref, nak_ref, nav_ref, mlg_ref):
    x = x_ref[...]
    h = _norm_mod(x, g_ref[...], mod_ref[3:4, :], mod_ref[4:5, :]).astype(BF16)

    def proj(c0, n):
        return jnp.dot(h, w_ref[:, c0:c0 + n], preferred_element_type=F32)

    mlf_ref[...] = proj(0, 4 * ML_W)
    cos = jnp.tile(cos_ref[...], (1, DF_HEADS))
    sin = jnp.tile(sin_ref[...], (1, DF_HEADS))
    lane = lax.broadcasted_iota(jnp.int32, (TT, DF_W), 1)
    first_half = (lane % DF_DQK) < (DF_DQK // 2)

    def rope(v):
        partner = jnp.where(first_half, pltpu.roll(v, DF_W - DF_DQK // 2, 1),
                            pltpu.roll(v, DF_DQK // 2, 1))
        return v * cos + partner * sin

    c0 = 4 * ML_W
    dfq_ref[...] = (rope(proj(c0, DF_W)) * DF_DQK ** -0.5).astype(BF16)
    dfk_ref[...] = rope(proj(c0 + DF_W, DF_W)).astype(BF16)
    dfv_ref[...] = proj(c0 + 2 * DF_W, DF_W).astype(BF16)
    c0 = c0 + 3 * DF_W
    naq_ref[...] = (proj(c0, NA_W) * NA_DH ** -0.5).astype(BF16)
    nak_ref[...] = proj(c0 + NA_W, NA_W).astype(BF16)
    nav_ref[...] = proj(c0 + 2 * NA_W, NA_W).astype(BF16)
    mlg_ref[...] = proj(N_MAIN, GATE_PAD)


def _inproj_call(x, modtab, g, w_proj, cos_t, sin_t, l):
    B, T, D = x.shape
    NP = w_proj.shape[-1]

    def tile_spec(n):
        return pl.BlockSpec((None, TT, n), lambda b, t: (b, t, 0))

    def sds(n, dt):
        return jax.ShapeDtypeStruct((B, T, n), dt)

    return pl.pallas_call(
        _inproj_kernel,
        grid=(B, T // TT),
        in_specs=[
            tile_spec(D),
            _mod_spec(),
            pl.BlockSpec((1, D), lambda b, t: (0, 0)),
            pl.BlockSpec((None, D, NP), lambda b, t: (l, 0, 0)),
            pl.BlockSpec((TT, 2 * DF_DQK), lambda b, t: (t, 0)),
            pl.BlockSpec((TT, 2 * DF_DQK), lambda b, t: (t, 0)),
        ],
        out_specs=[tile_spec(4 * ML_W)] + [tile_spec(DF_W)] * 6 + [tile_spec(GATE_PAD)],
        out_shape=[sds(4 * ML_W, F32)] + [sds(DF_W, BF16)] * 6 + [sds(GATE_PAD, F32)],
        compiler_params=_cparams(("parallel", "parallel")),
        name="in_proj",
    )(x, modtab, g, w_proj, cos_t, sin_t)


def _conv_kernel(x_ref, w_ref, b_ref, o_ref, *, n_ctx):
    x = x_ref[...]
    T = x.shape[0]
    row = lax.broadcasted_iota(jnp.int32, x.shape, 0)
    prev = jnp.where((row == 0) | (row == n_ctx), 0.0, pltpu.roll(x, 1, 0))
    nxt = jnp.where((row == n_ctx - 1) | (row == T - 1), 0.0, pltpu.roll(x, T - 1, 0))
    y = prev * w_ref[0:1, :] + x * w_ref[1:2, :] + nxt * w_ref[2:3, :] + b_ref[...]
    y = y * _sigmoid(y)
    scale = jnp.where(pl.program_id(1) < ML_W // x.shape[1], ML_DH ** -0.5, 1.0)
    o_ref[...] = (y * scale).astype(BF16)


def _conv_call(mlf, conv_w, conv_b, n_ctx):
    B, T, _ = mlf.shape
    ct = 256
    return pl.pallas_call(
        functools.partial(_conv_kernel, n_ctx=n_ctx),
        grid=(B, 2 * ML_W // ct),
        in_specs=[
            pl.BlockSpec((None, T, ct), lambda b, c: (b, 0, c)),
            pl.BlockSpec((3, ct), lambda b, c: (0, c)),
            pl.BlockSpec((1, ct), lambda b, c: (0, c)),
        ],
        out_specs=pl.BlockSpec((None, T, ct), lambda b, c: (b, 0, c)),
        out_shape=jax.ShapeDtypeStruct((B, T, 2 * ML_W), BF16),
        compiler_params=_cparams(("parallel", "parallel")),
        name="ml_conv",
    )(mlf, conv_w, conv_b)


def _log_sigmoid(x):
    return jnp.minimum(x, 0.0) - jnp.log1p(jnp.exp(-jnp.abs(x)))


def _mlstm_kernel(qkf_ref, qkb_ref, vf_ref, vb_ref, gf_ref, gb_ref, gtf_ref, gtb_ref,
                  gbias_ref, gbias_t_ref, hf_ref, hb_ref, c_scr, n_scr, m_scr):
    L = TT

    @pl.when(pl.program_id(1) == 0)
    def _():
        c_scr[...] = jnp.zeros(c_scr.shape, F32)
        n_scr[...] = jnp.zeros(n_scr.shape, F32)
        m_scr[...] = jnp.zeros(m_scr.shape, F32)

    row = lax.broadcasted_iota(jnp.int32, (L, L), 0)
    col = lax.broadcasted_iota(jnp.int32, (L, L), 1)
    lower = row >= col
    upper = row <= col
    lower_f = lower.astype(F32)
    upper_f = upper.astype(F32)

    def hdot(a, b):
        return jnp.dot(a, b, preferred_element_type=F32, precision=lax.Precision.HIGHEST)

    dirs = ((qkf_ref, vf_ref, gf_ref, gtf_ref, hf_ref), (qkb_ref, vb_ref, gb_ref, gtb_ref, hb_ref))
    for d, (qk_ref, v_ref, g_ref, gt_ref, h_ref) in enumerate(dirs):
        g = g_ref[:, 0:GATE_W] + gbias_ref[...]
        gt = gt_ref[...] + gbias_t_ref[...]
        lf = _log_sigmoid(g)
        lft = _log_sigmoid(gt)
        if d == 0:
            mask = lower
            bcol_all = hdot(lower_f, lf)
            brow_all = hdot(lft, upper_f)
        else:
            mask = upper
            bcol_all = hdot(upper_f, lf)
            brow_all = hdot(lft, lower_f)
        for hd in range(ML_HEADS):
            ci = d * 2 * ML_HEADS + hd
            cf = ci + ML_HEADS
            si = d * ML_HEADS + hd
            q = qk_ref[:, hd * ML_DH:(hd + 1) * ML_DH]
            k = qk_ref[:, ML_W + hd * ML_DH:ML_W + (hd + 1) * ML_DH]
            v = v_ref[:, hd * ML_DH:(hd + 1) * ML_DH]
            li_row = gt[ci:ci + 1, :]
            li_col = g[:, ci:ci + 1]
            b_col = bcol_all[:, cf:cf + 1]
            b_row = brow_all[cf:cf + 1, :]
            btot = b_col[L - 1:L, :] if d == 0 else b_col[0:1, :]
            m = m_scr[si, 0:1, 0:1]
            cmat = c_scr[si]
            nvec = n_scr[si]

            dmat = jnp.where(mask, b_col + (li_row - b_row), -jnp.inf)
            inter = b_col + m
            m_t = jnp.maximum(inter, jnp.max(dmat, axis=1, keepdims=True))
            w = jnp.exp(dmat - m_t)
            s_inter = jnp.exp(inter - m_t)
            s = lax.dot_general(q, k, (((1,), (1,)), ((), ())), preferred_element_type=F32)
            qkw = s * w
            qf = q.astype(F32)
            num = (jnp.dot(qkw.astype(BF16), v.astype(BF16), preferred_element_type=F32)
                   + s_inter * jnp.dot(q, cmat.astype(BF16), preferred_element_type=F32))
            den = (jnp.sum(qkw, axis=1, keepdims=True)
                   + s_inter * jnp.sum(qf * nvec, axis=1, keepdims=True))
            h_ref[:, hd * ML_DH:(hd + 1) * ML_DH] = num / jnp.maximum(jnp.abs(den), jnp.exp(-m_t))

            wk = btot - b_col + li_col
            m_new = jnp.maximum(btot + m, jnp.max(wk, axis=0, keepdims=True))
            decay = jnp.exp(btot + m - m_new)
            wk = jnp.exp(wk - m_new)
            kw = wk * k.astype(F32)
            c_scr[si] = decay * cmat + jnp.dot(kw.T.astype(BF16), v.astype(BF16),
                                               preferred_element_type=F32)
            n_scr[si] = decay * nvec + jnp.sum(kw, axis=0, keepdims=True)
            m_scr[si] = jnp.broadcast_to(m_new, m_scr.shape[1:])


def _mlstm_call(qk, mlf, mlg, mlg_t, gate_b):
    B, T, _ = qk.shape
    nt = T // TT

    def fwd(b, t):
        return (b, t, 0)

    def bwd_t(t):
        return jnp.where(t == 0, 0, nt - t)

    def spec(n, cblk, back):
        if back:
            return pl.BlockSpec((None, TT, n), lambda b, t: (b, bwd_t(t), cblk))
        return pl.BlockSpec((None, TT, n), lambda b, t: (b, t, cblk))

    def tspec(back):
        if back:
            return pl.BlockSpec((None, GATE_W, TT), lambda b, t: (b, 0, bwd_t(t)))
        return pl.BlockSpec((None, GATE_W, TT), lambda b, t: (b, 0, t))

    nch = 2 * ML_HEADS
    out_sds = jax.ShapeDtypeStruct((B, T, ML_W), F32)
    return pl.pallas_call(
        _mlstm_kernel,
        grid=(B, nt),
        in_specs=[
            spec(2 * ML_W, 0, False), spec(2 * ML_W, 0, True),
            spec(ML_W, 2, False), spec(ML_W, 2, True),
            spec(GATE_PAD, 0, False), spec(GATE_PAD, 0, True),
            tspec(False), tspec(True),
            pl.BlockSpec((1, GATE_W), lambda b, t: (0, 0)),
            pl.BlockSpec((GATE_W, 1), lambda b, t: (0, 0)),
        ],
        out_specs=[spec(ML_W, 0, False), spec(ML_W, 0, True)],
        out_shape=[out_sds, out_sds],
        scratch_shapes=[
            pltpu.VMEM((nch, ML_DH, ML_DH), F32),
            pltpu.VMEM((nch, 1, ML_DH), F32),
            pltpu.VMEM((nch, 8, 128), F32),
        ],
        compiler_params=_cparams(("parallel", "arbitrary")),
        name="mlstm_scan",
    )(qk, qk, mlf, mlf, mlg, mlg, mlg_t, mlg_t, gate_b.reshape(1, GATE_W), gate_b.reshape(GATE_W, 1))


def _diff_kernel(q_ref, k_ref, v_ref, lam_ref, g_ref, o_ref, *, lam_init, n_ctx):
    lv = lam_ref[...]
    lam = (jnp.exp(jnp.sum(lv[0:1, :] * lv[1:2, :], axis=1, keepdims=True))
           - jnp.exp(jnp.sum(lv[2:3, :] * lv[3:4, :], axis=1, keepdims=True)) + lam_init)
    q = q_ref[...]
    lane = lax.broadcasted_iota(jnp.int32, q.shape, 1)
    zero = jnp.zeros(q.shape, q.dtype)
    q1 = jnp.where(lane < DF_DQK, q, zero)
    q2 = jnp.where(lane < DF_DQK, zero, q)

    def attend(nk):
        k = k_ref[0:nk, :]
        v = v_ref[0:nk, :]
        dn = (((1,), (1,)), ((), ()))
        s1 = lax.dot_general(q1, k, dn, preferred_element_type=F32)
        s2 = lax.dot_general(q2, k, dn, preferred_element_type=F32)
        e1 = jnp.exp(s1 - jnp.max(s1, axis=1, keepdims=True))
        e2 = jnp.exp(s2 - jnp.max(s2, axis=1, keepdims=True))
        r1 = 1.0 / jnp.sum(e1, axis=1, keepdims=True)
        r2 = lam / jnp.sum(e2, axis=1, keepdims=True)
        p = e1 * r1 - e2 * r2
        o = jnp.dot(p.astype(BF16), v, preferred_element_type=F32)
        var = jnp.mean(o * o, axis=1, keepdims=True)
        o_ref[...] = (o * lax.rsqrt(var + EPS) * g_ref[...] * (1.0 - lam_init)).astype(o_ref.dtype)

    @pl.when(pl.program_id(2) == 0)
    def _():
        attend(n_ctx)

    @pl.when(pl.program_id(2) != 0)
    def _():
        attend(k_ref.shape[0])


def _diff_call(dfq, dfk, dfv, lam_vecs, norm_g, lam_init, n_ctx):
    B, T, _ = dfq.shape
    return pl.pallas_call(
        functools.partial(_diff_kernel, lam_init=lam_init, n_ctx=n_ctx),
        grid=(B, DF_HEADS, T // TT),
        in_specs=[
            pl.BlockSpec((None, TT, DF_DV), lambda b, h, t: (b, t, h)),
            pl.BlockSpec((None, T, DF_DV), lambda b, h, t: (b, 0, h)),
            pl.BlockSpec((None, T, DF_DV), lambda b, h, t: (b, 0, h)),
            pl.BlockSpec((4, DF_DQK), lambda b, h, t: (0, 0)),
            pl.BlockSpec((1, DF_DV), lambda b, h, t: (0, h)),
        ],
        out_specs=pl.BlockSpec((None, TT, DF_DV), lambda b, h, t: (b, t, h)),
        out_shape=jax.ShapeDtypeStruct((B, T, DF_W), BF16),
        compiler_params=_cparams(("parallel", "parallel", "parallel")),
        name="diff_attn",
    )(dfq, dfk, dfv, lam_vecs, norm_g.reshape(1, DF_W))


def _na_tables(rows):
    kh = min(NA_KH, rows)
    assert kh == NA_KH and rows >= NA_KROWS and rows % NA_QROWS == 0
    n_tiles = rows // NA_QROWS
    qr = np.arange(TT) // GRID_W
    qc = np.arange(TT) % GRID_W
    kr = np.arange(NA_KROWS * GRID_W) // GRID_W
    kc = np.arange(NA_KROWS * GRID_W) % GRID_W
    qstart = np.clip(qc - NA_KW // 2, 0, GRID_W - NA_KW)
    col_ok = (kc[None, :] >= qstart[:, None]) & (kc[None, :] < qstart[:, None] + NA_KW)
    dc = np.clip(kc[None, :] - qc[:, None] + NA_KW - 1, 0, 2 * NA_KW - 2)
    drs, valids = [], []
    for tile in (0, 1, n_tiles - 1):
        r0 = tile * NA_QROWS
        start = int(np.clip(r0 - kh // 2, 0, rows - NA_KROWS))
        r = r0 + qr
        rs = np.clip(r - kh // 2, 0, rows - kh)
        akr = start + kr
        row_ok = (akr[None, :] >= rs[:, None]) & (akr[None, :] < rs[:, None] + kh)
        dr = akr[None, :] - r[:, None] + NA_KH - 1
        valids.append(row_ok & col_ok)
        drs.append(np.clip(dr, 0, 2 * NA_KH - 2))
    return np.stack(drs), np.broadcast_to(dc, (3,) + dc.shape), np.stack(valids)


def _na_kernel(q_ref, k_ref, v_ref, bias_ref, o_ref, *, n_ctx, rows):
    t = pl.program_id(1)
    dn = (((1,), (1,)), ((), ()))

    @pl.when(t == 0)
    def _():
        for hd in range(NA_HEADS):
            sl = slice(hd * NA_DH, (hd + 1) * NA_DH)
            q = q_ref[:, sl]
            s = lax.dot_general(q, k_ref[0:n_ctx, sl], dn, preferred_element_type=F32)
            e = jnp.exp(s - jnp.max(s, axis=1, keepdims=True))
            o = jnp.dot(e.astype(BF16), v_ref[0:n_ctx, sl], preferred_element_type=F32)
            o_ref[:, sl] = (o / jnp.sum(e, axis=1, keepdims=True)).astype(o_ref.dtype)

    @pl.when(t != 0)
    def _():
        r0 = (t - 1) * NA_QROWS
        start = jnp.clip(r0 - NA_KH // 2, 0, rows - NA_KROWS)
        tok0 = pl.multiple_of(n_ctx + start * GRID_W, GRID_W)
        nk = NA_KROWS * GRID_W
        for hd in range(NA_HEADS):
            sl = slice(hd * NA_DH, (hd + 1) * NA_DH)
            q = q_ref[:, sl]
            s_lat = lax.dot_general(q, k_ref[pl.ds(tok0, nk), sl], dn,
                                    preferred_element_type=F32) + bias_ref[hd]
            s_ctx = lax.dot_general(q, k_ref[0:n_ctx, sl], dn, preferred_element_type=F32)
            m = jnp.maximum(jnp.max(s_lat, axis=1, keepdims=True),
                            jnp.max(s_ctx, axis=1, keepdims=True))
            e_lat = jnp.exp(s_lat - m)
            e_ctx = jnp.exp(s_ctx - m)
            den = jnp.sum(e_lat, axis=1, keepdims=True) + jnp.sum(e_ctx, axis=1, keepdims=True)
            o = (jnp.dot(e_lat.astype(BF16), v_ref[pl.ds(tok0, nk), sl], preferred_element_type=F32)
                 + jnp.dot(e_ctx.astype(BF16), v_ref[0:n_ctx, sl], preferred_element_type=F32))
            o_ref[:, sl] = (o / den).astype(o_ref.dtype)


def _na_call(naq, nak, nav, bias_tab, n_ctx):
    B, T, _ = naq.shape
    nt = T // TT
    rows = (T - n_ctx) // GRID_W
    nk = NA_KROWS * GRID_W

    def bias_idx(b, t):
        return (jnp.where(t <= 1, 0, jnp.where(t == nt - 1, 2, 1)), 0, 0, 0)

    return pl.pallas_call(
        functools.partial(_na_kernel, n_ctx=n_ctx, rows=rows),
        grid=(B, nt),
        in_specs=[
            pl.BlockSpec((None, TT, NA_W), lambda b, t: (b, t, 0)),
            pl.BlockSpec((None, T, NA_W), lambda b, t: (b, 0, 0)),
            pl.BlockSpec((None, T, NA_W), lambda b, t: (b, 0, 0)),
            pl.BlockSpec((None, NA_HEADS, TT, nk), bias_idx),
        ],
        out_specs=pl.BlockSpec((None, TT, NA_W), lambda b, t: (b, t, 0)),
        out_shape=jax.ShapeDtypeStruct((B, T, NA_W), BF16),
        compiler_params=_cparams(("parallel", "arbitrary")),
        name="na_attn",
    )(naq, nak, nav, bias_tab)


def _merge_kernel(x_ref, mod_ref, g_ref, wg_ref, wb_ref, wo_ref, hf_ref, hb_ref, o_ref_in,
                  mlg_ref, bd_ref, cn_ref, out_ref):
    x = x_ref[...]
    D = x.shape[1]
    hn = _norm_mod(x, g_ref[...], mod_ref[3:4, :], mod_ref[4:5, :]).astype(BF16)
    hsum = hf_ref[...] + hb_ref[...]
    parts = []
    for hd in range(ML_HEADS):
        hh = hsum[:, hd * ML_DH:(hd + 1) * ML_DH]
        parts.append(hh * lax.rsqrt(jnp.mean(hh * hh, axis=1, keepdims=True) + EPS))
    a = (_sigmoid(o_ref_in[...]) * (jnp.concatenate(parts, axis=1) * mlg_ref[...])).astype(BF16)
    y = None
    for i, br in enumerate((a, bd_ref[...], cn_ref[...])):
        gate = _sigmoid(jnp.dot(hn, wg_ref[:, i * D:(i + 1) * D], preferred_element_type=F32))
        term = gate * jnp.dot(br, wb_ref[i], preferred_element_type=F32)
        y = term if y is None else y + term
    z = jnp.dot(y.astype(BF16), wo_ref[...], preferred_element_type=F32)
    out_ref[...] = x + mod_ref[5:6, :] * z


def _merge_call(x, modtab, g, w_gate, w_branch, w_out, hf, hb, mlf, ml_norm_g, bd, cn, l):
    B, T, D = x.shape

    def tile_spec(n, cblk=0):
        return pl.BlockSpec((None, TT, n), lambda b, t: (b, t, cblk))

    return pl.pallas_call(
        _merge_kernel,
        grid=(B, T // TT),
        in_specs=[
            tile_spec(D),
            _mod_spec(),
            pl.BlockSpec((1, D), lambda b, t: (0, 0)),
            pl.BlockSpec((None, D, N_BRANCH * D), lambda b, t: (l, 0, 0)),
            pl.BlockSpec((None, N_BRANCH, BRANCH_W, D), lambda b, t: (l, 0, 0, 0)),
            pl.BlockSpec((None, D, D), lambda b, t: (l, 0, 0)),
            tile_spec(ML_W), tile_spec(ML_W),
            tile_spec(ML_W, 3),
            pl.BlockSpec((1, ML_W), lambda b, t: (0, 0)),
            tile_spec(DF_W), tile_spec(NA_W),
        ],
        out_specs=tile_spec(D),
        out_shape=jax.ShapeDtypeStruct(x.shape, F32),
        compiler_params=_cparams(("parallel", "parallel")),
        name="merge",
    )(x, modtab, g, w_gate, w_branch, w_out, hf, hb, mlf, ml_norm_g, bd, cn)


def _final_kernel(x_ref, g_ref, o_ref):
    x = x_ref[...]
    o_ref[...] = x * lax.rsqrt(jnp.mean(x * x, axis=-1, keepdims=True) + EPS) * g_ref[...]


def _final_call(x, g, n_ctx):
    B, T, D = x.shape
    S = T - n_ctx
    off = n_ctx // TT
    return pl.pallas_call(
        _final_kernel,
        grid=(B, S // TT),
        in_specs=[
            pl.BlockSpec((None, TT, D), lambda b, t: (b, t + off, 0)),
            pl.BlockSpec((1, D), lambda b, t: (0, 0)),
        ],
        out_specs=pl.BlockSpec((None, TT, D), lambda b, t: (b, t, 0)),
        out_shape=jax.ShapeDtypeStruct((B, S, D), F32),
        compiler_params=_cparams(("parallel", "parallel")),
        name="final_norm",
    )(x, g)


def _rope_tables(S, n_ctx):
    t = jnp.arange(S)
    rowp = (t // GRID_W).astype(F32)
    colp = (t % GRID_W).astype(F32)
    half = DF_DQK // 2
    freqs = ROPE_BASE ** (-jnp.arange(0, half, 2, dtype=F32) / half)
    ang = jnp.concatenate([rowp[:, None] * freqs, colp[:, None] * freqs], axis=-1)
    cos, sin = jnp.cos(ang), jnp.sin(ang)
    cos = jnp.concatenate([jnp.ones((n_ctx, half), F32), cos], axis=0)
    sin = jnp.concatenate([jnp.zeros((n_ctx, half), F32), sin], axis=0)
    cos_t = jnp.tile(cos, (1, 4))
    sin_t = jnp.tile(jnp.concatenate([-sin, sin], axis=1), (1, 2))
    return cos_t, sin_t


def kernel(x, c, ctx, c_ctx, w_ada, b_ada, norm_g, ffn_w1, ffn_w2, w_in, ml_conv_w, ml_conv_b,
           ml_gate_b, ml_norm_g, df_lambda, df_norm_g, na_rel_bias, w_branch, w_out, final_g):
    B, S, D = x.shape
    n_ctx = ctx.shape[1]
    depth = w_ada.shape[0]
    assert D == D_MODEL and n_ctx == TT and S % TT == 0 and S % GRID_W == 0
    T = n_ctx + S
    rows = S // GRID_W

    w1 = ffn_w1.astype(BF16)
    w2 = ffn_w2.astype(BF16)
    w_proj = jnp.concatenate(
        [w_in[:, :, OFF_ML:OFF_MLG], w_in[:, :, OFF_DF:OFF_GATE],
         jnp.pad(w_in[:, :, OFF_MLG:OFF_DF], ((0, 0), (0, 0), (0, GATE_PAD - GATE_W)))],
        axis=-1).astype(BF16)
    w_gate = w_in[:, :, OFF_GATE:].astype(BF16)
    wb = w_branch.astype(BF16)
    wo = w_out.astype(BF16)

    cvec = jnp.zeros((16, D), F32).at[0].set(c_ctx).at[1:B + 1].set(c)
    mod = _ada_call(cvec, w_ada, b_ada).reshape(depth, 16, 9, D)
    modtab = jnp.stack([jnp.broadcast_to(mod[:, 0:1], (depth, B, 9, D)), mod[:, 1:B + 1]], axis=2)

    cos_t, sin_t = _rope_tables(S, n_ctx)
    dr, dc, valid = _na_tables(rows)

    h = jnp.concatenate([ctx, x], axis=1)
    for l in range(depth):
        lam_init = 0.8 - 0.6 * math.exp(-0.3 * l)
        mt = modtab[l]
        h = _ffn_call(h, mt, norm_g[l, 0][None], w1, w2, l, 0, 0)
        mlf, dfq, dfk, dfv, naq, nak, nav, mlg = _inproj_call(h, mt, norm_g[l, 1][None], w_proj,
                                                              cos_t, sin_t, l)
        qk = _conv_call(mlf, ml_conv_w[l], ml_conv_b[l][None], n_ctx)
        mlg_t = jnp.transpose(mlg[:, :, :GATE_W], (0, 2, 1))
        hf, hb = _mlstm_call(qk, mlf, mlg, mlg_t, ml_gate_b[l])
        bd = _diff_call(dfq, dfk, dfv, df_lambda[l], df_norm_g[l], lam_init, n_ctx)
        bias_tab = jnp.where(valid[:, None], na_rel_bias[l][:, dr, dc].transpose(1, 0, 2, 3), NEG)
        cn = _na_call(naq, nak, nav, bias_tab.astype(F32), n_ctx)
        h = _merge_call(h, mt, norm_g[l, 1][None], w_gate, wb, wo, hf, hb, mlf,
                        ml_norm_g[l][None], bd, cn, l)
        h = _ffn_call(h, mt, norm_g[l, 2][None], w1, w2, l, 1, 6)
    return _final_call(h, final_g[None], n_ctx)
```

```python
import functools
import math

import numpy as np
import jax
import jax.numpy as jnp
from jax import lax
from jax.experimental import pallas as pl
from jax.experimental.pallas import tpu as pltpu

F32 = jnp.float32
BF16 = jnp.bfloat16

D_MODEL = 1024
GRID_W = 64
N_BRANCH = 3
BRANCH_W = 512
ML_HEADS = 4
ML_DH = 128
ML_W = ML_HEADS * ML_DH
DF_HEADS = 4
DF_DQK = 64
DF_DV = 128
DF_W = DF_HEADS * DF_DV
ROPE_BASE = 10000.0
NA_HEADS = 8
NA_DH = 64
NA_W = NA_HEADS * NA_DH
NA_KH = 8
NA_KW = 16
D_FF = 2816
EPS = 1e-6
GATE_W = 4 * ML_HEADS
GATE_PAD = 128
OFF_ML = 0
OFF_MLG = 4 * ML_W
OFF_DF = OFF_MLG + GATE_W
OFF_GATE = OFF_DF + 3 * DF_W + 3 * NA_W
N_MAIN = 4 * ML_W + 3 * DF_W + 3 * NA_W

TT = 256
NA_QROWS = 4
NA_KROWS = 12
NEG = -1e30
LOG2E = math.log2(math.e)
VMEM_LIMIT = 56 * 1024 * 1024


def _cparams(sem):
    return pltpu.CompilerParams(dimension_semantics=sem, vmem_limit_bytes=VMEM_LIMIT)


def _norm_mod(x, g, shift, scale):
    var = jnp.mean(x * x, axis=-1, keepdims=True)
    return x * lax.rsqrt(var + EPS) * g * (1.0 + scale) + shift


def _sigmoid(x):
    return 1.0 / (1.0 + jnp.exp(-x))


def _ada_kernel(c_ref, w_ref, b_ref, o_ref):
    c = c_ref[...]
    h = (c * _sigmoid(c)).astype(BF16)
    o_ref[...] = jnp.dot(h, w_ref[...].astype(BF16), preferred_element_type=F32) + b_ref[...]


def _ada_call(cvec, w_ada, b_ada):
    L, D, N = w_ada.shape
    tn = N // 8
    return pl.pallas_call(
        _ada_kernel,
        grid=(L, N // tn),
        in_specs=[
            pl.BlockSpec(cvec.shape, lambda l, j: (0, 0)),
            pl.BlockSpec((None, D, tn), lambda l, j: (l, 0, j)),
            pl.BlockSpec((None, 1, tn), lambda l, j: (l, 0, j)),
        ],
        out_specs=pl.BlockSpec((None, cvec.shape[0], tn), lambda l, j: (l, 0, j)),
        out_shape=jax.ShapeDtypeStruct((L, cvec.shape[0], N), F32),
        compiler_params=_cparams(("parallel", "parallel")),
        name="adaln",
    )(cvec, w_ada, b_ada.reshape(L, 1, N))


def _ffn_kernel(x_ref, mod_ref, g_ref, w1_ref, w2_ref, o_ref, *, k0, n_chunks):
    x = x_ref[...]
    h = _norm_mod(x, g_ref[...], mod_ref[k0:k0 + 1, :], mod_ref[k0 + 1:k0 + 2, :]).astype(BF16)
    fc = D_FF // n_chunks
    acc = jnp.zeros(x.shape, F32)
    for c in range(n_chunks):
        a = jnp.dot(h, w1_ref[:, c * fc:(c + 1) * fc], preferred_element_type=F32)
        b = jnp.dot(h, w1_ref[:, D_FF + c * fc:D_FF + (c + 1) * fc], preferred_element_type=F32)
        u = (a * _sigmoid(a) * b).astype(BF16)
        acc = acc + jnp.dot(u, w2_ref[c * fc:(c + 1) * fc, :], preferred_element_type=F32)
    o_ref[...] = x + 0.5 * mod_ref[k0 + 2:k0 + 3, :] * acc


def _mod_spec():
    return pl.BlockSpec((None, None, 9, D_MODEL), lambda b, t: (b, jnp.minimum(t, 1), 0, 0))


def _ffn_call(x, modtab, g, w1, w2, l, i, k0):
    B, T, D = x.shape
    kern = functools.partial(_ffn_kernel, k0=k0, n_chunks=2)
    return pl.pallas_call(
        kern,
        grid=(B, T // TT),
        in_specs=[
            pl.BlockSpec((None, TT, D), lambda b, t: (b, t, 0)),
            _mod_spec(),
            pl.BlockSpec((1, D), lambda b, t: (0, 0)),
            pl.BlockSpec((None, None, D, 2 * D_FF), lambda b, t: (l, i, 0, 0)),
            pl.BlockSpec((None, None, D_FF, D), lambda b, t: (l, i, 0, 0)),
        ],
        out_specs=pl.BlockSpec((None, TT, D), lambda b, t: (b, t, 0)),
        out_shape=jax.ShapeDtypeStruct(x.shape, F32),
        compiler_params=_cparams(("parallel", "parallel")),
        name="ffn_half",
    )(x, modtab, g, w1, w2)


def _inproj_kernel(x_ref, mod_ref, g_ref, w_ref, cos_ref, sin_ref,
                   mlf_ref, dfq_ref, dfk_ref, dfv_ref, naq_ref, nak_ref, nav_ref, mlg_ref):
    x = x_ref[...]
    h = _norm_mod(x, g_ref[...], mod_ref[3:4, :], mod_ref[4:5, :]).astype(BF16)

    def proj(c0, n):
        return jnp.dot(h, w_ref[:, c0:c0 + n], preferred_element_type=F32)

    mlf_ref[...] = proj(0, 4 * ML_W)
    cos = jnp.tile(cos_ref[...], (1, DF_HEADS))
    sin = jnp.tile(sin_ref[...], (1, DF_HEADS))
    lane = lax.broadcasted_iota(jnp.int32, (TT, DF_W), 1)
    first_half = (lane % DF_DQK) < (DF_DQK // 2)

    def rope(v):
        partner = jnp.where(first_half, pltpu.roll(v, DF_W - DF_DQK // 2, 1),
                            pltpu.roll(v, DF_DQK // 2, 1))
        return v * cos + partner * sin

    c0 = 4 * ML_W
    dfq_ref[...] = (rope(proj(c0, DF_W)) * (DF_DQK ** -0.5 * LOG2E)).astype(BF16)
    dfk_ref[...] = rope(proj(c0 + DF_W, DF_W)).astype(BF16)
    dfv_ref[...] = proj(c0 + 2 * DF_W, DF_W).astype(BF16)
    c0 = c0 + 3 * DF_W
    naq_ref[...] = (proj(c0, NA_W) * NA_DH ** -0.5).astype(BF16)
    nak_ref[...] = proj(c0 + NA_W, NA_W).astype(BF16)
    nav_ref[...] = proj(c0 + 2 * NA_W, NA_W).astype(BF16)
    mlg_ref[...] = proj(N_MAIN, GATE_PAD)


def _inproj_call(x, modtab, g, w_proj, cos_t, sin_t, l):
    B, T, D = x.shape
    NP = w_proj.shape[-1]

    def tile_spec(n):
        return pl.BlockSpec((None, TT, n), lambda b, t: (b, t, 0))

    def sds(n, dt):
        return jax.ShapeDtypeStruct((B, T, n), dt)

    return pl.pallas_call(
        _inproj_kernel,
        grid=(B, T // TT),
        in_specs=[
            tile_spec(D),
            _mod_spec(),
            pl.BlockSpec((1, D), lambda b, t: (0, 0)),
            pl.BlockSpec((None, D, NP), lambda b, t: (l, 0, 0)),
            pl.BlockSpec((TT, 2 * DF_DQK), lambda b, t: (t, 0)),
            pl.BlockSpec((TT, 2 * DF_DQK), lambda b, t: (t, 0)),
        ],
        out_specs=[tile_spec(4 * ML_W)] + [tile_spec(DF_W)] * 6 + [tile_spec(GATE_PAD)],
        out_shape=[sds(4 * ML_W, F32)] + [sds(DF_W, BF16)] * 6 + [sds(GATE_PAD, F32)],
        compiler_params=_cparams(("parallel", "parallel")),
        name="in_proj",
    )(x, modtab, g, w_proj, cos_t, sin_t)


def _conv_kernel(x_ref, w_ref, b_ref, o_ref, *, n_ctx):
    x = x_ref[...]
    T = x.shape[0]
    row = lax.broadcasted_iota(jnp.int32, x.shape, 0)
    prev = jnp.where((row == 0) | (row == n_ctx), 0.0, pltpu.roll(x, 1, 0))
    nxt = jnp.where((row == n_ctx - 1) | (row == T - 1), 0.0, pltpu.roll(x, T - 1, 0))
    y = prev * w_ref[0:1, :] + x * w_ref[1:2, :] + nxt * w_ref[2:3, :] + b_ref[...]
    y = y * _sigmoid(y)
    scale = jnp.where(pl.program_id(1) < ML_W // x.shape[1], ML_DH ** -0.5, 1.0)
    o_ref[...] = (y * scale).astype(BF16)


def _conv_call(mlf, conv_w, conv_b, n_ctx):
    B, T, _ = mlf.shape
    ct = 256
    return pl.pallas_call(
        functools.partial(_conv_kernel, n_ctx=n_ctx),
        grid=(B, 2 * ML_W // ct),
        in_specs=[
            pl.BlockSpec((None, T, ct), lambda b, c: (b, 0, c)),
            pl.BlockSpec((3, ct), lambda b, c: (0, c)),
            pl.BlockSpec((1, ct), lambda b, c: (0, c)),
        ],
        out_specs=pl.BlockSpec((None, T, ct), lambda b, c: (b, 0, c)),
        out_shape=jax.ShapeDtypeStruct((B, T, 2 * ML_W), BF16),
        compiler_params=_cparams(("parallel", "parallel")),
        name="ml_conv",
    )(mlf, conv_w, conv_b)


def _log_sigmoid(x):
    return jnp.minimum(x, 0.0) - jnp.log1p(jnp.exp(-jnp.abs(x)))


def _mlstm_kernel(qkf_ref, qkb_ref, vf_ref, vb_ref, gf_ref, gb_ref, gtf_ref, gtb_ref,
                  gbias_ref, gbias_t_ref, hf_ref, hb_ref, c_scr, n_scr, m_scr):
    L = TT

    @pl.when(pl.program_id(1) == 0)
    def _():
        c_scr[...] = jnp.zeros(c_scr.shape, F32)
        n_scr[...] = jnp.zeros(n_scr.shape, F32)
        m_scr[...] = jnp.zeros(m_scr.shape, F32)

    row = lax.broadcasted_iota(jnp.int32, (L, L), 0)
    col = lax.broadcasted_iota(jnp.int32, (L, L), 1)
    lower = row >= col
    upper = row <= col
    lower_f = lower.astype(F32)
    upper_f = upper.astype(F32)

    def hdot(a, b):
        return jnp.dot(a, b, preferred_element_type=F32, precision=lax.Precision.HIGHEST)

    dirs = ((qkf_ref, vf_ref, gf_ref, gtf_ref, hf_ref), (qkb_ref, vb_ref, gb_ref, gtb_ref, hb_ref))
    for d, (qk_ref, v_ref, g_ref, gt_ref, h_ref) in enumerate(dirs):
        g = g_ref[:, 0:GATE_W] + gbias_ref[...]
        gt = gt_ref[...] + gbias_t_ref[...]
        lf = _log_sigmoid(g)
        lft = _log_sigmoid(gt)
        if d == 0:
            mask = lower
            bcol_all = hdot(lower_f, lf)
            brow_all = hdot(lft, upper_f)
        else:
            mask = upper
            bcol_all = hdot(upper_f, lf)
            brow_all = hdot(lft, lower_f)
        for hd in range(ML_HEADS):
            ci = d * 2 * ML_HEADS + hd
            cf = ci + ML_HEADS
            si = d * ML_HEADS + hd
            q = qk_ref[:, hd * ML_DH:(hd + 1) * ML_DH]
            k = qk_ref[:, ML_W + hd * ML_DH:ML_W + (hd + 1) * ML_DH]
            v = v_ref[:, hd * ML_DH:(hd + 1) * ML_DH]
            li_row = gt[ci:ci + 1, :]
            li_col = g[:, ci:ci + 1]
            b_col = bcol_all[:, cf:cf + 1]
            b_row = brow_all[cf:cf + 1, :]
            btot = b_col[L - 1:L, :] if d == 0 else b_col[0:1, :]
            m = m_scr[si, 0:1, 0:1]
            cmat = c_scr[si]
            nvec = n_scr[si]

            dmat = jnp.where(mask, b_col + (li_row - b_row), -jnp.inf)
            inter = b_col + m
            m_t = jnp.maximum(inter, jnp.max(dmat, axis=1, keepdims=True))
            w = jnp.exp(dmat - m_t)
            s_inter = jnp.exp(inter - m_t)
            s = lax.dot_general(q, k, (((1,), (1,)), ((), ())), preferred_element_type=F32)
            qkw = s * w
            qf = q.astype(F32)
            num = (jnp.dot(qkw.astype(BF16), v.astype(BF16), preferred_element_type=F32)
                   + s_inter * jnp.dot(q, cmat.astype(BF16), preferred_element_type=F32))
            den = (jnp.sum(qkw, axis=1, keepdims=True)
                   + s_inter * jnp.sum(qf * nvec, axis=1, keepdims=True))
            h_ref[:, hd * ML_DH:(hd + 1) * ML_DH] = num / jnp.maximum(jnp.abs(den), jnp.exp(-m_t))

            wk = btot - b_col + li_col
            m_new = jnp.maximum(btot + m, jnp.max(wk, axis=0, keepdims=True))
            decay = jnp.exp(btot + m - m_new)
            wk = jnp.exp(wk - m_new)
            kw = wk * k.astype(F32)
            c_scr[si] = decay * cmat + jnp.dot(kw.T.astype(BF16), v.astype(BF16),
                                               preferred_element_type=F32)
            n_scr[si] = decay * nvec + jnp.sum(kw, axis=0, keepdims=True)
            m_scr[si] = jnp.broadcast_to(m_new, m_scr.shape[1:])


def _mlstm_call(qk, mlf, mlg, mlg_t, gate_b):
    B, T, _ = qk.shape
    nt = T // TT

    def fwd(b, t):
        return (b, t, 0)

    def bwd_t(t):
        return jnp.where(t == 0, 0, nt - t)

    def spec(n, cblk, back):
        if back:
            return pl.BlockSpec((None, TT, n), lambda b, t: (b, bwd_t(t), cblk))
        return pl.BlockSpec((None, TT, n), lambda b, t: (b, t, cblk))

    def tspec(back):
        if back:
            return pl.BlockSpec((None, GATE_W, TT), lambda b, t: (b, 0, bwd_t(t)))
        return pl.BlockSpec((None, GATE_W, TT), lambda b, t: (b, 0, t))

    nch = 2 * ML_HEADS
    out_sds = jax.ShapeDtypeStruct((B, T, ML_W), F32)
    return pl.pallas_call(
        _mlstm_kernel,
        grid=(B, nt),
        in_specs=[
            spec(2 * ML_W, 0, False), spec(2 * ML_W, 0, True),
            spec(ML_W, 2, False), spec(ML_W, 2, True),
            spec(GATE_PAD, 0, False), spec(GATE_PAD, 0, True),
            tspec(False), tspec(True),
            pl.BlockSpec((1, GATE_W), lambda b, t: (0, 0)),
            pl.BlockSpec((GATE_W, 1), lambda b, t: (0, 0)),
        ],
        out_specs=[spec(ML_W, 0, False), spec(ML_W, 0, True)],
        out_shape=[out_sds, out_sds],
        scratch_shapes=[
            pltpu.VMEM((nch, ML_DH, ML_DH), F32),
            pltpu.VMEM((nch, 1, ML_DH), F32),
            pltpu.VMEM((nch, 8, 128), F32),
        ],
        compiler_params=_cparams(("parallel", "arbitrary")),
        name="mlstm_scan",
    )(qk, qk, mlf, mlf, mlg, mlg, mlg_t, mlg_t, gate_b.reshape(1, GATE_W), gate_b.reshape(GATE_W, 1))


def _diff_kernel(q_ref, k_ref, v_ref, lam_ref, g_ref, o_ref, vext_scr, *, lam_init, n_ctx):
    @pl.when(pl.program_id(2) == 0)
    def _():
        lane_v = lax.broadcasted_iota(jnp.int32, v_ref.shape, 1)
        vext_scr[:, 0:DF_DV] = v_ref[...]
        vext_scr[:, DF_DV:2 * DF_DV] = jnp.where(lane_v == 0, 1.0, 0.0).astype(BF16)

    lv = lam_ref[...]
    lam = (jnp.exp(jnp.sum(lv[0:1, :] * lv[1:2, :], axis=1, keepdims=True))
           - jnp.exp(jnp.sum(lv[2:3, :] * lv[3:4, :], axis=1, keepdims=True)) + lam_init)
    q = q_ref[...]
    lane = lax.broadcasted_iota(jnp.int32, q.shape, 1)
    zero = jnp.zeros(q.shape, q.dtype)
    q1 = jnp.where(lane < DF_DQK, q, zero)
    q2 = jnp.where(lane < DF_DQK, zero, q)

    def attend(nk):
        k = k_ref[0:nk, :]
        vext = vext_scr[0:nk, :]
        dn = (((1,), (1,)), ((), ()))
        s1 = lax.dot_general(q1, k, dn, preferred_element_type=F32)
        s2 = lax.dot_general(q2, k, dn, preferred_element_type=F32)
        e1 = jnp.exp2(s1 - jnp.max(s1, axis=1, keepdims=True)).astype(BF16)
        e2 = jnp.exp2(s2 - jnp.max(s2, axis=1, keepdims=True)).astype(BF16)
        o1 = jnp.dot(e1, vext, preferred_element_type=F32)
        o2 = jnp.dot(e2, vext, preferred_element_type=F32)
        o = (o1[:, 0:DF_DV] / o1[:, DF_DV:DF_DV + 1]
             - lam * (o2[:, 0:DF_DV] / o2[:, DF_DV:DF_DV + 1]))
        var = jnp.mean(o * o, axis=1, keepdims=True)
        o_ref[...] = (o * lax.rsqrt(var + EPS) * g_ref[...] * (1.0 - lam_init)).astype(o_ref.dtype)

    @pl.when(pl.program_id(2) == 0)
    def _():
        attend(n_ctx)

    @pl.when(pl.program_id(2) != 0)
    def _():
        attend(k_ref.shape[0])


def _diff_call(dfq, dfk, dfv, lam_vecs, norm_g, lam_init, n_ctx):
    B, T, _ = dfq.shape
    return pl.pallas_call(
        functools.partial(_diff_kernel, lam_init=lam_init, n_ctx=n_ctx),
        grid=(B, DF_HEADS, T // TT),
        in_specs=[
            pl.BlockSpec((None, TT, DF_DV), lambda b, h, t: (b, t, h)),
            pl.BlockSpec((None, T, DF_DV), lambda b, h, t: (b, 0, h)),
            pl.BlockSpec((None, T, DF_DV), lambda b, h, t: (b, 0, h)),
            pl.BlockSpec((4, DF_DQK), lambda b, h, t: (0, 0)),
            pl.BlockSpec((1, DF_DV), lambda b, h, t: (0, h)),
        ],
        out_specs=pl.BlockSpec((None, TT, DF_DV), lambda b, h, t: (b, t, h)),
        out_shape=jax.ShapeDtypeStruct((B, T, DF_W), BF16),
        scratch_shapes=[pltpu.VMEM((T, 2 * DF_DV), BF16)],
        compiler_params=_cparams(("parallel", "parallel", "arbitrary")),
        name="diff_attn",
    )(dfq, dfk, dfv, lam_vecs, norm_g.reshape(1, DF_W))


def _na_tables(rows):
    kh = min(NA_KH, rows)
    assert kh == NA_KH and rows >= NA_KROWS and rows % NA_QROWS == 0
    n_tiles = rows // NA_QROWS
    qc = np.arange(GRID_W)
    qstart = np.clip(qc - NA_KW // 2, 0, GRID_W - NA_KW)
    col_ok = (qc[None, :] >= qstart[:, None]) & (qc[None, :] < qstart[:, None] + NA_KW)
    drs, valids = [], []
    for tile in (0, 1, n_tiles - 1):
        r0 = tile * NA_QROWS
        start = int(np.clip(r0 - kh // 2, 0, rows - NA_KROWS))
        r = r0 + np.arange(NA_QROWS)
        rs = np.clip(r - kh // 2, 0, rows - kh)
        akr = start + np.arange(NA_KROWS)
        row_ok = (akr[None, :] >= rs[:, None]) & (akr[None, :] < rs[:, None] + kh)
        drs.append(np.clip(akr[None, :] - r[:, None] + NA_KH - 1, 0, 2 * NA_KH - 2))
        ok = row_ok[:, None, :, None] & col_ok[None, :, None, :]
        valids.append(ok.reshape(TT, NA_KROWS * GRID_W))
    return np.stack(drs), np.stack(valids)


def _na_bias_tables(rel_bias, rows):
    dr, valid = _na_tables(rows)
    L, H = rel_bias.shape[:2]
    pad = GRID_W - NA_KW
    padded = jnp.pad(rel_bias, ((0, 0), (0, 0), (0, 0), (pad, pad)), mode="edge")
    base = NA_KW - 1 + pad
    by_col = jnp.stack([padded[..., base - q:base - q + GRID_W] for q in range(GRID_W)], axis=-2)
    tabs = []
    for ty in range(3):
        slabs = jnp.stack([by_col[:, :, int(d)] for d in dr[ty].reshape(-1)], axis=2)
        slabs = slabs.reshape(L, H, NA_QROWS, NA_KROWS, GRID_W, GRID_W)
        tab = slabs.transpose(0, 1, 2, 4, 3, 5).reshape(L, H, TT, NA_KROWS * GRID_W)
        tabs.append(jnp.where(valid[ty], tab, NEG))
    return jnp.stack(tabs, axis=1).astype(F32)


def _na_kernel(q_ref, k_ref, v_ref, bias_ref, o_ref, *, n_ctx, rows):
    t = pl.program_id(1)
    dn = (((1,), (1,)), ((), ()))

    @pl.when(t == 0)
    def _():
        for hd in range(NA_HEADS):
            sl = slice(hd * NA_DH, (hd + 1) * NA_DH)
            q = q_ref[:, sl]
            s = lax.dot_general(q, k_ref[0:n_ctx, sl], dn, preferred_element_type=F32)
            e = jnp.exp(s - jnp.max(s, axis=1, keepdims=True))
            o = jnp.dot(e.astype(BF16), v_ref[0:n_ctx, sl], preferred_element_type=F32)
            o_ref[:, sl] = (o / jnp.sum(e, axis=1, keepdims=True)).astype(o_ref.dtype)

    @pl.when(t != 0)
    def _():
        r0 = (t - 1) * NA_QROWS
        start = jnp.clip(r0 - NA_KH // 2, 0, rows - NA_KROWS)
        tok0 = pl.multiple_of(n_ctx + start * GRID_W, GRID_W)
        nk = NA_KROWS * GRID_W
        for hd in range(NA_HEADS):
            sl = slice(hd * NA_DH, (hd + 1) * NA_DH)
            q = q_ref[:, sl]
            s_lat = lax.dot_general(q, k_ref[pl.ds(tok0, nk), sl], dn,
                                    preferred_element_type=F32) + bias_ref[hd]
            s_ctx = lax.dot_general(q, k_ref[0:n_ctx, sl], dn, preferred_element_type=F32)
            m = jnp.maximum(jnp.max(s_lat, axis=1, keepdims=True),
                            jnp.max(s_ctx, axis=1, keepdims=True))
            e_lat = jnp.exp(s_lat - m)
            e_ctx = jnp.exp(s_ctx - m)
            den = jnp.sum(e_lat, axis=1, keepdims=True) + jnp.sum(e_ctx, axis=1, keepdims=True)
            o = (jnp.dot(e_lat.astype(BF16), v_ref[pl.ds(tok0, nk), sl], preferred_element_type=F32)
                 + jnp.dot(e_ctx.astype(BF16), v_ref[0:n_ctx, sl], preferred_element_type=F32))
            o_ref[:, sl] = (o / den).astype(o_ref.dtype)


def _na_call(naq, nak, nav, bias_tab, n_ctx, l):
    B, T, _ = naq.shape
    nt = T // TT
    rows = (T - n_ctx) // GRID_W
    nk = NA_KROWS * GRID_W

    def bias_idx(b, t):
        return (l, jnp.where(t <= 1, 0, jnp.where(t == nt - 1, 2, 1)), 0, 0, 0)

    return pl.pallas_call(
        functools.partial(_na_kernel, n_ctx=n_ctx, rows=rows),
        grid=(B, nt),
        in_specs=[
            pl.BlockSpec((None, TT, NA_W), lambda b, t: (b, t, 0)),
            pl.BlockSpec((None, T, NA_W), lambda b, t: (b, 0, 0)),
            pl.BlockSpec((None, T, NA_W), lambda b, t: (b, 0, 0)),
            pl.BlockSpec((None, None, NA_HEADS, TT, nk), bias_idx),
        ],
        out_specs=pl.BlockSpec((None, TT, NA_W), lambda b, t: (b, t, 0)),
        out_shape=jax.ShapeDtypeStruct((B, T, NA_W), BF16),
        compiler_params=_cparams(("parallel", "arbitrary")),
        name="na_attn",
    )(naq, nak, nav, bias_tab)


def _merge_kernel(x_ref, mod_ref, g_ref, wg_ref, wb_ref, wo_ref, hf_ref, hb_ref, o_ref_in,
                  mlg_ref, bd_ref, cn_ref, out_ref):
    x = x_ref[...]
    D = x.shape[1]
    hn = _norm_mod(x, g_ref[...], mod_ref[3:4, :], mod_ref[4:5, :]).astype(BF16)
    hsum = hf_ref[...] + hb_ref[...]
    parts = []
    for hd in range(ML_HEADS):
        hh = hsum[:, hd * ML_DH:(hd + 1) * ML_DH]
        parts.append(hh * lax.rsqrt(jnp.mean(hh * hh, axis=1, keepdims=True) + EPS))
    a = (_sigmoid(o_ref_in[...]) * (jnp.concatenate(parts, axis=1) * mlg_ref[...])).astype(BF16)
    y = None
    for i, br in enumerate((a, bd_ref[...], cn_ref[...])):
        gate = _sigmoid(jnp.dot(hn, wg_ref[:, i * D:(i + 1) * D], preferred_element_type=F32))
        term = gate * jnp.dot(br, wb_ref[i], preferred_element_type=F32)
        y = term if y is None else y + term
    z = jnp.dot(y.astype(BF16), wo_ref[...], preferred_element_type=F32)
    out_ref[...] = x + mod_ref[5:6, :] * z


def _merge_call(x, modtab, g, w_gate, w_branch, w_out, hf, hb, mlf, ml_norm_g, bd, cn, l):
    B, T, D = x.shape

    def tile_spec(n, cblk=0):
        return pl.BlockSpec((None, TT, n), lambda b, t: (b, t, cblk))

    return pl.pallas_call(
        _merge_kernel,
        grid=(B, T // TT),
        in_specs=[
            tile_spec(D),
            _mod_spec(),
            pl.BlockSpec((1, D), lambda b, t: (0, 0)),
            pl.BlockSpec((None, D, N_BRANCH * D), lambda b, t: (l, 0, 0)),
            pl.BlockSpec((None, N_BRANCH, BRANCH_W, D), lambda b, t: (l, 0, 0, 0)),
            pl.BlockSpec((None, D, D), lambda b, t: (l, 0, 0)),
            tile_spec(ML_W), tile_spec(ML_W),
            tile_spec(ML_W, 3),
            pl.BlockSpec((1, ML_W), lambda b, t: (0, 0)),
            tile_spec(DF_W), tile_spec(NA_W),
        ],
        out_specs=tile_spec(D),
        out_shape=jax.ShapeDtypeStruct(x.shape, F32),
        compiler_params=_cparams(("parallel", "parallel")),
        name="merge",
    )(x, modtab, g, w_gate, w_branch, w_out, hf, hb, mlf, ml_norm_g, bd, cn)


def _final_kernel(x_ref, g_ref, o_ref):
    x = x_ref[...]
    o_ref[...] = x * lax.rsqrt(jnp.mean(x * x, axis=-1, keepdims=True) + EPS) * g_ref[...]


def _final_call(x, g, n_ctx):
    B, T, D = x.shape
    S = T - n_ctx
    off = n_ctx // TT
    return pl.pallas_call(
        _final_kernel,
        grid=(B, S // TT),
        in_specs=[
            pl.BlockSpec((None, TT, D), lambda b, t: (b, t + off, 0)),
            pl.BlockSpec((1, D), lambda b, t: (0, 0)),
        ],
        out_specs=pl.BlockSpec((None, TT, D), lambda b, t: (b, t, 0)),
        out_shape=jax.ShapeDtypeStruct((B, S, D), F32),
        compiler_params=_cparams(("parallel", "parallel")),
        name="final_norm",
    )(x, g)


def _rope_tables(S, n_ctx):
    t = jnp.arange(S)
    rowp = (t // GRID_W).astype(F32)
    colp = (t % GRID_W).astype(F32)
    half = DF_DQK // 2
    freqs = ROPE_BASE ** (-jnp.arange(0, half, 2, dtype=F32) / half)
    ang = jnp.concatenate([rowp[:, None] * freqs, colp[:, None] * freqs], axis=-1)
    cos, sin = jnp.cos(ang), jnp.sin(ang)
    cos = jnp.concatenate([jnp.ones((n_ctx, half), F32), cos], axis=0)
    sin = jnp.concatenate([jnp.zeros((n_ctx, half), F32), sin], axis=0)
    cos_t = jnp.tile(cos, (1, 4))
    sin_t = jnp.tile(jnp.concatenate([-sin, sin], axis=1), (1, 2))
    return cos_t, sin_t


def kernel(x, c, ctx, c_ctx, w_ada, b_ada, norm_g, ffn_w1, ffn_w2, w_in, ml_conv_w, ml_conv_b,
           ml_gate_b, ml_norm_g, df_lambda, df_norm_g, na_rel_bias, w_branch, w_out, final_g):
    B, S, D = x.shape
    n_ctx = ctx.shape[1]
    depth = w_ada.shape[0]
    assert D == D_MODEL and n_ctx == TT and S % TT == 0 and S % GRID_W == 0
    T = n_ctx + S
    rows = S // GRID_W

    w1 = ffn_w1.astype(BF16)
    w2 = ffn_w2.astype(BF16)
    w_proj = jnp.concatenate(
        [w_in[:, :, OFF_ML:OFF_MLG], w_in[:, :, OFF_DF:OFF_GATE],
         jnp.pad(w_in[:, :, OFF_MLG:OFF_DF], ((0, 0), (0, 0), (0, GATE_PAD - GATE_W)))],
        axis=-1).astype(BF16)
    w_gate = w_in[:, :, OFF_GATE:].astype(BF16)
    wb = w_branch.astype(BF16)
    wo = w_out.astype(BF16)

    cvec = jnp.zeros((16, D), F32).at[0].set(c_ctx).at[1:B + 1].set(c)
    mod = _ada_call(cvec, w_ada, b_ada).reshape(depth, 16, 9, D)
    modtab = jnp.stack([jnp.broadcast_to(mod[:, 0:1], (depth, B, 9, D)), mod[:, 1:B + 1]], axis=2)

    cos_t, sin_t = _rope_tables(S, n_ctx)
    bias_tab = _na_bias_tables(na_rel_bias, rows)

    h = jnp.concatenate([ctx, x], axis=1)
    for l in range(depth):
        lam_init = 0.8 - 0.6 * math.exp(-0.3 * l)
        mt = modtab[l]
        h = _ffn_call(h, mt, norm_g[l, 0][None], w1, w2, l, 0, 0)
        mlf, dfq, dfk, dfv, naq, nak, nav, mlg = _inproj_call(h, mt, norm_g[l, 1][None], w_proj,
                                                              cos_t, sin_t, l)
        qk = _conv_call(mlf, ml_conv_w[l], ml_conv_b[l][None], n_ctx)
        mlg_t = jnp.transpose(mlg[:, :, :GATE_W], (0, 2, 1))
        hf, hb = _mlstm_call(qk, mlf, mlg, mlg_t, ml_gate_b[l])
        bd = _diff_call(dfq, dfk, dfv, df_lambda[l], df_norm_g[l], lam_init, n_ctx)
        cn = _na_call(naq, nak, nav, bias_tab, n_ctx, l)
        h = _merge_call(h, mt, norm_g[l, 1][None], w_gate, wb, wo, hf, hb, mlf,
                        ml_norm_g[l][None], bd, cn, l)
        h = _ffn_call(h, mt, norm_g[l, 2][None], w1, w2, l, 1, 6)
    return _final_call(h, final_g[None], n_ctx)
```

```python
import functools
import math

import numpy as np
import jax
import jax.numpy as jnp
from jax import lax
from jax.experimental import pallas as pl
from jax.experimental.pallas import tpu as pltpu

F32 = jnp.float32
BF16 = jnp.bfloat16

D_MODEL = 1024
GRID_W = 64
N_BRANCH = 3
BRANCH_W = 512
ML_HEADS = 4
ML_DH = 128
ML_W = ML_HEADS * ML_DH
DF_HEADS = 4
DF_DQK = 64
DF_DV = 128
DF_W = DF_HEADS * DF_DV
ROPE_BASE = 10000.0
NA_HEADS = 8
NA_DH = 64
NA_W = NA_HEADS * NA_DH
NA_KH = 8
NA_KW = 16
D_FF = 2816
EPS = 1e-6
GATE_W = 4 * ML_HEADS

TT = 256
NA_QROWS = 4
NA_KROWS = 12
MXU_N = 256
FF_CHUNKS = ((0, 6 * MXU_N), (6 * MXU_N, D_FF))
DF_KCHUNK = MXU_N
VEXT_ROWS = DF_DV + 16
NEG = -1e30
LOG2E = math.log2(math.e)
VMEM_LIMIT = 56 * 1024 * 1024


def _cparams(sem):
    return pltpu.CompilerParams(dimension_semantics=sem, vmem_limit_bytes=VMEM_LIMIT)


def _norm_mod(x, g, shift, scale):
    var = jnp.mean(x * x, axis=-1, keepdims=True)
    return x * lax.rsqrt(var + EPS) * g * (1.0 + scale) + shift


def _sigmoid(x):
    return 1.0 / (1.0 + jnp.exp(-x))


def _ada_kernel(c_ref, w_ref, b_ref, o_ref):
    c = c_ref[...]
    h = (c * _sigmoid(c)).astype(BF16)
    o_ref[...] = jnp.dot(h, w_ref[...].astype(BF16), preferred_element_type=F32) + b_ref[...]


def _ada_call(cvec, w_ada, b_ada):
    L, D, N = w_ada.shape
    tn = N // 8
    return pl.pallas_call(
        _ada_kernel,
        grid=(L, N // tn),
        in_specs=[
            pl.BlockSpec(cvec.shape, lambda l, j: (0, 0)),
            pl.BlockSpec((None, D, tn), lambda l, j: (l, 0, j)),
            pl.BlockSpec((None, 1, tn), lambda l, j: (l, 0, j)),
        ],
        out_specs=pl.BlockSpec((None, cvec.shape[0], tn), lambda l, j: (l, 0, j)),
        out_shape=jax.ShapeDtypeStruct((L, cvec.shape[0], N), F32),
        compiler_params=_cparams(("parallel", "parallel")),
        name="adaln",
    )(cvec, w_ada, b_ada.reshape(L, 1, N))


def _ffn_kernel(x_ref, mod_ref, g_ref, w1_ref, w2_ref, *rest, k0, final):
    o_ref = rest[-1]
    x = x_ref[...]
    h = _norm_mod(x, g_ref[...], mod_ref[k0:k0 + 1, :], mod_ref[k0 + 1:k0 + 2, :]).astype(BF16)
    acc = jnp.zeros(x.shape, F32)
    for c0, c1 in FF_CHUNKS:
        a = jnp.dot(h, w1_ref[:, c0:c1], preferred_element_type=F32)
        b = jnp.dot(h, w1_ref[:, D_FF + c0:D_FF + c1], preferred_element_type=F32)
        u = (a * _sigmoid(a) * b).astype(BF16)
        acc = acc + jnp.dot(u, w2_ref[c0:c1, :], preferred_element_type=F32)
    y = x + 0.5 * mod_ref[k0 + 2:k0 + 3, :] * acc
    if final:
        y = y * lax.rsqrt(jnp.mean(y * y, axis=-1, keepdims=True) + EPS) * rest[0][...]
    o_ref[...] = y


def _mod_spec():
    return pl.BlockSpec((None, None, 9, D_MODEL), lambda b, t: (b, jnp.minimum(t, 1), 0, 0))


def _ffn_call(x, modtab, g, w1, w2, l, i, k0, final_g=None, n_ctx=0):
    B, T, D = x.shape
    final = final_g is not None
    in_specs = [
        pl.BlockSpec((None, TT, D), lambda b, t: (b, t, 0)),
        _mod_spec(),
        pl.BlockSpec((1, D), lambda b, t: (0, 0)),
        pl.BlockSpec((None, None, D, 2 * D_FF), lambda b, t: (l, i, 0, 0)),
        pl.BlockSpec((None, None, D_FF, D), lambda b, t: (l, i, 0, 0)),
    ]
    args = [x, modtab, g, w1, w2]
    if final:
        off = n_ctx // TT
        in_specs.append(pl.BlockSpec((1, D), lambda b, t: (0, 0)))
        args.append(final_g)
        out_spec = pl.BlockSpec((None, TT, D), lambda b, t: (b, jnp.maximum(t - off, 0), 0))
        out_shape = jax.ShapeDtypeStruct((B, T - n_ctx, D), F32)
    else:
        out_spec = pl.BlockSpec((None, TT, D), lambda b, t: (b, t, 0))
        out_shape = jax.ShapeDtypeStruct(x.shape, F32)
    return pl.pallas_call(
        functools.partial(_ffn_kernel, k0=k0, final=final),
        grid=(B, T // TT),
        in_specs=in_specs,
        out_specs=out_spec,
        out_shape=out_shape,
        compiler_params=_cparams(("parallel", "arbitrary")),
        name="ffn_half",
    )(*args)


def _inproj_kernel(x_ref, mod_ref, g_ref, w_ref, wt_ref, cos_ref, sin_ref,
                   mlf_ref, mlvt_ref, mlgt_ref, dfq_ref, dfk_ref, dfvt_ref, naq_ref, nak_ref, nav_ref):
    x = x_ref[...]
    h = _norm_mod(x, g_ref[...], mod_ref[3:4, :], mod_ref[4:5, :]).astype(BF16)

    def proj(c0, n):
        return jnp.dot(h, w_ref[:, c0:c0 + n], preferred_element_type=F32)

    def proj_t(r0, n):
        return lax.dot_general(wt_ref[r0:r0 + n, :], h, (((1,), (1,)), ((), ())),
                               preferred_element_type=F32)

    mlf_ref[...] = proj(0, 3 * ML_W)
    mlvt_ref[...] = proj_t(0, ML_W)
    dfvt_ref[...] = proj_t(ML_W, DF_W).astype(BF16)
    mlgt_ref[...] = proj_t(ML_W + DF_W, GATE_W)
    cos = jnp.tile(cos_ref[...], (1, DF_HEADS))
    sin = jnp.tile(sin_ref[...], (1, DF_HEADS))
    lane = lax.broadcasted_iota(jnp.int32, (TT, DF_W), 1)
    first_half = (lane % DF_DQK) < (DF_DQK // 2)

    def rope(v):
        partner = jnp.where(first_half, pltpu.roll(v, DF_W - DF_DQK // 2, 1),
                            pltpu.roll(v, DF_DQK // 2, 1))
        return v * cos + partner * sin

    c0 = 3 * ML_W
    dfq_ref[...] = (rope(proj(c0, DF_W)) * (DF_DQK ** -0.5 * LOG2E)).astype(BF16)
    dfk_ref[...] = rope(proj(c0 + DF_W, DF_W)).astype(BF16)
    c0 = c0 + 2 * DF_W
    naq_ref[...] = (proj(c0, NA_W) * NA_DH ** -0.5).astype(BF16)
    nak_ref[...] = proj(c0 + NA_W, NA_W).astype(BF16)
    nav_ref[...] = proj(c0 + 2 * NA_W, NA_W).astype(BF16)


def _split_w_in(w_in):
    o_mlv, o_mlo, o_mlg = 2 * ML_W, 3 * ML_W, 4 * ML_W
    o_dfq = o_mlg + GATE_W
    o_dfv = o_dfq + 2 * DF_W
    o_na = o_dfv + DF_W
    o_gate = o_na + 3 * NA_W
    w_tok = jnp.concatenate([w_in[:, :, 0:o_mlv], w_in[:, :, o_mlo:o_mlg], w_in[:, :, o_dfq:o_dfv],
                             w_in[:, :, o_na:o_gate]], axis=-1)
    w_tr = jnp.concatenate([w_in[:, :, o_mlv:o_mlo], w_in[:, :, o_dfv:o_na], w_in[:, :, o_mlg:o_dfq]],
                           axis=-1)
    return (w_tok.astype(BF16), jnp.swapaxes(w_tr, 1, 2).astype(BF16), w_in[:, :, o_gate:].astype(BF16))


def _inproj_call(x, modtab, g, w_tok, w_tr, cos_t, sin_t, l):
    B, T, D = x.shape

    def tile_spec(n):
        return pl.BlockSpec((None, TT, n), lambda b, t: (b, t, 0))

    def sds(n, dt):
        return jax.ShapeDtypeStruct((B, T, n), dt)

    def tspec(n):
        return pl.BlockSpec((None, n, TT), lambda b, t: (b, 0, t))

    def tsds(n, dt):
        return jax.ShapeDtypeStruct((B, n, T), dt)

    return pl.pallas_call(
        _inproj_kernel,
        grid=(B, T // TT),
        in_specs=[
            tile_spec(D),
            _mod_spec(),
            pl.BlockSpec((1, D), lambda b, t: (0, 0)),
            pl.BlockSpec((None, D, w_tok.shape[-1]), lambda b, t: (l, 0, 0)),
            pl.BlockSpec((None, w_tr.shape[1], D), lambda b, t: (l, 0, 0)),
            pl.BlockSpec((TT, 2 * DF_DQK), lambda b, t: (t, 0)),
            pl.BlockSpec((TT, 2 * DF_DQK), lambda b, t: (t, 0)),
        ],
        out_specs=[tile_spec(3 * ML_W), tspec(ML_W), tspec(GATE_W),
                   tile_spec(DF_W), tile_spec(DF_W), tspec(DF_W)] + [tile_spec(NA_W)] * 3,
        out_shape=[sds(3 * ML_W, F32), tsds(ML_W, F32), tsds(GATE_W, F32),
                   sds(DF_W, BF16), sds(DF_W, BF16), tsds(DF_W, BF16)] + [sds(NA_W, BF16)] * 3,
        compiler_params=_cparams(("parallel", "parallel")),
        name="in_proj",
    )(x, modtab, g, w_tok, w_tr, cos_t, sin_t)


def _conv_kernel(x_ref, w_ref, b_ref, o_ref, *, n_ctx):
    x = x_ref[...]
    T = x.shape[0]
    row = lax.broadcasted_iota(jnp.int32, x.shape, 0)
    prev = jnp.where((row == 0) | (row == n_ctx), 0.0, pltpu.roll(x, 1, 0))
    nxt = jnp.where((row == n_ctx - 1) | (row == T - 1), 0.0, pltpu.roll(x, T - 1, 0))
    y = prev * w_ref[0:1, :] + x * w_ref[1:2, :] + nxt * w_ref[2:3, :] + b_ref[...]
    y = y * _sigmoid(y)
    scale = jnp.where(pl.program_id(1) < ML_W // x.shape[1], ML_DH ** -0.5, 1.0)
    o_ref[...] = (y * scale).astype(BF16)


def _conv_call(mlf, conv_w, conv_b, n_ctx):
    B, T, _ = mlf.shape
    ct = 256
    return pl.pallas_call(
        functools.partial(_conv_kernel, n_ctx=n_ctx),
        grid=(B, 2 * ML_W // ct),
        in_specs=[
            pl.BlockSpec((None, T, ct), lambda b, c: (b, 0, c)),
            pl.BlockSpec((3, ct), lambda b, c: (0, c)),
            pl.BlockSpec((1, ct), lambda b, c: (0, c)),
        ],
        out_specs=pl.BlockSpec((None, T, ct), lambda b, c: (b, 0, c)),
        out_shape=jax.ShapeDtypeStruct((B, T, 2 * ML_W), BF16),
        compiler_params=_cparams(("parallel", "parallel")),
        name="ml_conv",
    )(mlf, conv_w, conv_b)


def _log_sigmoid(x):
    return jnp.minimum(x, 0.0) - jnp.log1p(jnp.exp(-jnp.abs(x)))


def _lane_scan(x, reverse):
    n = x.shape[1]
    lane = lax.broadcasted_iota(jnp.int32, x.shape, 1)
    k = 1
    while k < n:
        if reverse:
            x = x + jnp.where(lane < n - k, pltpu.roll(x, n - k, 1), 0.0)
        else:
            x = x + jnp.where(lane >= k, pltpu.roll(x, k, 1), 0.0)
        k *= 2
    return x


def _mlstm_kernel(qkf_ref, qkb_ref, vtf_ref, vtb_ref, gtf_ref, gtb_ref, gbias_ref,
                  hf_ref, hb_ref, c_scr, n_scr, m_scr):
    L = TT

    @pl.when(pl.program_id(1) == 0)
    def _():
        c_scr[...] = jnp.zeros(c_scr.shape, F32)
        n_scr[...] = jnp.zeros(n_scr.shape, F32)
        m_scr[...] = jnp.zeros(m_scr.shape, F32)

    key = lax.broadcasted_iota(jnp.int32, (L, L), 0)
    qry = lax.broadcasted_iota(jnp.int32, (L, L), 1)
    row8 = lax.broadcasted_iota(jnp.int32, (8, L), 0)
    nt_dims = (((1,), (1,)), ((), ()))

    chains = []
    dirs = ((qkf_ref, vtf_ref, gtf_ref, hf_ref), (qkb_ref, vtb_ref, gtb_ref, hb_ref))
    for d, (qk_ref, vt_ref, gt_ref, h_ref) in enumerate(dirs):
        gt = gt_ref[...] + gbias_ref[...]
        b_rows = _lane_scan(_log_sigmoid(gt), reverse=(d == 1))
        mask = (key <= qry) if d == 0 else (key >= qry)
        r_rows = gt - pltpu.roll(b_rows, GATE_W - ML_HEADS, 0)
        r_cols = jnp.concatenate([r_rows, jnp.zeros((128 - GATE_W, L), F32)], axis=0).T
        for hd in range(ML_HEADS):
            ci = d * 2 * ML_HEADS + hd
            b_row = b_rows[ci + ML_HEADS:ci + ML_HEADS + 1, :]
            chains.append(dict(
                si=d * ML_HEADS + hd, hd=hd, h_ref=h_ref, mask=mask,
                q=qk_ref[:, hd * ML_DH:(hd + 1) * ML_DH],
                k=qk_ref[:, ML_W + hd * ML_DH:ML_W + (hd + 1) * ML_DH],
                vt=vt_ref[hd * ML_DH:(hd + 1) * ML_DH, :],
                li_row=gt[ci:ci + 1, :], b_row=b_row, r_col=r_cols[:, ci:ci + 1],
                btot=b_row[:, L - 1:L] if d == 0 else b_row[:, 0:1]))

    for ch in chains:
        m = m_scr[ch["si"], 0:1, 0:1]
        dmat = jnp.where(ch["mask"], ch["r_col"] + ch["b_row"], -jnp.inf)
        inter = ch["b_row"] + m
        m_t = jnp.maximum(inter, jnp.max(dmat, axis=0, keepdims=True))
        ch.update(m=m, m_t=m_t, w=jnp.exp(dmat - m_t), s_inter=jnp.exp(inter - m_t),
                  s=lax.dot_general(ch["k"], ch["q"], nt_dims, preferred_element_type=F32))

    for ch in chains:
        si, hd, q = ch["si"], ch["hd"], ch["q"]
        cmat = c_scr[si]
        n8 = n_scr[si]
        qkw = ch["s"] * ch["w"]
        num = (jnp.dot(ch["vt"].astype(BF16), qkw.astype(BF16), preferred_element_type=F32)
               + ch["s_inter"] * lax.dot_general(cmat.astype(BF16), q, nt_dims,
                                                 preferred_element_type=F32))
        nq = lax.dot_general(n8.astype(BF16), q, nt_dims, preferred_element_type=F32)[0:1, :]
        den = jnp.sum(qkw, axis=0, keepdims=True) + ch["s_inter"] * nq
        ht = num / jnp.maximum(jnp.abs(den), jnp.exp(-ch["m_t"]))
        ch["h_ref"][:, hd * ML_DH:(hd + 1) * ML_DH] = ht.T

    for ch in chains:
        si, m, btot = ch["si"], ch["m"], ch["btot"]
        wk = btot - ch["b_row"] + ch["li_row"]
        m_new = jnp.maximum(btot + m, jnp.max(wk, axis=1, keepdims=True))
        decay = jnp.exp(btot + m - m_new)
        wk = jnp.exp(wk - m_new)
        c_scr[si] = decay * c_scr[si] + jnp.dot((ch["vt"] * wk).astype(BF16), ch["k"],
                                                preferred_element_type=F32)
        wk8 = jnp.where(row8 == 0, wk, 0.0).astype(BF16)
        n_scr[si] = decay * n_scr[si] + jnp.dot(wk8, ch["k"], preferred_element_type=F32)
        m_scr[si] = jnp.broadcast_to(m_new, m_scr.shape[1:])


def _mlstm_call(qk, mlvt, mlgt, gate_b):
    B, T, _ = qk.shape
    nt = T // TT

    def bwd_t(t):
        return jnp.where(t == 0, 0, nt - t)

    def spec(n, back):
        if back:
            return pl.BlockSpec((None, TT, n), lambda b, t: (b, bwd_t(t), 0))
        return pl.BlockSpec((None, TT, n), lambda b, t: (b, t, 0))

    def tspec(n, back):
        if back:
            return pl.BlockSpec((None, n, TT), lambda b, t: (b, 0, bwd_t(t)))
        return pl.BlockSpec((None, n, TT), lambda b, t: (b, 0, t))

    nch = 2 * ML_HEADS
    out_sds = jax.ShapeDtypeStruct((B, T, ML_W), F32)
    return pl.pallas_call(
        _mlstm_kernel,
        grid=(B, nt),
        in_specs=[
            spec(2 * ML_W, False), spec(2 * ML_W, True),
            tspec(ML_W, False), tspec(ML_W, True),
            tspec(GATE_W, False), tspec(GATE_W, True),
            pl.BlockSpec((GATE_W, 1), lambda b, t: (0, 0)),
        ],
        out_specs=[spec(ML_W, False), spec(ML_W, True)],
        out_shape=[out_sds, out_sds],
        scratch_shapes=[
            pltpu.VMEM((nch, ML_DH, ML_DH), F32),
            pltpu.VMEM((nch, 8, ML_DH), F32),
            pltpu.VMEM((nch, 8, 128), F32),
        ],
        compiler_params=_cparams(("parallel", "arbitrary")),
        name="mlstm_scan",
    )(qk, qk, mlvt, mlvt, mlgt, mlgt, gate_b.reshape(GATE_W, 1))


def _diff_kernel(q_ref, k_ref, vt_ref, lam_ref, g_ref, o_ref, vext_scr, sa_scr, sb_scr, ma_scr, mb_scr,
                 *, lam_init, n_ctx):
    t = pl.program_id(2)
    nk = k_ref.shape[0]
    dn = (((1,), (1,)), ((), ()))
    chunks = [(c, min(c + DF_KCHUNK, nk)) for c in range(0, nk, DF_KCHUNK)]

    @pl.when(t == 0)
    def _():
        row_v = lax.broadcasted_iota(jnp.int32, (VEXT_ROWS - DF_DV, nk), 0)
        vext_scr[0:DF_DV, :] = vt_ref[...]
        vext_scr[DF_DV:VEXT_ROWS, :] = jnp.where(row_v == 0, 1.0, 0.0).astype(BF16)
        sb_scr[...] = jnp.zeros(sb_scr.shape, F32)
        mb_scr[...] = jnp.zeros(mb_scr.shape, F32)

    lv = lam_ref[...]
    lam = (jnp.exp(jnp.sum(lv[0:1, :] * lv[1:2, :], axis=1, keepdims=True))
           - jnp.exp(jnp.sum(lv[2:3, :] * lv[3:4, :], axis=1, keepdims=True)) + lam_init)

    def step(s_rd, m_rd, s_wr, m_wr):
        q = q_ref[...]
        lane = lax.broadcasted_iota(jnp.int32, q.shape, 1)
        zero = jnp.zeros(q.shape, q.dtype)
        qms = (jnp.where(lane < DF_DQK, q, zero), jnp.where(lane < DF_DQK, zero, q))
        m_old = [m_rd[j, 0:1, :] for j in range(2)]
        m_new = [None, None]
        acc = [None, None]
        for c0, c1 in chunks:
            for j in range(2):
                e = jnp.exp2(s_rd[j, c0:c1, :] - m_old[j]).astype(BF16)
                part = jnp.dot(vext_scr[:, c0:c1], e, preferred_element_type=F32)
                acc[j] = part if acc[j] is None else acc[j] + part
                s = lax.dot_general(k_ref[c0:c1, :], qms[j], dn, preferred_element_type=F32)
                if c0 >= n_ctx:
                    s = jnp.where(t == 0, NEG, s)
                s_wr[j, c0:c1, :] = s
                cm = jnp.max(s, axis=0, keepdims=True)
                m_new[j] = cm if m_new[j] is None else jnp.maximum(m_new[j], cm)
        for j in range(2):
            m_wr[j] = jnp.broadcast_to(m_new[j], m_wr.shape[1:])
        outs = [a[0:DF_DV, :] / a[DF_DV:DF_DV + 1, :] for a in acc]
        o = outs[0] - lam * outs[1]
        var = jnp.mean(o * o, axis=0, keepdims=True)
        on = o * lax.rsqrt(var + EPS) * g_ref[...] * (1.0 - lam_init)
        o_ref[...] = on.T.astype(o_ref.dtype)

    @pl.when(t % 2 == 0)
    def _():
        step(sb_scr, mb_scr, sa_scr, ma_scr)

    @pl.when(t % 2 == 1)
    def _():
        step(sa_scr, ma_scr, sb_scr, mb_scr)


def _diff_call(dfq, dfk, dfvt, lam_vecs, norm_g, lam_init, n_ctx):
    B, T, _ = dfq.shape
    nt = T // TT
    assert n_ctx % DF_KCHUNK == 0
    return pl.pallas_call(
        functools.partial(_diff_kernel, lam_init=lam_init, n_ctx=n_ctx),
        grid=(B, DF_HEADS, nt + 1),
        in_specs=[
            pl.BlockSpec((None, TT, DF_DV), lambda b, h, t: (b, jnp.minimum(t, nt - 1), h)),
            pl.BlockSpec((None, T, DF_DV), lambda b, h, t: (b, 0, h)),
            pl.BlockSpec((None, DF_DV, T), lambda b, h, t: (b, h, 0)),
            pl.BlockSpec((4, DF_DQK), lambda b, h, t: (0, 0)),
            pl.BlockSpec((DF_DV, 1), lambda b, h, t: (h, 0)),
        ],
        out_specs=pl.BlockSpec((None, TT, DF_DV), lambda b, h, t: (b, jnp.maximum(t - 1, 0), h)),
        out_shape=jax.ShapeDtypeStruct((B, T, DF_W), BF16),
        scratch_shapes=[pltpu.VMEM((VEXT_ROWS, T), BF16),
                        pltpu.VMEM((2, T, TT), F32), pltpu.VMEM((2, T, TT), F32),
                        pltpu.VMEM((2, 8, TT), F32), pltpu.VMEM((2, 8, TT), F32)],
        compiler_params=_cparams(("parallel", "parallel", "arbitrary")),
        name="diff_attn",
    )(dfq, dfk, dfvt, lam_vecs, norm_g.reshape(DF_W, 1))


def _na_tables(rows):
    kh = min(NA_KH, rows)
    assert kh == NA_KH and rows >= NA_KROWS and rows % NA_QROWS == 0
    n_tiles = rows // NA_QROWS
    qc = np.arange(GRID_W)
    qstart = np.clip(qc - NA_KW // 2, 0, GRID_W - NA_KW)
    col_ok = (qc[None, :] >= qstart[:, None]) & (qc[None, :] < qstart[:, None] + NA_KW)
    drs, valids = [], []
    for tile in (0, 1, n_tiles - 1):
        r0 = tile * NA_QROWS
        start = int(np.clip(r0 - kh // 2, 0, rows - NA_KROWS))
        r = r0 + np.arange(NA_QROWS)
        rs = np.clip(r - kh // 2, 0, rows - kh)
        akr = start + np.arange(NA_KROWS)
        row_ok = (akr[None, :] >= rs[:, None]) & (akr[None, :] < rs[:, None] + kh)
        drs.append(np.clip(akr[None, :] - r[:, None] + NA_KH - 1, 0, 2 * NA_KH - 2))
        ok = row_ok[:, None, :, None] & col_ok[None, :, None, :]
        valids.append(ok.reshape(TT, NA_KROWS * GRID_W))
    return np.stack(drs), np.stack(valids)


def _na_bias_tables(rel_bias, rows):
    dr, valid = _na_tables(rows)
    L, H = rel_bias.shape[:2]
    pad = GRID_W - NA_KW
    padded = jnp.pad(rel_bias, ((0, 0), (0, 0), (0, 0), (pad, pad)), mode="edge")
    base = NA_KW - 1 + pad
    by_col = jnp.stack([padded[..., base - q:base - q + GRID_W] for q in range(GRID_W)], axis=-2)
    tabs = []
    for ty in range(3):
        slabs = jnp.stack([by_col[:, :, int(d)] for d in dr[ty].reshape(-1)], axis=2)
        slabs = slabs.reshape(L, H, NA_QROWS, NA_KROWS, GRID_W, GRID_W)
        tab = slabs.transpose(0, 1, 2, 4, 3, 5).reshape(L, H, TT, NA_KROWS * GRID_W)
        tabs.append(jnp.where(valid[ty], tab, NEG))
    return jnp.stack(tabs, axis=1).astype(F32)


def _na_kernel(q_ref, k_ref, v_ref, bias_ref, o_ref, *, n_ctx, rows):
    t = pl.program_id(1)
    dn = (((1,), (1,)), ((), ()))

    @pl.when(t == 0)
    def _():
        for hd in range(NA_HEADS):
            sl = slice(hd * NA_DH, (hd + 1) * NA_DH)
            q = q_ref[:, sl]
            s = lax.dot_general(q, k_ref[0:n_ctx, sl], dn, preferred_element_type=F32)
            e = jnp.exp(s - jnp.max(s, axis=1, keepdims=True))
            o = jnp.dot(e.astype(BF16), v_ref[0:n_ctx, sl], preferred_element_type=F32)
            o_ref[:, sl] = (o / jnp.sum(e, axis=1, keepdims=True)).astype(o_ref.dtype)

    @pl.when(t != 0)
    def _():
        r0 = (t - 1) * NA_QROWS
        start = jnp.clip(r0 - NA_KH // 2, 0, rows - NA_KROWS)
        tok0 = pl.multiple_of(n_ctx + start * GRID_W, GRID_W)
        nk = NA_KROWS * GRID_W
        for hd in range(NA_HEADS):
            sl = slice(hd * NA_DH, (hd + 1) * NA_DH)
            q = q_ref[:, sl]
            s_lat = lax.dot_general(q, k_ref[pl.ds(tok0, nk), sl], dn,
                                    preferred_element_type=F32) + bias_ref[hd]
            s_ctx = lax.dot_general(q, k_ref[0:n_ctx, sl], dn, preferred_element_type=F32)
            m = jnp.maximum(jnp.max(s_lat, axis=1, keepdims=True),
                            jnp.max(s_ctx, axis=1, keepdims=True))
            e_lat = jnp.exp(s_lat - m)
            e_ctx = jnp.exp(s_ctx - m)
            den = jnp.sum(e_lat, axis=1, keepdims=True) + jnp.sum(e_ctx, axis=1, keepdims=True)
            o = (jnp.dot(e_lat.astype(BF16), v_ref[pl.ds(tok0, nk), sl], preferred_element_type=F32)
                 + jnp.dot(e_ctx.astype(BF16), v_ref[0:n_ctx, sl], preferred_element_type=F32))
            o_ref[:, sl] = (o / den).astype(o_ref.dtype)


def _na_call(naq, nak, nav, bias_tab, n_ctx, l):
    B, T, _ = naq.shape
    nt = T // TT
    rows = (T - n_ctx) // GRID_W
    nk = NA_KROWS * GRID_W

    def bias_idx(b, t):
        return (l, jnp.where(t <= 1, 0, jnp.where(t == nt - 1, 2, 1)), 0, 0, 0)

    return pl.pallas_call(
        functools.partial(_na_kernel, n_ctx=n_ctx, rows=rows),
        grid=(B, nt),
        in_specs=[
            pl.BlockSpec((None, TT, NA_W), lambda b, t: (b, t, 0)),
            pl.BlockSpec((None, T, NA_W), lambda b, t: (b, 0, 0)),
            pl.BlockSpec((None, T, NA_W), lambda b, t: (b, 0, 0)),
            pl.BlockSpec((None, None, NA_HEADS, TT, nk), bias_idx),
        ],
        out_specs=pl.BlockSpec((None, TT, NA_W), lambda b, t: (b, t, 0)),
        out_shape=jax.ShapeDtypeStruct((B, T, NA_W), BF16),
        compiler_params=_cparams(("parallel", "arbitrary")),
        name="na_attn",
    )(naq, nak, nav, bias_tab)


def _merge_kernel(x_ref, mod_ref, g_ref, wg_ref, wb_ref, wo_ref, hf_ref, hb_ref, o_ref_in,
                  mlg_ref, bd_ref, cn_ref, out_ref):
    x = x_ref[...]
    D = x.shape[1]
    hn = _norm_mod(x, g_ref[...], mod_ref[3:4, :], mod_ref[4:5, :]).astype(BF16)
    hsum = hf_ref[...] + hb_ref[...]
    parts = []
    for hd in range(ML_HEADS):
        hh = hsum[:, hd * ML_DH:(hd + 1) * ML_DH]
        parts.append(hh * lax.rsqrt(jnp.mean(hh * hh, axis=1, keepdims=True) + EPS))
    a = (_sigmoid(o_ref_in[...]) * (jnp.concatenate(parts, axis=1) * mlg_ref[...])).astype(BF16)
    y = None
    for i, br in enumerate((a, bd_ref[...], cn_ref[...])):
        gate = _sigmoid(jnp.dot(hn, wg_ref[:, i * D:(i + 1) * D], preferred_element_type=F32))
        term = gate * jnp.dot(br, wb_ref[i], preferred_element_type=F32)
        y = term if y is None else y + term
    z = jnp.dot(y.astype(BF16), wo_ref[...], preferred_element_type=F32)
    out_ref[...] = x + mod_ref[5:6, :] * z


def _merge_call(x, modtab, g, w_gate, w_branch, w_out, hf, hb, mlf, ml_norm_g, bd, cn, l):
    B, T, D = x.shape

    def tile_spec(n, cblk=0):
        return pl.BlockSpec((None, TT, n), lambda b, t: (b, t, cblk))

    return pl.pallas_call(
        _merge_kernel,
        grid=(B, T // TT),
        in_specs=[
            tile_spec(D),
            _mod_spec(),
            pl.BlockSpec((1, D), lambda b, t: (0, 0)),
            pl.BlockSpec((None, D, N_BRANCH * D), lambda b, t: (l, 0, 0)),
            pl.BlockSpec((None, N_BRANCH, BRANCH_W, D), lambda b, t: (l, 0, 0, 0)),
            pl.BlockSpec((None, D, D), lambda b, t: (l, 0, 0)),
            tile_spec(ML_W), tile_spec(ML_W),
            tile_spec(ML_W, 2),
            pl.BlockSpec((1, ML_W), lambda b, t: (0, 0)),
            tile_spec(DF_W), tile_spec(NA_W),
        ],
        out_specs=tile_spec(D),
        out_shape=jax.ShapeDtypeStruct(x.shape, F32),
        compiler_params=_cparams(("parallel", "parallel")),
        name="merge",
    )(x, modtab, g, w_gate, w_branch, w_out, hf, hb, mlf, ml_norm_g, bd, cn)


def _rope_tables(S, n_ctx):
    t = jnp.arange(S)
    rowp = (t // GRID_W).astype(F32)
    colp = (t % GRID_W).astype(F32)
    half = DF_DQK // 2
    freqs = ROPE_BASE ** (-jnp.arange(0, half, 2, dtype=F32) / half)
    ang = jnp.concatenate([rowp[:, None] * freqs, colp[:, None] * freqs], axis=-1)
    cos, sin = jnp.cos(ang), jnp.sin(ang)
    cos = jnp.concatenate([jnp.ones((n_ctx, half), F32), cos], axis=0)
    sin = jnp.concatenate([jnp.zeros((n_ctx, half), F32), sin], axis=0)
    cos_t = jnp.tile(cos, (1, 4))
    sin_t = jnp.tile(jnp.concatenate([-sin, sin], axis=1), (1, 2))
    return cos_t, sin_t


def kernel(x, c, ctx, c_ctx, w_ada, b_ada, norm_g, ffn_w1, ffn_w2, w_in, ml_conv_w, ml_conv_b,
           ml_gate_b, ml_norm_g, df_lambda, df_norm_g, na_rel_bias, w_branch, w_out, final_g):
    B, S, D = x.shape
    n_ctx = ctx.shape[1]
    depth = w_ada.shape[0]
    assert D == D_MODEL and n_ctx == TT and S % TT == 0 and S % GRID_W == 0
    T = n_ctx + S
    rows = S // GRID_W

    w1 = ffn_w1.astype(BF16)
    w2 = ffn_w2.astype(BF16)
    w_tok, w_tr, w_gate = _split_w_in(w_in)
    wb = w_branch.astype(BF16)
    wo = w_out.astype(BF16)

    cvec = jnp.zeros((16, D), F32).at[0].set(c_ctx).at[1:B + 1].set(c)
    mod = _ada_call(cvec, w_ada, b_ada).reshape(depth, 16, 9, D)
    modtab = jnp.stack([jnp.broadcast_to(mod[:, 0:1], (depth, B, 9, D)), mod[:, 1:B + 1]], axis=2)

    cos_t, sin_t = _rope_tables(S, n_ctx)
    bias_tab = _na_bias_tables(na_rel_bias, rows)

    h = jnp.concatenate([ctx, x], axis=1)
    for l in range(depth):
        lam_init = 0.8 - 0.6 * math.exp(-0.3 * l)
        mt = modtab[l]
        h = _ffn_call(h, mt, norm_g[l, 0][None], w1, w2, l, 0, 0)
        mlf, mlvt, mlgt, dfq, dfk, dfvt, naq, nak, nav = _inproj_call(
            h, mt, norm_g[l, 1][None], w_tok, w_tr, cos_t, sin_t, l)
        qk = _conv_call(mlf, ml_conv_w[l], ml_conv_b[l][None], n_ctx)
        hf, hb = _mlstm_call(qk, mlvt, mlgt, ml_gate_b[l])
        bd = _diff_call(dfq, dfk, dfvt, df_lambda[l], df_norm_g[l], lam_init, n_ctx)
        cn = _na_call(naq, nak, nav, bias_tab, n_ctx, l)
        h = _merge_call(h, mt, norm_g[l, 1][None], w_gate, wb, wo, hf, hb, mlf,
                        ml_norm_g[l][None], bd, cn, l)
        last = l == depth - 1
        h = _ffn_call(h, mt, norm_g[l, 2][None], w1, w2, l, 1, 6,
                      final_g=final_g[None] if last else None, n_ctx=n_ctx)
    return h
```

```python
import functools
import math

import numpy as np
import jax
import jax.numpy as jnp
from jax import lax
from jax.experimental import pallas as pl
from jax.experimental.pallas import tpu as pltpu

F32 = jnp.float32
BF16 = jnp.bfloat16

D_MODEL = 1024
GRID_W = 64
N_BRANCH = 3
BRANCH_W = 512
ML_HEADS = 4
ML_DH = 128
ML_W = ML_HEADS * ML_DH
DF_HEADS = 4
DF_DQK = 64
DF_DV = 128
DF_W = DF_HEADS * DF_DV
ROPE_BASE = 10000.0
NA_HEADS = 8
NA_DH = 64
NA_W = NA_HEADS * NA_DH
NA_KH = 8
NA_KW = 16
D_FF = 2816
EPS = 1e-6
GATE_W = 4 * ML_HEADS

TT = 256
NA_QROWS = 4
NA_KROWS = 12
MXU_N = 256
FF_CHUNKS = ((0, 6 * MXU_N), (6 * MXU_N, D_FF))
DF_KCHUNK = MXU_N
ONES_ROWS = 16
VEXT_ROWS = DF_DV + ONES_ROWS
NA_VROWS = NA_DH + ONES_ROWS
NEG = -1e30
LOG2E = math.log2(math.e)
VMEM_LIMIT = 56 * 1024 * 1024


def _cparams(sem):
    return pltpu.CompilerParams(dimension_semantics=sem, vmem_limit_bytes=VMEM_LIMIT)


def _norm_mod(x, g, shift, scale):
    var = jnp.mean(x * x, axis=-1, keepdims=True)
    return x * lax.rsqrt(var + EPS) * g * (1.0 + scale) + shift


def _sigmoid(x):
    return 1.0 / (1.0 + jnp.exp(-x))


def _ada_kernel(c_ref, w_ref, b_ref, o_ref):
    c = c_ref[...]
    h = (c * _sigmoid(c)).astype(BF16)
    o_ref[...] = jnp.dot(h, w_ref[...].astype(BF16), preferred_element_type=F32) + b_ref[...]


def _ada_call(cvec, w_ada, b_ada):
    L, D, N = w_ada.shape
    tn = N // 8
    return pl.pallas_call(
        _ada_kernel,
        grid=(L, N // tn),
        in_specs=[
            pl.BlockSpec(cvec.shape, lambda l, j: (0, 0)),
            pl.BlockSpec((None, D, tn), lambda l, j: (l, 0, j)),
            pl.BlockSpec((None, 1, tn), lambda l, j: (l, 0, j)),
        ],
        out_specs=pl.BlockSpec((None, cvec.shape[0], tn), lambda l, j: (l, 0, j)),
        out_shape=jax.ShapeDtypeStruct((L, cvec.shape[0], N), F32),
        compiler_params=_cparams(("parallel", "parallel")),
        name="adaln",
    )(cvec, w_ada, b_ada.reshape(L, 1, N))


def _ffn_kernel(x_ref, mod_ref, g_ref, w1_ref, w2_ref, *rest, k0, final):
    o_ref = rest[-1]
    x = x_ref[...]
    h = _norm_mod(x, g_ref[...], mod_ref[k0:k0 + 1, :], mod_ref[k0 + 1:k0 + 2, :]).astype(BF16)
    acc = jnp.zeros(x.shape, F32)
    for c0, c1 in FF_CHUNKS:
        a = jnp.dot(h, w1_ref[:, c0:c1], preferred_element_type=F32)
        b = jnp.dot(h, w1_ref[:, D_FF + c0:D_FF + c1], preferred_element_type=F32)
        u = (a * _sigmoid(a) * b).astype(BF16)
        acc = acc + jnp.dot(u, w2_ref[c0:c1, :], preferred_element_type=F32)
    y = x + 0.5 * mod_ref[k0 + 2:k0 + 3, :] * acc
    if final:
        y = y * lax.rsqrt(jnp.mean(y * y, axis=-1, keepdims=True) + EPS) * rest[0][...]
    o_ref[...] = y


def _mod_spec():
    return pl.BlockSpec((None, None, 9, D_MODEL), lambda b, t: (b, jnp.minimum(t, 1), 0, 0))


def _ffn_call(x, modtab, g, w1, w2, l, i, k0, final_g=None, n_ctx=0):
    B, T, D = x.shape
    final = final_g is not None
    in_specs = [
        pl.BlockSpec((None, TT, D), lambda b, t: (b, t, 0)),
        _mod_spec(),
        pl.BlockSpec((1, D), lambda b, t: (0, 0)),
        pl.BlockSpec((None, None, D, 2 * D_FF), lambda b, t: (l, i, 0, 0)),
        pl.BlockSpec((None, None, D_FF, D), lambda b, t: (l, i, 0, 0)),
    ]
    args = [x, modtab, g, w1, w2]
    if final:
        off = n_ctx // TT
        in_specs.append(pl.BlockSpec((1, D), lambda b, t: (0, 0)))
        args.append(final_g)
        out_spec = pl.BlockSpec((None, TT, D), lambda b, t: (b, jnp.maximum(t - off, 0), 0))
        out_shape = jax.ShapeDtypeStruct((B, T - n_ctx, D), F32)
    else:
        out_spec = pl.BlockSpec((None, TT, D), lambda b, t: (b, t, 0))
        out_shape = jax.ShapeDtypeStruct(x.shape, F32)
    return pl.pallas_call(
        functools.partial(_ffn_kernel, k0=k0, final=final),
        grid=(B, T // TT),
        in_specs=in_specs,
        out_specs=out_spec,
        out_shape=out_shape,
        compiler_params=_cparams(("parallel", "arbitrary")),
        name="ffn_half",
    )(*args)


def _inproj_kernel(x_ref, mod_ref, g_ref, w_ref, wt_ref, cos_ref, sin_ref,
                   mlf_ref, mlvt_ref, mlgt_ref, dfq_ref, dfk_ref, dfvx_ref, naq_ref, nak_ref, navx_ref):
    x = x_ref[...]
    h = _norm_mod(x, g_ref[...], mod_ref[3:4, :], mod_ref[4:5, :]).astype(BF16)

    def proj(c0, n):
        return jnp.dot(h, w_ref[:, c0:c0 + n], preferred_element_type=F32)

    def proj_t(r0, n):
        return lax.dot_general(wt_ref[r0:r0 + n, :], h, (((1,), (1,)), ((), ())),
                               preferred_element_type=F32)

    mlf_ref[...] = proj(0, 3 * ML_W)
    mlvt_ref[...] = proj_t(0, ML_W)
    mlgt_ref[...] = proj_t(ML_W + DF_W, GATE_W)
    ones_rows = jnp.where(lax.broadcasted_iota(jnp.int32, (ONES_ROWS, TT), 0) == 0, 1.0, 0.0).astype(BF16)
    for vx_ref, r0, heads, dh in ((dfvx_ref, ML_W, DF_HEADS, DF_DV),
                                  (navx_ref, ML_W + DF_W + GATE_W, NA_HEADS, NA_DH)):
        vt = proj_t(r0, heads * dh).astype(BF16)
        for hd in range(heads):
            r = hd * (dh + ONES_ROWS)
            vx_ref[r:r + dh, :] = vt[hd * dh:(hd + 1) * dh, :]
            vx_ref[r + dh:r + dh + ONES_ROWS, :] = ones_rows
    cos = jnp.tile(cos_ref[...], (1, DF_HEADS))
    sin = jnp.tile(sin_ref[...], (1, DF_HEADS))
    lane = lax.broadcasted_iota(jnp.int32, (TT, DF_W), 1)
    first_half = (lane % DF_DQK) < (DF_DQK // 2)

    def rope(v):
        partner = jnp.where(first_half, pltpu.roll(v, DF_W - DF_DQK // 2, 1),
                            pltpu.roll(v, DF_DQK // 2, 1))
        return v * cos + partner * sin

    c0 = 3 * ML_W
    dfq_ref[...] = (rope(proj(c0, DF_W)) * (DF_DQK ** -0.5 * LOG2E)).astype(BF16)
    dfk_ref[...] = rope(proj(c0 + DF_W, DF_W)).astype(BF16)
    c0 = c0 + 2 * DF_W
    naq_ref[...] = (proj(c0, NA_W) * (NA_DH ** -0.5 * LOG2E)).astype(BF16)
    nak_ref[...] = proj(c0 + NA_W, NA_W).astype(BF16)


def _split_w_in(w_in):
    o_mlv, o_mlo, o_mlg = 2 * ML_W, 3 * ML_W, 4 * ML_W
    o_dfq = o_mlg + GATE_W
    o_dfv = o_dfq + 2 * DF_W
    o_na = o_dfv + DF_W
    o_nav = o_na + 2 * NA_W
    o_gate = o_na + 3 * NA_W
    w_tok = jnp.concatenate([w_in[:, :, 0:o_mlv], w_in[:, :, o_mlo:o_mlg], w_in[:, :, o_dfq:o_dfv],
                             w_in[:, :, o_na:o_nav]], axis=-1)
    w_tr = jnp.concatenate([w_in[:, :, o_mlv:o_mlo], w_in[:, :, o_dfv:o_na], w_in[:, :, o_mlg:o_dfq],
                            w_in[:, :, o_nav:o_gate]], axis=-1)
    return (w_tok.astype(BF16), jnp.swapaxes(w_tr, 1, 2).astype(BF16), w_in[:, :, o_gate:].astype(BF16))


def _inproj_call(x, modtab, g, w_tok, w_tr, cos_t, sin_t, l):
    B, T, D = x.shape

    def tile_spec(n):
        return pl.BlockSpec((None, TT, n), lambda b, t: (b, t, 0))

    def sds(n, dt):
        return jax.ShapeDtypeStruct((B, T, n), dt)

    def tspec(n):
        return pl.BlockSpec((None, n, TT), lambda b, t: (b, 0, t))

    def tsds(n, dt):
        return jax.ShapeDtypeStruct((B, n, T), dt)

    return pl.pallas_call(
        _inproj_kernel,
        grid=(B, T // TT),
        in_specs=[
            tile_spec(D),
            _mod_spec(),
            pl.BlockSpec((1, D), lambda b, t: (0, 0)),
            pl.BlockSpec((None, D, w_tok.shape[-1]), lambda b, t: (l, 0, 0)),
            pl.BlockSpec((None, w_tr.shape[1], D), lambda b, t: (l, 0, 0)),
            pl.BlockSpec((TT, 2 * DF_DQK), lambda b, t: (t, 0)),
            pl.BlockSpec((TT, 2 * DF_DQK), lambda b, t: (t, 0)),
        ],
        out_specs=[tile_spec(3 * ML_W), tspec(ML_W), tspec(GATE_W),
                   tile_spec(DF_W), tile_spec(DF_W), tspec(DF_HEADS * VEXT_ROWS),
                   tile_spec(NA_W), tile_spec(NA_W), tspec(NA_HEADS * NA_VROWS)],
        out_shape=[sds(3 * ML_W, F32), tsds(ML_W, F32), tsds(GATE_W, F32),
                   sds(DF_W, BF16), sds(DF_W, BF16), tsds(DF_HEADS * VEXT_ROWS, BF16),
                   sds(NA_W, BF16), sds(NA_W, BF16), tsds(NA_HEADS * NA_VROWS, BF16)],
        compiler_params=_cparams(("parallel", "parallel")),
        name="in_proj",
    )(x, modtab, g, w_tok, w_tr, cos_t, sin_t)


def _conv_kernel(x_ref, w_ref, b_ref, o_ref, *, n_ctx):
    x = x_ref[...]
    T = x.shape[0]
    row = lax.broadcasted_iota(jnp.int32, x.shape, 0)
    prev = jnp.where((row == 0) | (row == n_ctx), 0.0, pltpu.roll(x, 1, 0))
    nxt = jnp.where((row == n_ctx - 1) | (row == T - 1), 0.0, pltpu.roll(x, T - 1, 0))
    y = prev * w_ref[0:1, :] + x * w_ref[1:2, :] + nxt * w_ref[2:3, :] + b_ref[...]
    y = y * _sigmoid(y)
    scale = jnp.where(pl.program_id(1) < ML_W // x.shape[1], ML_DH ** -0.5, 1.0)
    o_ref[...] = (y * scale).astype(BF16)


def _conv_call(mlf, conv_w, conv_b, n_ctx):
    B, T, _ = mlf.shape
    ct = 256
    return pl.pallas_call(
        functools.partial(_conv_kernel, n_ctx=n_ctx),
        grid=(B, 2 * ML_W // ct),
        in_specs=[
            pl.BlockSpec((None, T, ct), lambda b, c: (b, 0, c)),
            pl.BlockSpec((3, ct), lambda b, c: (0, c)),
            pl.BlockSpec((1, ct), lambda b, c: (0, c)),
        ],
        out_specs=pl.BlockSpec((None, T, ct), lambda b, c: (b, 0, c)),
        out_shape=jax.ShapeDtypeStruct((B, T, 2 * ML_W), BF16),
        compiler_params=_cparams(("parallel", "parallel")),
        name="ml_conv",
    )(mlf, conv_w, conv_b)


def _log_sigmoid(x):
    return jnp.minimum(x, 0.0) - jnp.log1p(jnp.exp(-jnp.abs(x)))


def _lane_scan(x, reverse):
    n = x.shape[1]
    lane = lax.broadcasted_iota(jnp.int32, x.shape, 1)
    k = 1
    while k < n:
        if reverse:
            x = x + jnp.where(lane < n - k, pltpu.roll(x, n - k, 1), 0.0)
        else:
            x = x + jnp.where(lane >= k, pltpu.roll(x, k, 1), 0.0)
        k *= 2
    return x


def _mlstm_kernel(qkf_ref, qkb_ref, vtf_ref, vtb_ref, gtf_ref, gtb_ref, gbias_ref,
                  hf_ref, hb_ref, c_scr, n_scr, m_scr):
    L = TT

    @pl.when(pl.program_id(1) == 0)
    def _():
        c_scr[...] = jnp.zeros(c_scr.shape, F32)
        n_scr[...] = jnp.zeros(n_scr.shape, F32)
        m_scr[...] = jnp.zeros(m_scr.shape, F32)

    key = lax.broadcasted_iota(jnp.int32, (L, L), 0)
    qry = lax.broadcasted_iota(jnp.int32, (L, L), 1)
    row8 = lax.broadcasted_iota(jnp.int32, (8, L), 0)
    nt_dims = (((1,), (1,)), ((), ()))

    chains = []
    dirs = ((qkf_ref, vtf_ref, gtf_ref, hf_ref), (qkb_ref, vtb_ref, gtb_ref, hb_ref))
    for d, (qk_ref, vt_ref, gt_ref, h_ref) in enumerate(dirs):
        gt = gt_ref[...] + gbias_ref[...]
        b_rows = _lane_scan(_log_sigmoid(gt), reverse=(d == 1))
        mask = (key <= qry) if d == 0 else (key >= qry)
        r_rows = gt - pltpu.roll(b_rows, GATE_W - ML_HEADS, 0)
        r_cols = jnp.concatenate([r_rows, jnp.zeros((128 - GATE_W, L), F32)], axis=0).T
        for hd in range(ML_HEADS):
            ci = d * 2 * ML_HEADS + hd
            b_row = b_rows[ci + ML_HEADS:ci + ML_HEADS + 1, :]
            chains.append(dict(
                si=d * ML_HEADS + hd, hd=hd, h_ref=h_ref, mask=mask,
                q=qk_ref[:, hd * ML_DH:(hd + 1) * ML_DH],
                k=qk_ref[:, ML_W + hd * ML_DH:ML_W + (hd + 1) * ML_DH],
                vt=vt_ref[hd * ML_DH:(hd + 1) * ML_DH, :],
                li_row=gt[ci:ci + 1, :], b_row=b_row, r_col=r_cols[:, ci:ci + 1],
                btot=b_row[:, L - 1:L] if d == 0 else b_row[:, 0:1]))

    for ch in chains:
        m = m_scr[ch["si"], 0:1, 0:1]
        dmat = jnp.where(ch["mask"], ch["r_col"] + ch["b_row"], -jnp.inf)
        inter = ch["b_row"] + m
        m_t = jnp.maximum(inter, jnp.max(dmat, axis=0, keepdims=True))
        ch.update(m=m, m_t=m_t, w=jnp.exp(dmat - m_t), s_inter=jnp.exp(inter - m_t),
                  s=lax.dot_general(ch["k"], ch["q"], nt_dims, preferred_element_type=F32))

    for ch in chains:
        si, hd, q = ch["si"], ch["hd"], ch["q"]
        cmat = c_scr[si]
        n8 = n_scr[si]
        qkw = ch["s"] * ch["w"]
        num = (jnp.dot(ch["vt"].astype(BF16), qkw.astype(BF16), preferred_element_type=F32)
               + ch["s_inter"] * lax.dot_general(cmat.astype(BF16), q, nt_dims,
                                                 preferred_element_type=F32))
        nq = lax.dot_general(n8.astype(BF16), q, nt_dims, preferred_element_type=F32)[0:1, :]
        den = jnp.sum(qkw, axis=0, keepdims=True) + ch["s_inter"] * nq
        ht = num / jnp.maximum(jnp.abs(den), jnp.exp(-ch["m_t"]))
        ch["h_ref"][:, hd * ML_DH:(hd + 1) * ML_DH] = ht.T

    for ch in chains:
        si, m, btot = ch["si"], ch["m"], ch["btot"]
        wk = btot - ch["b_row"] + ch["li_row"]
        m_new = jnp.maximum(btot + m, jnp.max(wk, axis=1, keepdims=True))
        decay = jnp.exp(btot + m - m_new)
        wk = jnp.exp(wk - m_new)
        c_scr[si] = decay * c_scr[si] + jnp.dot((ch["vt"] * wk).astype(BF16), ch["k"],
                                                preferred_element_type=F32)
        wk8 = jnp.where(row8 == 0, wk, 0.0).astype(BF16)
        n_scr[si] = decay * n_scr[si] + jnp.dot(wk8, ch["k"], preferred_element_type=F32)
        m_scr[si] = jnp.broadcast_to(m_new, m_scr.shape[1:])


def _mlstm_call(qk, mlvt, mlgt, gate_b):
    B, T, _ = qk.shape
    nt = T // TT

    def bwd_t(t):
        return jnp.where(t == 0, 0, nt - t)

    def spec(n, back):
        if back:
            return pl.BlockSpec((None, TT, n), lambda b, t: (b, bwd_t(t), 0))
        return pl.BlockSpec((None, TT, n), lambda b, t: (b, t, 0))

    def tspec(n, back):
        if back:
            return pl.BlockSpec((None, n, TT), lambda b, t: (b, 0, bwd_t(t)))
        return pl.BlockSpec((None, n, TT), lambda b, t: (b, 0, t))

    nch = 2 * ML_HEADS
    out_sds = jax.ShapeDtypeStruct((B, T, ML_W), F32)
    return pl.pallas_call(
        _mlstm_kernel,
        grid=(B, nt),
        in_specs=[
            spec(2 * ML_W, False), spec(2 * ML_W, True),
            tspec(ML_W, False), tspec(ML_W, True),
            tspec(GATE_W, False), tspec(GATE_W, True),
            pl.BlockSpec((GATE_W, 1), lambda b, t: (0, 0)),
        ],
        out_specs=[spec(ML_W, False), spec(ML_W, True)],
        out_shape=[out_sds, out_sds],
        scratch_shapes=[
            pltpu.VMEM((nch, ML_DH, ML_DH), F32),
            pltpu.VMEM((nch, 8, ML_DH), F32),
            pltpu.VMEM((nch, 8, 128), F32),
        ],
        compiler_params=_cparams(("parallel", "arbitrary")),
        name="mlstm_scan",
    )(qk, qk, mlvt, mlvt, mlgt, mlgt, gate_b.reshape(GATE_W, 1))


def _diff_kernel(q_ref, k_ref, vx_ref, lam_ref, g_ref, o_ref, sa_scr, sb_scr, ma_scr, mb_scr,
                 *, lam_init, n_ctx):
    t = pl.program_id(2)
    nt = pl.num_programs(2) - 1
    nk = k_ref.shape[0]
    dn = (((1,), (1,)), ((), ()))
    all_chunks = [(c, min(c + DF_KCHUNK, nk)) for c in range(0, nk, DF_KCHUNK)]
    ctx_chunks = [ch for ch in all_chunks if ch[0] < n_ctx]
    slots = ((sa_scr, ma_scr), (sb_scr, mb_scr))

    def step(finish_chunks, rd, score_chunks, wr):
        acc = [None, None]
        m_new = [None, None]
        if score_chunks:
            q = q_ref[...]
            lane = lax.broadcasted_iota(jnp.int32, q.shape, 1)
            zero = jnp.zeros(q.shape, q.dtype)
            qms = (jnp.where(lane < DF_DQK, q, zero), jnp.where(lane < DF_DQK, zero, q))
        if finish_chunks:
            m_old = [rd[1][j, 0:1, :] for j in range(2)]
        for i in range(max(len(finish_chunks), len(score_chunks))):
            for j in range(2):
                if i < len(finish_chunks):
                    c0, c1 = finish_chunks[i]
                    e = jnp.exp2(rd[0][j, c0:c1, :] - m_old[j]).astype(BF16)
                    part = jnp.dot(vx_ref[:, c0:c1], e, preferred_element_type=F32)
                    acc[j] = part if acc[j] is None else acc[j] + part
                if i < len(score_chunks):
                    c0, c1 = score_chunks[i]
                    s = lax.dot_general(k_ref[c0:c1, :], qms[j], dn, preferred_element_type=F32)
                    wr[0][j, c0:c1, :] = s
                    cm = jnp.max(s, axis=0, keepdims=True)
                    m_new[j] = cm if m_new[j] is None else jnp.maximum(m_new[j], cm)
        if score_chunks:
            for j in range(2):
                wr[1][j] = jnp.broadcast_to(m_new[j], wr[1].shape[1:])
        if finish_chunks:
            lv = lam_ref[...]
            lam = (jnp.exp(jnp.sum(lv[0:1, :] * lv[1:2, :], axis=1, keepdims=True))
                   - jnp.exp(jnp.sum(lv[2:3, :] * lv[3:4, :], axis=1, keepdims=True)) + lam_init)
            outs = [a[0:DF_DV, :] / a[DF_DV:DF_DV + 1, :] for a in acc]
            o = outs[0] - lam * outs[1]
            var = jnp.mean(o * o, axis=0, keepdims=True)
            on = o * lax.rsqrt(var + EPS) * g_ref[...] * (1.0 - lam_init)
            o_ref[...] = on.T.astype(o_ref.dtype)

    @pl.when(t == 0)
    def _():
        step([], None, ctx_chunks, slots[0])

    @pl.when(t == 1)
    def _():
        step(ctx_chunks, slots[0], all_chunks, slots[1])

    for parity in range(2):
        @pl.when((t % 2 == parity) & (t >= 2) & (t < nt))
        def _():
            step(all_chunks, slots[1 - parity], all_chunks, slots[parity])

    @pl.when((t == nt) & (t % 2 == 0))
    def _():
        step(all_chunks, slots[1], [], None)

    @pl.when((t == nt) & (t % 2 == 1))
    def _():
        step(all_chunks, slots[0], [], None)


def _diff_call(dfq, dfk, dfvx, lam_vecs, norm_g, lam_init, n_ctx):
    B, T, _ = dfq.shape
    nt = T // TT
    assert n_ctx % DF_KCHUNK == 0 and nt >= 2
    return pl.pallas_call(
        functools.partial(_diff_kernel, lam_init=lam_init, n_ctx=n_ctx),
        grid=(B, DF_HEADS, nt + 1),
        in_specs=[
            pl.BlockSpec((None, TT, DF_DV), lambda b, h, t: (b, jnp.minimum(t, nt - 1), h)),
            pl.BlockSpec((None, T, DF_DV), lambda b, h, t: (b, 0, h)),
            pl.BlockSpec((None, VEXT_ROWS, T), lambda b, h, t: (b, h, 0)),
            pl.BlockSpec((4, DF_DQK), lambda b, h, t: (0, 0)),
            pl.BlockSpec((DF_DV, 1), lambda b, h, t: (h, 0)),
        ],
        out_specs=pl.BlockSpec((None, TT, DF_DV), lambda b, h, t: (b, jnp.maximum(t - 1, 0), h)),
        out_shape=jax.ShapeDtypeStruct((B, T, DF_W), BF16),
        scratch_shapes=[pltpu.VMEM((2, T, TT), F32), pltpu.VMEM((2, T, TT), F32),
                        pltpu.VMEM((2, 8, TT), F32), pltpu.VMEM((2, 8, TT), F32)],
        compiler_params=_cparams(("parallel", "parallel", "arbitrary")),
        name="diff_attn",
    )(dfq, dfk, dfvx, lam_vecs, norm_g.reshape(DF_W, 1))


def _na_tables(rows):
    kh = min(NA_KH, rows)
    assert kh == NA_KH and rows >= NA_KROWS and rows % NA_QROWS == 0
    n_tiles = rows // NA_QROWS
    qc = np.arange(GRID_W)
    qstart = np.clip(qc - NA_KW // 2, 0, GRID_W - NA_KW)
    col_ok = (qc[None, :] >= qstart[:, None]) & (qc[None, :] < qstart[:, None] + NA_KW)
    drs, valids = [], []
    for tile in (0, 1, n_tiles - 1):
        r0 = tile * NA_QROWS
        start = int(np.clip(r0 - kh // 2, 0, rows - NA_KROWS))
        r = r0 + np.arange(NA_QROWS)
        rs = np.clip(r - kh // 2, 0, rows - kh)
        akr = start + np.arange(NA_KROWS)
        row_ok = (akr[None, :] >= rs[:, None]) & (akr[None, :] < rs[:, None] + kh)
        drs.append(np.clip(akr[None, :] - r[:, None] + NA_KH - 1, 0, 2 * NA_KH - 2))
        ok = row_ok[:, None, :, None] & col_ok[None, :, None, :]
        valids.append(ok.reshape(TT, NA_KROWS * GRID_W))
    return np.stack(drs), np.stack(valids)


def _na_bias_tables(rel_bias, rows):
    dr, valid = _na_tables(rows)
    L, H = rel_bias.shape[:2]
    pad = GRID_W - NA_KW
    padded = jnp.pad(rel_bias, ((0, 0), (0, 0), (0, 0), (pad, pad)), mode="edge")
    base = NA_KW - 1 + pad
    by_col = jnp.stack([padded[..., base - q:base - q + GRID_W] for q in range(GRID_W)], axis=-2)
    tabs = []
    for ty in range(3):
        slabs = jnp.stack([by_col[:, :, int(d)] for d in dr[ty].reshape(-1)], axis=2)
        slabs = slabs.reshape(L, H, NA_QROWS, NA_KROWS, GRID_W, GRID_W)
        tab = slabs.transpose(0, 1, 2, 4, 3, 5).reshape(L, H, TT, NA_KROWS * GRID_W)
        tabs.append(jnp.swapaxes(jnp.where(valid[ty], tab * LOG2E, NEG), 2, 3))
    return jnp.stack(tabs, axis=1).astype(F32)


def _na_kernel(q_ref, k_ref, vx_ref, bias_ref, o_ref, *, n_ctx, rows):
    t = pl.program_id(1)
    dn = (((1,), (1,)), ((), ()))

    def run(key_slices):
        def scores(hd):
            sl = slice(hd * NA_DH, (hd + 1) * NA_DH)
            q = q_ref[:, sl]
            parts = []
            for ks, add_bias in key_slices:
                s = lax.dot_general(k_ref[ks, sl], q, dn, preferred_element_type=F32)
                parts.append(s + bias_ref[hd] if add_bias else s)
            m = jnp.max(parts[0], axis=0, keepdims=True)
            for p in parts[1:]:
                m = jnp.maximum(m, jnp.max(p, axis=0, keepdims=True))
            return parts, m

        def finish(hd, parts, m):
            o = None
            for (ks, _), s in zip(key_slices, parts):
                e = jnp.exp2(s - m).astype(BF16)
                part = jnp.dot(vx_ref[hd * NA_VROWS:(hd + 1) * NA_VROWS, ks], e,
                               preferred_element_type=F32)
                o = part if o is None else o + part
            return o[0:NA_DH, :] / o[NA_DH:NA_DH + 1, :]

        prev = None
        done = []
        for hd in range(NA_HEADS + 1):
            cur = scores(hd) if hd < NA_HEADS else None
            if prev is not None:
                done.append(finish(hd - 1, *prev))
                if len(done) == 2:
                    c0 = (hd - 2) * NA_DH
                    o_ref[:, c0:c0 + 2 * NA_DH] = jnp.concatenate(done, axis=0).T.astype(o_ref.dtype)
                    done = []
            prev = cur

    @pl.when(t == 0)
    def _():
        run([(slice(0, n_ctx), False)])

    @pl.when(t != 0)
    def _():
        r0 = (t - 1) * NA_QROWS
        start = jnp.clip(r0 - NA_KH // 2, 0, rows - NA_KROWS)
        tok0 = pl.multiple_of(n_ctx + start * GRID_W, TT)
        run([(pl.ds(tok0, NA_KROWS * GRID_W), True), (slice(0, n_ctx), False)])


def _na_call(naq, nak, navx, bias_tab, n_ctx, l):
    B, T, _ = naq.shape
    nt = T // TT
    rows = (T - n_ctx) // GRID_W
    nk = NA_KROWS * GRID_W
    assert n_ctx % TT == 0 and (NA_KH // 2) % NA_QROWS == 0 and (rows - NA_KROWS) % NA_QROWS == 0

    def bias_idx(b, t):
        return (l, jnp.where(t <= 1, 0, jnp.where(t == nt - 1, 2, 1)), 0, 0, 0)

    return pl.pallas_call(
        functools.partial(_na_kernel, n_ctx=n_ctx, rows=rows),
        grid=(B, nt),
        in_specs=[
            pl.BlockSpec((None, TT, NA_W), lambda b, t: (b, t, 0)),
            pl.BlockSpec((None, T, NA_W), lambda b, t: (b, 0, 0)),
            pl.BlockSpec((None, NA_HEADS * NA_VROWS, T), lambda b, t: (b, 0, 0)),
            pl.BlockSpec((None, None, NA_HEADS, nk, TT), bias_idx),
        ],
        out_specs=pl.BlockSpec((None, TT, NA_W), lambda b, t: (b, t, 0)),
        out_shape=jax.ShapeDtypeStruct((B, T, NA_W), BF16),
        compiler_params=_cparams(("parallel", "arbitrary")),
        name="na_attn",
    )(naq, nak, navx, bias_tab)


def _merge_kernel(x_ref, mod_ref, g_ref, wg_ref, wb_ref, wo_ref, hf_ref, hb_ref, o_ref_in,
                  mlg_ref, bd_ref, cn_ref, out_ref):
    x = x_ref[...]
    D = x.shape[1]
    hn = _norm_mod(x, g_ref[...], mod_ref[3:4, :], mod_ref[4:5, :]).astype(BF16)
    hsum = hf_ref[...] + hb_ref[...]
    parts = []
    for hd in range(ML_HEADS):
        hh = hsum[:, hd * ML_DH:(hd + 1) * ML_DH]
        parts.append(hh * lax.rsqrt(jnp.mean(hh * hh, axis=1, keepdims=True) + EPS))
    a = (_sigmoid(o_ref_in[...]) * (jnp.concatenate(parts, axis=1) * mlg_ref[...])).astype(BF16)
    y = None
    for i, br in enumerate((a, bd_ref[...], cn_ref[...])):
        gate = _sigmoid(jnp.dot(hn, wg_ref[:, i * D:(i + 1) * D], preferred_element_type=F32))
        term = gate * jnp.dot(br, wb_ref[i], preferred_element_type=F32)
        y = term if y is None else y + term
    z = jnp.dot(y.astype(BF16), wo_ref[...], preferred_element_type=F32)
    out_ref[...] = x + mod_ref[5:6, :] * z


def _merge_call(x, modtab, g, w_gate, w_branch, w_out, hf, hb, mlf, ml_norm_g, bd, cn, l):
    B, T, D = x.shape

    def tile_spec(n, cblk=0):
        return pl.BlockSpec((None, TT, n), lambda b, t: (b, t, cblk))

    return pl.pallas_call(
        _merge_kernel,
        grid=(B, T // TT),
        in_specs=[
            tile_spec(D),
            _mod_spec(),
            pl.BlockSpec((1, D), lambda b, t: (0, 0)),
            pl.BlockSpec((None, D, N_BRANCH * D), lambda b, t: (l, 0, 0)),
            pl.BlockSpec((None, N_BRANCH, BRANCH_W, D), lambda b, t: (l, 0, 0, 0)),
            pl.BlockSpec((None, D, D), lambda b, t: (l, 0, 0)),
            tile_spec(ML_W), tile_spec(ML_W),
            tile_spec(ML_W, 2),
            pl.BlockSpec((1, ML_W), lambda b, t: (0, 0)),
            tile_spec(DF_W), tile_spec(NA_W),
        ],
        out_specs=tile_spec(D),
        out_shape=jax.ShapeDtypeStruct(x.shape, F32),
        compiler_params=_cparams(("parallel", "parallel")),
        name="merge",
    )(x, modtab, g, w_gate, w_branch, w_out, hf, hb, mlf, ml_norm_g, bd, cn)


def _rope_tables(S, n_ctx):
    t = jnp.arange(S)
    rowp = (t // GRID_W).astype(F32)
    colp = (t % GRID_W).astype(F32)
    half = DF_DQK // 2
    freqs = ROPE_BASE ** (-jnp.arange(0, half, 2, dtype=F32) / half)
    ang = jnp.concatenate([rowp[:, None] * freqs, colp[:, None] * freqs], axis=-1)
    cos, sin = jnp.cos(ang), jnp.sin(ang)
    cos = jnp.concatenate([jnp.ones((n_ctx, half), F32), cos], axis=0)
    sin = jnp.concatenate([jnp.zeros((n_ctx, half), F32), sin], axis=0)
    cos_t = jnp.tile(cos, (1, 4))
    sin_t = jnp.tile(jnp.concatenate([-sin, sin], axis=1), (1, 2))
    return cos_t, sin_t


def kernel(x, c, ctx, c_ctx, w_ada, b_ada, norm_g, ffn_w1, ffn_w2, w_in, ml_conv_w, ml_conv_b,
           ml_gate_b, ml_norm_g, df_lambda, df_norm_g, na_rel_bias, w_branch, w_out, final_g):
    B, S, D = x.shape
    n_ctx = ctx.shape[1]
    depth = w_ada.shape[0]
    assert D == D_MODEL and n_ctx == TT and S % TT == 0 and S % GRID_W == 0
    T = n_ctx + S
    rows = S // GRID_W

    w1 = ffn_w1.astype(BF16)
    w2 = ffn_w2.astype(BF16)
    w_tok, w_tr, w_gate = _split_w_in(w_in)
    wb = w_branch.astype(BF16)
    wo = w_out.astype(BF16)

    cvec = jnp.zeros((16, D), F32).at[0].set(c_ctx).at[1:B + 1].set(c)
    mod = _ada_call(cvec, w_ada, b_ada).reshape(depth, 16, 9, D)
    modtab = jnp.stack([jnp.broadcast_to(mod[:, 0:1], (depth, B, 9, D)), mod[:, 1:B + 1]], axis=2)

    cos_t, sin_t = _rope_tables(S, n_ctx)
    bias_tab = _na_bias_tables(na_rel_bias, rows)

    h = jnp.concatenate([ctx, x], axis=1)
    for l in range(depth):
        lam_init = 0.8 - 0.6 * math.exp(-0.3 * l)
        mt = modtab[l]
        h = _ffn_call(h, mt, norm_g[l, 0][None], w1, w2, l, 0, 0)
        mlf, mlvt, mlgt, dfq, dfk, dfvx, naq, nak, navx = _inproj_call(
            h, mt, norm_g[l, 1][None], w_tok, w_tr, cos_t, sin_t, l)
        qk = _conv_call(mlf, ml_conv_w[l], ml_conv_b[l][None], n_ctx)
        hf, hb = _mlstm_call(qk, mlvt, mlgt, ml_gate_b[l])
        bd = _diff_call(dfq, dfk, dfvx, df_lambda[l], df_norm_g[l], lam_init, n_ctx)
        cn = _na_call(naq, nak, navx, bias_tab, n_ctx, l)
        h = _merge_call(h, mt, norm_g[l, 1][None], w_gate, wb, wo, hf, hb, mlf,
                        ml_norm_g[l][None], bd, cn, l)
        last = l == depth - 1
        h = _ffn_call(h, mt, norm_g[l, 2][None], w1, w2, l, 1, 6,
                      final_g=final_g[None] if last else None, n_ctx=n_ctx)
    return h
```

```python
import functools
import math

import numpy as np
import jax
import jax.numpy as jnp
from jax import lax
from jax.experimental import pallas as pl
from jax.experimental.pallas import tpu as pltpu

F32 = jnp.float32
BF16 = jnp.bfloat16

D_MODEL = 1024
GRID_W = 64
N_BRANCH = 3
BRANCH_W = 512
ML_HEADS = 4
ML_DH = 128
ML_W = ML_HEADS * ML_DH
DF_HEADS = 4
DF_DQK = 64
DF_DV = 128
DF_W = DF_HEADS * DF_DV
ROPE_BASE = 10000.0
NA_HEADS = 8
NA_DH = 64
NA_W = NA_HEADS * NA_DH
NA_KH = 8
NA_KW = 16
D_FF = 2816
EPS = 1e-6
GATE_W = 4 * ML_HEADS

TT = 256
NA_QROWS = 4
NA_KROWS = 12
MXU_N = 256
FF_CHUNKS = ((0, 6 * MXU_N), (6 * MXU_N, D_FF))
DF_KCHUNK = MXU_N
ONES_ROWS = 16
VEXT_ROWS = DF_DV + ONES_ROWS
NA_VROWS = NA_DH + ONES_ROWS
NEG = -1e30
LOG2E = math.log2(math.e)
VMEM_LIMIT = 56 * 1024 * 1024


def _cparams(sem):
    return pltpu.CompilerParams(dimension_semantics=sem, vmem_limit_bytes=VMEM_LIMIT)


def _norm_mod(x, g, shift, scale):
    var = jnp.mean(x * x, axis=-1, keepdims=True)
    return x * lax.rsqrt(var + EPS) * g * (1.0 + scale) + shift


def _sigmoid(x):
    return 1.0 / (1.0 + jnp.exp(-x))


def _ada_kernel(c_ref, w_ref, b_ref, o_ref):
    c = c_ref[...]
    h = (c * _sigmoid(c)).astype(BF16)
    o_ref[...] = jnp.dot(h, w_ref[...].astype(BF16), preferred_element_type=F32) + b_ref[...]


def _ada_call(cvec, w_ada, b_ada):
    L, D, N = w_ada.shape
    tn = N // 8
    return pl.pallas_call(
        _ada_kernel,
        grid=(L, N // tn),
        in_specs=[
            pl.BlockSpec(cvec.shape, lambda l, j: (0, 0)),
            pl.BlockSpec((None, D, tn), lambda l, j: (l, 0, j)),
            pl.BlockSpec((None, 1, tn), lambda l, j: (l, 0, j)),
        ],
        out_specs=pl.BlockSpec((None, cvec.shape[0], tn), lambda l, j: (l, 0, j)),
        out_shape=jax.ShapeDtypeStruct((L, cvec.shape[0], N), F32),
        compiler_params=_cparams(("parallel", "parallel")),
        name="adaln",
    )(cvec, w_ada, b_ada.reshape(L, 1, N))


def _ffn_kernel(x_ref, mod_ref, g_ref, w1_ref, w2_ref, *rest, k0, final, split_in):
    o_ref = rest[-1]
    x = x_ref[...]
    if split_in:
        x = jnp.where(pl.program_id(1) == 0, rest[0][...], x)
    h = _norm_mod(x, g_ref[...], mod_ref[k0:k0 + 1, :], mod_ref[k0 + 1:k0 + 2, :]).astype(BF16)
    acc = jnp.zeros(x.shape, F32)
    for c0, c1 in FF_CHUNKS:
        a = jnp.dot(h, w1_ref[:, c0:c1], preferred_element_type=F32)
        b = jnp.dot(h, w1_ref[:, D_FF + c0:D_FF + c1], preferred_element_type=F32)
        u = (a * _sigmoid(a) * b).astype(BF16)
        acc = acc + jnp.dot(u, w2_ref[c0:c1, :], preferred_element_type=F32)
    y = x + 0.5 * mod_ref[k0 + 2:k0 + 3, :] * acc
    if final:
        y = y * lax.rsqrt(jnp.mean(y * y, axis=-1, keepdims=True) + EPS) * rest[-2][...]
    o_ref[...] = y


def _mod_spec():
    return pl.BlockSpec((None, None, 9, D_MODEL), lambda b, t: (b, jnp.minimum(t, 1), 0, 0))


def _ffn_call(x, modtab, g, w1, w2, l, i, k0, final_g=None, ctx_in=None, n_ctx=0):
    B, _, D = x.shape
    T = x.shape[1] + (n_ctx if ctx_in is not None else 0)
    final = final_g is not None
    off = n_ctx // TT
    if ctx_in is not None:
        x_spec = pl.BlockSpec((None, TT, D), lambda b, t: (b, jnp.maximum(t - off, 0), 0))
    else:
        x_spec = pl.BlockSpec((None, TT, D), lambda b, t: (b, t, 0))
    in_specs = [
        x_spec,
        _mod_spec(),
        pl.BlockSpec((1, D), lambda b, t: (0, 0)),
        pl.BlockSpec((None, None, D, 2 * D_FF), lambda b, t: (l, i, 0, 0)),
        pl.BlockSpec((None, None, D_FF, D), lambda b, t: (l, i, 0, 0)),
    ]
    args = [x, modtab, g, w1, w2]
    if ctx_in is not None:
        assert n_ctx == TT
        in_specs.append(pl.BlockSpec((None, TT, D), lambda b, t: (b, 0, 0)))
        args.append(ctx_in)
    if final:
        in_specs.append(pl.BlockSpec((1, D), lambda b, t: (0, 0)))
        args.append(final_g)
        out_spec = pl.BlockSpec((None, TT, D), lambda b, t: (b, jnp.maximum(t - off, 0), 0))
        out_shape = jax.ShapeDtypeStruct((B, T - n_ctx, D), F32)
    else:
        out_spec = pl.BlockSpec((None, TT, D), lambda b, t: (b, t, 0))
        out_shape = jax.ShapeDtypeStruct((B, T, D), F32)
    return pl.pallas_call(
        functools.partial(_ffn_kernel, k0=k0, final=final, split_in=ctx_in is not None),
        grid=(B, T // TT),
        in_specs=in_specs,
        out_specs=out_spec,
        out_shape=out_shape,
        compiler_params=_cparams(("parallel", "arbitrary")),
        name="ffn_half",
    )(*args)


def _inproj_kernel(x_ref, mod_ref, g_ref, w_ref, wt_ref, cos_ref, sin_ref,
                   mlf_ref, mlvt_ref, mlgt_ref, dfq_ref, dfk_ref, dfvx_ref, naq_ref, nak_ref, navx_ref):
    x = x_ref[...]
    h = _norm_mod(x, g_ref[...], mod_ref[3:4, :], mod_ref[4:5, :]).astype(BF16)

    def proj(c0, n):
        return jnp.dot(h, w_ref[:, c0:c0 + n], preferred_element_type=F32)

    def proj_t(r0, n):
        return lax.dot_general(wt_ref[r0:r0 + n, :], h, (((1,), (1,)), ((), ())),
                               preferred_element_type=F32)

    mlf_ref[...] = proj(0, 3 * ML_W)
    mlvt_ref[...] = proj_t(0, ML_W)
    mlgt_ref[...] = proj_t(ML_W + DF_W, GATE_W)
    ones_rows = jnp.where(lax.broadcasted_iota(jnp.int32, (ONES_ROWS, TT), 0) == 0, 1.0, 0.0).astype(BF16)
    for vx_ref, r0, heads, dh in ((dfvx_ref, ML_W, DF_HEADS, DF_DV),
                                  (navx_ref, ML_W + DF_W + GATE_W, NA_HEADS, NA_DH)):
        vt = proj_t(r0, heads * dh).astype(BF16)
        for hd in range(heads):
            r = hd * (dh + ONES_ROWS)
            vx_ref[r:r + dh, :] = vt[hd * dh:(hd + 1) * dh, :]
            vx_ref[r + dh:r + dh + ONES_ROWS, :] = ones_rows
    cos = jnp.tile(cos_ref[...], (1, DF_HEADS))
    sin = jnp.tile(sin_ref[...], (1, DF_HEADS))
    lane = lax.broadcasted_iota(jnp.int32, (TT, DF_W), 1)
    first_half = (lane % DF_DQK) < (DF_DQK // 2)

    def rope(v):
        partner = jnp.where(first_half, pltpu.roll(v, DF_W - DF_DQK // 2, 1),
                            pltpu.roll(v, DF_DQK // 2, 1))
        return v * cos + partner * sin

    c0 = 3 * ML_W
    dfq_ref[...] = (rope(proj(c0, DF_W)) * (DF_DQK ** -0.5 * LOG2E)).astype(BF16)
    dfk_ref[...] = rope(proj(c0 + DF_W, DF_W)).astype(BF16)
    c0 = c0 + 2 * DF_W
    naq_ref[...] = (proj(c0, NA_W) * (NA_DH ** -0.5 * LOG2E)).astype(BF16)
    nak_ref[...] = proj(c0 + NA_W, NA_W).astype(BF16)


def _split_w_in(w_in):
    o_mlv, o_mlo, o_mlg = 2 * ML_W, 3 * ML_W, 4 * ML_W
    o_dfq = o_mlg + GATE_W
    o_dfv = o_dfq + 2 * DF_W
    o_na = o_dfv + DF_W
    o_nav = o_na + 2 * NA_W
    o_gate = o_na + 3 * NA_W
    w_tok = jnp.concatenate([w_in[:, :, 0:o_mlv], w_in[:, :, o_mlo:o_mlg], w_in[:, :, o_dfq:o_dfv],
                             w_in[:, :, o_na:o_nav]], axis=-1)
    w_tr = jnp.concatenate([w_in[:, :, o_mlv:o_mlo], w_in[:, :, o_dfv:o_na], w_in[:, :, o_mlg:o_dfq],
                            w_in[:, :, o_nav:o_gate]], axis=-1)
    return (w_tok.astype(BF16), jnp.swapaxes(w_tr, 1, 2).astype(BF16), w_in[:, :, o_gate:].astype(BF16))


def _inproj_call(x, modtab, g, w_tok, w_tr, cos_t, sin_t, l):
    B, T, D = x.shape

    def tile_spec(n):
        return pl.BlockSpec((None, TT, n), lambda b, t: (b, t, 0))

    def sds(n, dt):
        return jax.ShapeDtypeStruct((B, T, n), dt)

    def tspec(n):
        return pl.BlockSpec((None, n, TT), lambda b, t: (b, 0, t))

    def tsds(n, dt):
        return jax.ShapeDtypeStruct((B, n, T), dt)

    return pl.pallas_call(
        _inproj_kernel,
        grid=(B, T // TT),
        in_specs=[
            tile_spec(D),
            _mod_spec(),
            pl.BlockSpec((1, D), lambda b, t: (0, 0)),
            pl.BlockSpec((None, D, w_tok.shape[-1]), lambda b, t: (l, 0, 0)),
            pl.BlockSpec((None, w_tr.shape[1], D), lambda b, t: (l, 0, 0)),
            pl.BlockSpec((TT, 2 * DF_DQK), lambda b, t: (t, 0)),
            pl.BlockSpec((TT, 2 * DF_DQK), lambda b, t: (t, 0)),
        ],
        out_specs=[tile_spec(3 * ML_W), tspec(ML_W), tspec(GATE_W),
                   tile_spec(DF_W), tile_spec(DF_W), tspec(DF_HEADS * VEXT_ROWS),
                   tile_spec(NA_W), tile_spec(NA_W), tspec(NA_HEADS * NA_VROWS)],
        out_shape=[sds(3 * ML_W, F32), tsds(ML_W, F32), tsds(GATE_W, F32),
                   sds(DF_W, BF16), sds(DF_W, BF16), tsds(DF_HEADS * VEXT_ROWS, BF16),
                   sds(NA_W, BF16), sds(NA_W, BF16), tsds(NA_HEADS * NA_VROWS, BF16)],
        compiler_params=_cparams(("parallel", "parallel")),
        name="in_proj",
    )(x, modtab, g, w_tok, w_tr, cos_t, sin_t)


def _conv_kernel(x_ref, w_ref, b_ref, o_ref, *, n_ctx):
    x = x_ref[...]
    T = x.shape[0]
    row = lax.broadcasted_iota(jnp.int32, x.shape, 0)
    prev = jnp.where((row == 0) | (row == n_ctx), 0.0, pltpu.roll(x, 1, 0))
    nxt = jnp.where((row == n_ctx - 1) | (row == T - 1), 0.0, pltpu.roll(x, T - 1, 0))
    y = prev * w_ref[0:1, :] + x * w_ref[1:2, :] + nxt * w_ref[2:3, :] + b_ref[...]
    y = y * _sigmoid(y)
    scale = jnp.where(pl.program_id(1) < ML_W // x.shape[1], ML_DH ** -0.5, 1.0)
    o_ref[...] = (y * scale).astype(BF16)


def _conv_call(mlf, conv_w, conv_b, n_ctx):
    B, T, _ = mlf.shape
    ct = 256
    return pl.pallas_call(
        functools.partial(_conv_kernel, n_ctx=n_ctx),
        grid=(B, 2 * ML_W // ct),
        in_specs=[
            pl.BlockSpec((None, T, ct), lambda b, c: (b, 0, c)),
            pl.BlockSpec((3, ct), lambda b, c: (0, c)),
            pl.BlockSpec((1, ct), lambda b, c: (0, c)),
        ],
        out_specs=pl.BlockSpec((None, T, ct), lambda b, c: (b, 0, c)),
        out_shape=jax.ShapeDtypeStruct((B, T, 2 * ML_W), BF16),
        compiler_params=_cparams(("parallel", "parallel")),
        name="ml_conv",
    )(mlf, conv_w, conv_b)


def _log_sigmoid(x):
    return jnp.minimum(x, 0.0) - jnp.log1p(jnp.exp(-jnp.abs(x)))


def _lane_scan(x, reverse):
    n = x.shape[1]
    lane = lax.broadcasted_iota(jnp.int32, x.shape, 1)
    k = 1
    while k < n:
        if reverse:
            x = x + jnp.where(lane < n - k, pltpu.roll(x, n - k, 1), 0.0)
        else:
            x = x + jnp.where(lane >= k, pltpu.roll(x, k, 1), 0.0)
        k *= 2
    return x


def _mlstm_kernel(qkf_ref, qkb_ref, vtf_ref, vtb_ref, gtf_ref, gtb_ref, gbias_ref,
                  hf_ref, hb_ref, c_scr, n_scr, m_scr):
    L = TT

    @pl.when(pl.program_id(1) == 0)
    def _():
        c_scr[...] = jnp.zeros(c_scr.shape, F32)
        n_scr[...] = jnp.zeros(n_scr.shape, F32)
        m_scr[...] = jnp.zeros(m_scr.shape, F32)

    key = lax.broadcasted_iota(jnp.int32, (L, L), 0)
    qry = lax.broadcasted_iota(jnp.int32, (L, L), 1)
    row8 = lax.broadcasted_iota(jnp.int32, (8, L), 0)
    nt_dims = (((1,), (1,)), ((), ()))

    chains = []
    dirs = ((qkf_ref, vtf_ref, gtf_ref, hf_ref), (qkb_ref, vtb_ref, gtb_ref, hb_ref))
    for d, (qk_ref, vt_ref, gt_ref, h_ref) in enumerate(dirs):
        gt = gt_ref[...] + gbias_ref[...]
        b_rows = _lane_scan(_log_sigmoid(gt), reverse=(d == 1))
        mask = (key <= qry) if d == 0 else (key >= qry)
        r_rows = gt - pltpu.roll(b_rows, GATE_W - ML_HEADS, 0)
        r_cols = jnp.concatenate([r_rows, jnp.zeros((128 - GATE_W, L), F32)], axis=0).T
        for hd in range(ML_HEADS):
            ci = d * 2 * ML_HEADS + hd
            b_row = b_rows[ci + ML_HEADS:ci + ML_HEADS + 1, :]
            chains.append(dict(
                si=d * ML_HEADS + hd, hd=hd, h_ref=h_ref, mask=mask,
                q=qk_ref[:, hd * ML_DH:(hd + 1) * ML_DH],
                k=qk_ref[:, ML_W + hd * ML_DH:ML_W + (hd + 1) * ML_DH],
                vt=vt_ref[hd * ML_DH:(hd + 1) * ML_DH, :],
                li_row=gt[ci:ci + 1, :], b_row=b_row, r_col=r_cols[:, ci:ci + 1],
                btot=b_row[:, L - 1:L] if d == 0 else b_row[:, 0:1]))

    for ch in chains:
        m = m_scr[ch["si"], 0:1, 0:1]
        dmat = jnp.where(ch["mask"], ch["r_col"] + ch["b_row"], -jnp.inf)
        inter = ch["b_row"] + m
        m_t = jnp.maximum(inter, jnp.max(dmat, axis=0, keepdims=True))
        ch.update(m=m, m_t=m_t, w=jnp.exp(dmat - m_t), s_inter=jnp.exp(inter - m_t),
                  s=lax.dot_general(ch["k"], ch["q"], nt_dims, preferred_element_type=F32))

    for ch in chains:
        si, hd, q = ch["si"], ch["hd"], ch["q"]
        cmat = c_scr[si]
        n8 = n_scr[si]
        qkw = ch["s"] * ch["w"]
        num = (jnp.dot(ch["vt"].astype(BF16), qkw.astype(BF16), preferred_element_type=F32)
               + ch["s_inter"] * lax.dot_general(cmat.astype(BF16), q, nt_dims,
                                                 preferred_element_type=F32))
        nq = lax.dot_general(n8.astype(BF16), q, nt_dims, preferred_element_type=F32)[0:1, :]
        den = jnp.sum(qkw, axis=0, keepdims=True) + ch["s_inter"] * nq
        ht = num / jnp.maximum(jnp.abs(den), jnp.exp(-ch["m_t"]))
        ch["h_ref"][:, hd * ML_DH:(hd + 1) * ML_DH] = ht.T

    for ch in chains:
        si, m, btot = ch["si"], ch["m"], ch["btot"]
        wk = btot - ch["b_row"] + ch["li_row"]
        m_new = jnp.maximum(btot + m, jnp.max(wk, axis=1, keepdims=True))
        decay = jnp.exp(btot + m - m_new)
        wk = jnp.exp(wk - m_new)
        c_scr[si] = decay * c_scr[si] + jnp.dot((ch["vt"] * wk).astype(BF16), ch["k"],
                                                preferred_element_type=F32)
        wk8 = jnp.where(row8 == 0, wk, 0.0).astype(BF16)
        n_scr[si] = decay * n_scr[si] + jnp.dot(wk8, ch["k"], preferred_element_type=F32)
        m_scr[si] = jnp.broadcast_to(m_new, m_scr.shape[1:])


def _mlstm_call(qk, mlvt, mlgt, gate_b):
    B, T, _ = qk.shape
    nt = T // TT

    def bwd_t(t):
        return jnp.where(t == 0, 0, nt - t)

    def spec(n, back):
        if back:
            return pl.BlockSpec((None, TT, n), lambda b, t: (b, bwd_t(t), 0))
        return pl.BlockSpec((None, TT, n), lambda b, t: (b, t, 0))

    def tspec(n, back):
        if back:
            return pl.BlockSpec((None, n, TT), lambda b, t: (b, 0, bwd_t(t)))
        return pl.BlockSpec((None, n, TT), lambda b, t: (b, 0, t))

    nch = 2 * ML_HEADS
    out_sds = jax.ShapeDtypeStruct((B, T, ML_W), F32)
    return pl.pallas_call(
        _mlstm_kernel,
        grid=(B, nt),
        in_specs=[
            spec(2 * ML_W, False), spec(2 * ML_W, True),
            tspec(ML_W, False), tspec(ML_W, True),
            tspec(GATE_W, False), tspec(GATE_W, True),
            pl.BlockSpec((GATE_W, 1), lambda b, t: (0, 0)),
        ],
        out_specs=[spec(ML_W, False), spec(ML_W, True)],
        out_shape=[out_sds, out_sds],
        scratch_shapes=[
            pltpu.VMEM((nch, ML_DH, ML_DH), F32),
            pltpu.VMEM((nch, 8, ML_DH), F32),
            pltpu.VMEM((nch, 8, 128), F32),
        ],
        compiler_params=_cparams(("parallel", "arbitrary")),
        name="mlstm_scan",
    )(qk, qk, mlvt, mlvt, mlgt, mlgt, gate_b.reshape(GATE_W, 1))


def _diff_kernel(q_ref, k_ref, vx_ref, lam_ref, g_ref, o_ref, sa_scr, sb_scr, ma_scr, mb_scr,
                 acca_scr, accb_scr, *, lam_init, n_ctx):
    t = pl.program_id(2)
    nt = pl.num_programs(2) - 2
    nk = k_ref.shape[0]
    dn = (((1,), (1,)), ((), ()))
    all_chunks = [(c, min(c + DF_KCHUNK, nk)) for c in range(0, nk, DF_KCHUNK)]
    ctx_chunks = [ch for ch in all_chunks if ch[0] < n_ctx]
    slots = ((sa_scr, ma_scr), (sb_scr, mb_scr))
    accs = (acca_scr, accb_scr)

    def step(out_acc, finish_chunks, rd, acc_wr, score_chunks, wr):
        if out_acc is not None:
            lv = lam_ref[...]
            lam = (jnp.exp(jnp.sum(lv[0:1, :] * lv[1:2, :], axis=1, keepdims=True))
                   - jnp.exp(jnp.sum(lv[2:3, :] * lv[3:4, :], axis=1, keepdims=True)) + lam_init)
            outs = [out_acc[j, 0:DF_DV, :] / out_acc[j, DF_DV:DF_DV + 1, :] for j in range(2)]
            o = outs[0] - lam * outs[1]
            var = jnp.mean(o * o, axis=0, keepdims=True)
            on = o * lax.rsqrt(var + EPS) * g_ref[...] * (1.0 - lam_init)
            o_ref[...] = on.T.astype(o_ref.dtype)
        acc = [None, None]
        m_new = [None, None]
        if score_chunks:
            q = q_ref[...]
            lane = lax.broadcasted_iota(jnp.int32, q.shape, 1)
            zero = jnp.zeros(q.shape, q.dtype)
            qms = (jnp.where(lane < DF_DQK, q, zero), jnp.where(lane < DF_DQK, zero, q))
        if finish_chunks:
            m_old = [rd[1][j, 0:1, :] for j in range(2)]
        for i in range(max(len(finish_chunks), len(score_chunks))):
            for j in range(2):
                if i < len(finish_chunks):
                    c0, c1 = finish_chunks[i]
                    e = jnp.exp2(rd[0][j, c0:c1, :] - m_old[j]).astype(BF16)
                    part = jnp.dot(vx_ref[:, c0:c1], e, preferred_element_type=F32)
                    acc[j] = part if acc[j] is None else acc[j] + part
                if i < len(score_chunks):
                    c0, c1 = score_chunks[i]
                    s = lax.dot_general(k_ref[c0:c1, :], qms[j], dn, preferred_element_type=F32)
                    wr[0][j, c0:c1, :] = s
                    cm = jnp.max(s, axis=0, keepdims=True)
                    m_new[j] = cm if m_new[j] is None else jnp.maximum(m_new[j], cm)
        if score_chunks:
            for j in range(2):
                wr[1][j] = jnp.broadcast_to(m_new[j], wr[1].shape[1:])
        if finish_chunks:
            for j in range(2):
                acc_wr[j] = acc[j]

    @pl.when(t == 0)
    def _():
        step(None, [], None, None, ctx_chunks, slots[0])

    @pl.when(t == 1)
    def _():
        step(None, ctx_chunks, slots[0], accs[0], all_chunks, slots[1])

    for p in range(2):
        @pl.when((t % 2 == p) & (t >= 2) & (t < nt))
        def _():
            step(accs[p], all_chunks, slots[1 - p], accs[1 - p], all_chunks, slots[p])

        @pl.when((t % 2 == p) & (t == nt))
        def _():
            step(accs[p], all_chunks, slots[1 - p], accs[1 - p], [], None)

        @pl.when((t % 2 == p) & (t == nt + 1))
        def _():
            step(accs[p], [], None, None, [], None)


def _diff_call(dfq, dfk, dfvx, lam_vecs, norm_g, lam_init, n_ctx):
    B, T, _ = dfq.shape
    nt = T // TT
    assert n_ctx % DF_KCHUNK == 0 and nt >= 2
    return pl.pallas_call(
        functools.partial(_diff_kernel, lam_init=lam_init, n_ctx=n_ctx),
        grid=(B, DF_HEADS, nt + 2),
        in_specs=[
            pl.BlockSpec((None, TT, DF_DV), lambda b, h, t: (b, jnp.minimum(t, nt - 1), h)),
            pl.BlockSpec((None, T, DF_DV), lambda b, h, t: (b, 0, h)),
            pl.BlockSpec((None, VEXT_ROWS, T), lambda b, h, t: (b, h, 0)),
            pl.BlockSpec((4, DF_DQK), lambda b, h, t: (0, 0)),
            pl.BlockSpec((DF_DV, 1), lambda b, h, t: (h, 0)),
        ],
        out_specs=pl.BlockSpec((None, TT, DF_DV), lambda b, h, t: (b, jnp.maximum(t - 2, 0), h)),
        out_shape=jax.ShapeDtypeStruct((B, T, DF_W), BF16),
        scratch_shapes=[pltpu.VMEM((2, T, TT), F32), pltpu.VMEM((2, T, TT), F32),
                        pltpu.VMEM((2, 8, TT), F32), pltpu.VMEM((2, 8, TT), F32),
                        pltpu.VMEM((2, VEXT_ROWS, TT), F32), pltpu.VMEM((2, VEXT_ROWS, TT), F32)],
        compiler_params=_cparams(("parallel", "parallel", "arbitrary")),
        name="diff_attn",
    )(dfq, dfk, dfvx, lam_vecs, norm_g.reshape(DF_W, 1))


def _na_tables(rows):
    kh = min(NA_KH, rows)
    assert kh == NA_KH and rows >= NA_KROWS and rows % NA_QROWS == 0
    n_tiles = rows // NA_QROWS
    qc = np.arange(GRID_W)
    qstart = np.clip(qc - NA_KW // 2, 0, GRID_W - NA_KW)
    col_ok = (qc[None, :] >= qstart[:, None]) & (qc[None, :] < qstart[:, None] + NA_KW)
    drs, valids = [], []
    for tile in (0, 1, n_tiles - 1):
        r0 = tile * NA_QROWS
        start = int(np.clip(r0 - kh // 2, 0, rows - NA_KROWS))
        r = r0 + np.arange(NA_QROWS)
        rs = np.clip(r - kh // 2, 0, rows - kh)
        akr = start + np.arange(NA_KROWS)
        row_ok = (akr[None, :] >= rs[:, None]) & (akr[None, :] < rs[:, None] + kh)
        drs.append(np.clip(akr[None, :] - r[:, None] + NA_KH - 1, 0, 2 * NA_KH - 2))
        ok = row_ok[:, None, :, None] & col_ok[None, :, None, :]
        valids.append(ok.reshape(TT, NA_KROWS * GRID_W))
    return np.stack(drs), np.stack(valids)


def _na_bias_tables(rel_bias, rows):
    dr, valid = _na_tables(rows)
    L, H = rel_bias.shape[:2]
    pad = GRID_W - NA_KW
    flipped = jnp.flip(jnp.pad(rel_bias.astype(F32) * LOG2E, ((0, 0), (0, 0), (0, 0), (pad, pad)),
                               mode="edge"), axis=-1)
    off = flipped.shape[-1] - 1 - (NA_KW - 1 + pad)
    by_col = jnp.stack([flipped[..., off - kc:off - kc + GRID_W] for kc in range(GRID_W)], axis=-2)
    tabs = []
    for ty in range(3):
        per_kr = [jnp.stack([by_col[:, :, int(dr[ty][qr, kr])] for qr in range(NA_QROWS)], axis=3)
                  for kr in range(NA_KROWS)]
        tab = jnp.stack(per_kr, axis=2).reshape(L, H, NA_KROWS * GRID_W, TT)
        tabs.append(jnp.where(valid[ty].T, tab, NEG))
    return jnp.stack(tabs, axis=1)


def _na_kernel(q_ref, k_ref, vx_ref, bias_ref, o_ref, *, n_ctx, rows):
    t = pl.program_id(1)
    dn = (((1,), (1,)), ((), ()))
    tile_type = jnp.where(t <= 1, 0, jnp.where(t == pl.num_programs(1) - 1, 2, 1))

    def run(key_slices):
        def scores(hd):
            sl = slice(hd * NA_DH, (hd + 1) * NA_DH)
            q = q_ref[:, sl]
            parts = []
            for ks, add_bias in key_slices:
                s = lax.dot_general(k_ref[ks, sl], q, dn, preferred_element_type=F32)
                parts.append(s + bias_ref[tile_type, hd] if add_bias else s)
            m = jnp.max(parts[0], axis=0, keepdims=True)
            for p in parts[1:]:
                m = jnp.maximum(m, jnp.max(p, axis=0, keepdims=True))
            return parts, m

        def finish(hd, parts, m):
            o = None
            for (ks, _), s in zip(key_slices, parts):
                e = jnp.exp2(s - m).astype(BF16)
                part = jnp.dot(vx_ref[hd * NA_VROWS:(hd + 1) * NA_VROWS, ks], e,
                               preferred_element_type=F32)
                o = part if o is None else o + part
            return o[0:NA_DH, :] / o[NA_DH:NA_DH + 1, :]

        prev = None
        done = []
        for hd in range(NA_HEADS + 1):
            cur = scores(hd) if hd < NA_HEADS else None
            if prev is not None:
                done.append(finish(hd - 1, *prev))
                if len(done) == 2:
                    c0 = (hd - 2) * NA_DH
                    o_ref[:, c0:c0 + 2 * NA_DH] = jnp.concatenate(done, axis=0).T.astype(o_ref.dtype)
                    done = []
            prev = cur

    @pl.when(t == 0)
    def _():
        run([(slice(0, n_ctx), False)])

    @pl.when(t != 0)
    def _():
        r0 = (t - 1) * NA_QROWS
        start = jnp.clip(r0 - NA_KH // 2, 0, rows - NA_KROWS)
        tok0 = pl.multiple_of(n_ctx + start * GRID_W, TT)
        run([(pl.ds(tok0, NA_KROWS * GRID_W), True), (slice(0, n_ctx), False)])


def _na_call(naq, nak, navx, bias_tab, n_ctx, l):
    B, T, _ = naq.shape
    nt = T // TT
    rows = (T - n_ctx) // GRID_W
    nk = NA_KROWS * GRID_W
    assert n_ctx % TT == 0 and (NA_KH // 2) % NA_QROWS == 0 and (rows - NA_KROWS) % NA_QROWS == 0

    return pl.pallas_call(
        functools.partial(_na_kernel, n_ctx=n_ctx, rows=rows),
        grid=(B, nt),
        in_specs=[
            pl.BlockSpec((None, TT, NA_W), lambda b, t: (b, t, 0)),
            pl.BlockSpec((None, T, NA_W), lambda b, t: (b, 0, 0)),
            pl.BlockSpec((None, NA_HEADS * NA_VROWS, T), lambda b, t: (b, 0, 0)),
            pl.BlockSpec((None, 3, NA_HEADS, nk, TT), lambda b, t: (l, 0, 0, 0, 0),
                         pipeline_mode=pl.Buffered(1)),
        ],
        out_specs=pl.BlockSpec((None, TT, NA_W), lambda b, t: (b, t, 0)),
        out_shape=jax.ShapeDtypeStruct((B, T, NA_W), BF16),
        compiler_params=_cparams(("parallel", "arbitrary")),
        name="na_attn",
    )(naq, nak, navx, bias_tab)


def _merge_kernel(x_ref, mod_ref, g_ref, wg_ref, wb_ref, wo_ref, hf_ref, hb_ref, o_ref_in,
                  mlg_ref, bd_ref, cn_ref, out_ref):
    x = x_ref[...]
    D = x.shape[1]
    hn = _norm_mod(x, g_ref[...], mod_ref[3:4, :], mod_ref[4:5, :]).astype(BF16)
    hsum = hf_ref[...] + hb_ref[...]
    parts = []
    for hd in range(ML_HEADS):
        hh = hsum[:, hd * ML_DH:(hd + 1) * ML_DH]
        parts.append(hh * lax.rsqrt(jnp.mean(hh * hh, axis=1, keepdims=True) + EPS))
    a = (_sigmoid(o_ref_in[...]) * (jnp.concatenate(parts, axis=1) * mlg_ref[...])).astype(BF16)
    y = None
    for i, br in enumerate((a, bd_ref[...], cn_ref[...])):
        gate = _sigmoid(jnp.dot(hn, wg_ref[:, i * D:(i + 1) * D], preferred_element_type=F32))
        term = gate * jnp.dot(br, wb_ref[i], preferred_element_type=F32)
        y = term if y is None else y + term
    z = jnp.dot(y.astype(BF16), wo_ref[...], preferred_element_type=F32)
    out_ref[...] = x + mod_ref[5:6, :] * z


def _merge_call(x, modtab, g, w_gate, w_branch, w_out, hf, hb, mlf, ml_norm_g, bd, cn, l):
    B, T, D = x.shape

    def tile_spec(n, cblk=0):
        return pl.BlockSpec((None, TT, n), lambda b, t: (b, t, cblk))

    return pl.pallas_call(
        _merge_kernel,
        grid=(B, T // TT),
        in_specs=[
            tile_spec(D),
            _mod_spec(),
            pl.BlockSpec((1, D), lambda b, t: (0, 0)),
            pl.BlockSpec((None, D, N_BRANCH * D), lambda b, t: (l, 0, 0)),
            pl.BlockSpec((None, N_BRANCH, BRANCH_W, D), lambda b, t: (l, 0, 0, 0)),
            pl.BlockSpec((None, D, D), lambda b, t: (l, 0, 0)),
            tile_spec(ML_W), tile_spec(ML_W),
            tile_spec(ML_W, 2),
            pl.BlockSpec((1, ML_W), lambda b, t: (0, 0)),
            tile_spec(DF_W), tile_spec(NA_W),
        ],
        out_specs=tile_spec(D),
        out_shape=jax.ShapeDtypeStruct(x.shape, F32),
        compiler_params=_cparams(("parallel", "parallel")),
        name="merge",
    )(x, modtab, g, w_gate, w_branch, w_out, hf, hb, mlf, ml_norm_g, bd, cn)


def _rope_tables(S, n_ctx):
    t = jnp.arange(S)
    rowp = (t // GRID_W).astype(F32)
    colp = (t % GRID_W).astype(F32)
    half = DF_DQK // 2
    freqs = ROPE_BASE ** (-jnp.arange(0, half, 2, dtype=F32) / half)
    ang = jnp.concatenate([rowp[:, None] * freqs, colp[:, None] * freqs], axis=-1)
    cos, sin = jnp.cos(ang), jnp.sin(ang)
    cos = jnp.concatenate([jnp.ones((n_ctx, half), F32), cos], axis=0)
    sin = jnp.concatenate([jnp.zeros((n_ctx, half), F32), sin], axis=0)
    cos_t = jnp.tile(cos, (1, 4))
    sin_t = jnp.tile(jnp.concatenate([-sin, sin], axis=1), (1, 2))
    return cos_t, sin_t


def kernel(x, c, ctx, c_ctx, w_ada, b_ada, norm_g, ffn_w1, ffn_w2, w_in, ml_conv_w, ml_conv_b,
           ml_gate_b, ml_norm_g, df_lambda, df_norm_g, na_rel_bias, w_branch, w_out, final_g):
    B, S, D = x.shape
    n_ctx = ctx.shape[1]
    depth = w_ada.shape[0]
    assert D == D_MODEL and n_ctx == TT and S % TT == 0 and S % GRID_W == 0
    T = n_ctx + S
    rows = S // GRID_W

    w1 = ffn_w1.astype(BF16)
    w2 = ffn_w2.astype(BF16)
    w_tok, w_tr, w_gate = _split_w_in(w_in)
    wb = w_branch.astype(BF16)
    wo = w_out.astype(BF16)

    cvec = jnp.zeros((16, D), F32).at[0].set(c_ctx).at[1:B + 1].set(c)
    mod = _ada_call(cvec, w_ada, b_ada).reshape(depth, 16, 9, D)
    modtab = jnp.stack([jnp.broadcast_to(mod[:, 0:1], (depth, B, 9, D)), mod[:, 1:B + 1]], axis=2)

    cos_t, sin_t = _rope_tables(S, n_ctx)
    bias_tab = _na_bias_tables(na_rel_bias, rows)

    h = x
    for l in range(depth):
        lam_init = 0.8 - 0.6 * math.exp(-0.3 * l)
        mt = modtab[l]
        h = _ffn_call(h, mt, norm_g[l, 0][None], w1, w2, l, 0, 0,
                      ctx_in=ctx if l == 0 else None, n_ctx=n_ctx)
        mlf, mlvt, mlgt, dfq, dfk, dfvx, naq, nak, navx = _inproj_call(
            h, mt, norm_g[l, 1][None], w_tok, w_tr, cos_t, sin_t, l)
        qk = _conv_call(mlf, ml_conv_w[l], ml_conv_b[l][None], n_ctx)
        hf, hb = _mlstm_call(qk, mlvt, mlgt, ml_gate_b[l])
        bd = _diff_call(dfq, dfk, dfvx, df_lambda[l], df_norm_g[l], lam_init, n_ctx)
        cn = _na_call(naq, nak, navx, bias_tab, n_ctx, l)
        h = _merge_call(h, mt, norm_g[l, 1][None], w_gate, wb, wo, hf, hb, mlf,
                        ml_norm_g[l][None], bd, cn, l)
        last = l == depth - 1
        h = _ffn_call(h, mt, norm_g[l, 2][None], w1, w2, l, 1, 6,
                      final_g=final_g[None] if last else None, n_ctx=n_ctx)
    return h
```

```python
import functools
import math

import numpy as np
import jax
import jax.numpy as jnp
from jax import lax
from jax.experimental import pallas as pl
from jax.experimental.pallas import tpu as pltpu

F32 = jnp.float32
BF16 = jnp.bfloat16

D_MODEL = 1024
GRID_W = 64
N_BRANCH = 3
BRANCH_W = 512
ML_HEADS = 4
ML_DH = 128
ML_W = ML_HEADS * ML_DH
DF_HEADS = 4
DF_DQK = 64
DF_DV = 128
DF_W = DF_HEADS * DF_DV
ROPE_BASE = 10000.0
NA_HEADS = 8
NA_DH = 64
NA_W = NA_HEADS * NA_DH
NA_KH = 8
NA_KW = 16
D_FF = 2816
EPS = 1e-6
GATE_W = 4 * ML_HEADS

TT = 256
NA_QROWS = 4
NA_KROWS = 12
MXU_N = 256
FF_CHUNKS = ((0, 6 * MXU_N), (6 * MXU_N, D_FF))
DF_KCHUNK = MXU_N
ONES_ROWS = 16
VEXT_ROWS = DF_DV + ONES_ROWS
NA_VROWS = NA_DH + ONES_ROWS
NEG = -1e30
LOG2E = math.log2(math.e)
VMEM_LIMIT = 56 * 1024 * 1024


def _cparams(sem):
    return pltpu.CompilerParams(dimension_semantics=sem, vmem_limit_bytes=VMEM_LIMIT)


def _norm_mod(x, g, shift, scale):
    var = jnp.mean(x * x, axis=-1, keepdims=True)
    return x * lax.rsqrt(var + EPS) * g * (1.0 + scale) + shift


def _sigmoid(x):
    return 1.0 / (1.0 + jnp.exp(-x))


def _ada_kernel(c_ref, w_ref, b_ref, o_ref):
    c = c_ref[...]
    h = (c * _sigmoid(c)).astype(BF16)
    o_ref[...] = jnp.dot(h, w_ref[...].astype(BF16), preferred_element_type=F32) + b_ref[...]


def _ada_call(cvec, w_ada, b_ada):
    L, D, N = w_ada.shape
    tn = N // 8
    return pl.pallas_call(
        _ada_kernel,
        grid=(L, N // tn),
        in_specs=[
            pl.BlockSpec(cvec.shape, lambda l, j: (0, 0)),
            pl.BlockSpec((None, D, tn), lambda l, j: (l, 0, j)),
            pl.BlockSpec((None, 1, tn), lambda l, j: (l, 0, j)),
        ],
        out_specs=pl.BlockSpec((None, cvec.shape[0], tn), lambda l, j: (l, 0, j)),
        out_shape=jax.ShapeDtypeStruct((L, cvec.shape[0], N), F32),
        compiler_params=_cparams(("parallel", "parallel")),
        name="adaln",
    )(cvec, w_ada, b_ada.reshape(L, 1, N))


def _ffn_kernel(x_ref, mod_ref, g_ref, w1_ref, w2_ref, *rest, k0, final, split_in):
    o_ref = rest[-1]
    x = x_ref[...]
    if split_in:
        x = jnp.where(pl.program_id(1) == 0, rest[0][...], x)
    h = _norm_mod(x, g_ref[...], mod_ref[k0:k0 + 1, :], mod_ref[k0 + 1:k0 + 2, :]).astype(BF16)
    acc = jnp.zeros(x.shape, F32)
    for c0, c1 in FF_CHUNKS:
        a = jnp.dot(h, w1_ref[:, c0:c1], preferred_element_type=F32)
        b = jnp.dot(h, w1_ref[:, D_FF + c0:D_FF + c1], preferred_element_type=F32)
        u = (a * _sigmoid(a) * b).astype(BF16)
        acc = acc + jnp.dot(u, w2_ref[c0:c1, :], preferred_element_type=F32)
    y = x + 0.5 * mod_ref[k0 + 2:k0 + 3, :] * acc
    if final:
        y = y * lax.rsqrt(jnp.mean(y * y, axis=-1, keepdims=True) + EPS) * rest[-2][...]
    o_ref[...] = y


def _mod_spec():
    return pl.BlockSpec((None, None, 9, D_MODEL), lambda b, t: (b, jnp.minimum(t, 1), 0, 0))


def _ffn_call(x, modtab, g, w1, w2, l, i, k0, final_g=None, ctx_in=None, n_ctx=0):
    B, _, D = x.shape
    T = x.shape[1] + (n_ctx if ctx_in is not None else 0)
    final = final_g is not None
    off = n_ctx // TT
    if ctx_in is not None:
        x_spec = pl.BlockSpec((None, TT, D), lambda b, t: (b, jnp.maximum(t - off, 0), 0))
    else:
        x_spec = pl.BlockSpec((None, TT, D), lambda b, t: (b, t, 0))
    in_specs = [
        x_spec,
        _mod_spec(),
        pl.BlockSpec((1, D), lambda b, t: (0, 0)),
        pl.BlockSpec((None, None, D, 2 * D_FF), lambda b, t: (l, i, 0, 0)),
        pl.BlockSpec((None, None, D_FF, D), lambda b, t: (l, i, 0, 0)),
    ]
    args = [x, modtab, g, w1, w2]
    if ctx_in is not None:
        assert n_ctx == TT
        in_specs.append(pl.BlockSpec((None, TT, D), lambda b, t: (b, 0, 0)))
        args.append(ctx_in)
    if final:
        in_specs.append(pl.BlockSpec((1, D), lambda b, t: (0, 0)))
        args.append(final_g)
        out_spec = pl.BlockSpec((None, TT, D), lambda b, t: (b, jnp.maximum(t - off, 0), 0))
        out_shape = jax.ShapeDtypeStruct((B, T - n_ctx, D), F32)
    else:
        out_spec = pl.BlockSpec((None, TT, D), lambda b, t: (b, t, 0))
        out_shape = jax.ShapeDtypeStruct((B, T, D), F32)
    return pl.pallas_call(
        functools.partial(_ffn_kernel, k0=k0, final=final, split_in=ctx_in is not None),
        grid=(B, T // TT),
        in_specs=in_specs,
        out_specs=out_spec,
        out_shape=out_shape,
        compiler_params=_cparams(("parallel", "arbitrary")),
        name="ffn_half",
    )(*args)


def _inproj_kernel(x_ref, mod_ref, g_ref, w_ref, wt_ref, cos_ref, sin_ref,
                   mlf_ref, mlvt_ref, mlgt_ref, dfq_ref, dfk_ref, dfvx_ref, naq_ref, nak_ref, navx_ref):
    x = x_ref[...]
    h = _norm_mod(x, g_ref[...], mod_ref[3:4, :], mod_ref[4:5, :]).astype(BF16)

    def proj(c0, n):
        return jnp.dot(h, w_ref[:, c0:c0 + n], preferred_element_type=F32)

    def proj_t(r0, n):
        return lax.dot_general(wt_ref[r0:r0 + n, :], h, (((1,), (1,)), ((), ())),
                               preferred_element_type=F32)

    mlf_ref[...] = proj(0, 3 * ML_W)
    mlvt_ref[...] = proj_t(0, ML_W)
    mlgt_ref[...] = proj_t(ML_W + DF_W, GATE_W)
    ones_rows = jnp.where(lax.broadcasted_iota(jnp.int32, (ONES_ROWS, TT), 0) == 0, 1.0, 0.0).astype(BF16)
    for vx_ref, r0, heads, dh in ((dfvx_ref, ML_W, DF_HEADS, DF_DV),
                                  (navx_ref, ML_W + DF_W + GATE_W, NA_HEADS, NA_DH)):
        vt = proj_t(r0, heads * dh).astype(BF16)
        for hd in range(heads):
            r = hd * (dh + ONES_ROWS)
            vx_ref[r:r + dh, :] = vt[hd * dh:(hd + 1) * dh, :]
            vx_ref[r + dh:r + dh + ONES_ROWS, :] = ones_rows
    cos = jnp.tile(cos_ref[...], (1, DF_HEADS))
    sin = jnp.tile(sin_ref[...], (1, DF_HEADS))
    lane = lax.broadcasted_iota(jnp.int32, (TT, DF_W), 1)
    first_half = (lane % DF_DQK) < (DF_DQK // 2)

    def rope(v):
        partner = jnp.where(first_half, pltpu.roll(v, DF_W - DF_DQK // 2, 1),
                            pltpu.roll(v, DF_DQK // 2, 1))
        return v * cos + partner * sin

    c0 = 3 * ML_W
    dfq_ref[...] = (rope(proj(c0, DF_W)) * (DF_DQK ** -0.5 * LOG2E)).astype(BF16)
    dfk_ref[...] = rope(proj(c0 + DF_W, DF_W)).astype(BF16)
    c0 = c0 + 2 * DF_W
    naq_ref[...] = (proj(c0, NA_W) * (NA_DH ** -0.5 * LOG2E)).astype(BF16)
    nak_ref[...] = proj(c0 + NA_W, NA_W).astype(BF16)


def _split_w_in(w_in):
    o_mlv, o_mlo, o_mlg = 2 * ML_W, 3 * ML_W, 4 * ML_W
    o_dfq = o_mlg + GATE_W
    o_dfv = o_dfq + 2 * DF_W
    o_na = o_dfv + DF_W
    o_nav = o_na + 2 * NA_W
    o_gate = o_na + 3 * NA_W
    w_tok = jnp.concatenate([w_in[:, :, 0:o_mlv], w_in[:, :, o_mlo:o_mlg], w_in[:, :, o_dfq:o_dfv],
                             w_in[:, :, o_na:o_nav]], axis=-1)
    w_tr = jnp.concatenate([w_in[:, :, o_mlv:o_mlo], w_in[:, :, o_dfv:o_na], w_in[:, :, o_mlg:o_dfq],
                            w_in[:, :, o_nav:o_gate]], axis=-1)
    return (w_tok.astype(BF16), jnp.swapaxes(w_tr, 1, 2).astype(BF16), w_in[:, :, o_gate:].astype(BF16))


def _inproj_call(x, modtab, g, w_tok, w_tr, cos_t, sin_t, l):
    B, T, D = x.shape

    def tile_spec(n):
        return pl.BlockSpec((None, TT, n), lambda b, t: (b, t, 0))

    def sds(n, dt):
        return jax.ShapeDtypeStruct((B, T, n), dt)

    def tspec(n):
        return pl.BlockSpec((None, n, TT), lambda b, t: (b, 0, t))

    def tsds(n, dt):
        return jax.ShapeDtypeStruct((B, n, T), dt)

    return pl.pallas_call(
        _inproj_kernel,
        grid=(B, T // TT),
        in_specs=[
            tile_spec(D),
            _mod_spec(),
            pl.BlockSpec((1, D), lambda b, t: (0, 0)),
            pl.BlockSpec((None, D, w_tok.shape[-1]), lambda b, t: (l, 0, 0)),
            pl.BlockSpec((None, w_tr.shape[1], D), lambda b, t: (l, 0, 0)),
            pl.BlockSpec((TT, 2 * DF_DQK), lambda b, t: (t, 0)),
            pl.BlockSpec((TT, 2 * DF_DQK), lambda b, t: (t, 0)),
        ],
        out_specs=[tile_spec(3 * ML_W), tspec(ML_W), tspec(GATE_W),
                   tile_spec(DF_W), tile_spec(DF_W), tspec(DF_HEADS * VEXT_ROWS),
                   tile_spec(NA_W), tile_spec(NA_W), tspec(NA_HEADS * NA_VROWS)],
        out_shape=[sds(3 * ML_W, F32), tsds(ML_W, F32), tsds(GATE_W, F32),
                   sds(DF_W, BF16), sds(DF_W, BF16), tsds(DF_HEADS * VEXT_ROWS, BF16),
                   sds(NA_W, BF16), sds(NA_W, BF16), tsds(NA_HEADS * NA_VROWS, BF16)],
        compiler_params=_cparams(("parallel", "parallel")),
        name="in_proj",
    )(x, modtab, g, w_tok, w_tr, cos_t, sin_t)


def _conv_kernel(x_ref, w_ref, b_ref, o_ref, *, n_ctx):
    x = x_ref[...]
    T = x.shape[0]
    row = lax.broadcasted_iota(jnp.int32, x.shape, 0)
    prev = jnp.where((row == 0) | (row == n_ctx), 0.0, pltpu.roll(x, 1, 0))
    nxt = jnp.where((row == n_ctx - 1) | (row == T - 1), 0.0, pltpu.roll(x, T - 1, 0))
    y = prev * w_ref[0:1, :] + x * w_ref[1:2, :] + nxt * w_ref[2:3, :] + b_ref[...]
    y = y * _sigmoid(y)
    scale = jnp.where(pl.program_id(1) < ML_W // x.shape[1], ML_DH ** -0.5, 1.0)
    o_ref[...] = (y * scale).astype(BF16)


def _conv_call(mlf, conv_w, conv_b, n_ctx):
    B, T, _ = mlf.shape
    ct = 256
    return pl.pallas_call(
        functools.partial(_conv_kernel, n_ctx=n_ctx),
        grid=(B, 2 * ML_W // ct),
        in_specs=[
            pl.BlockSpec((None, T, ct), lambda b, c: (b, 0, c)),
            pl.BlockSpec((3, ct), lambda b, c: (0, c)),
            pl.BlockSpec((1, ct), lambda b, c: (0, c)),
        ],
        out_specs=pl.BlockSpec((None, T, ct), lambda b, c: (b, 0, c)),
        out_shape=jax.ShapeDtypeStruct((B, T, 2 * ML_W), BF16),
        compiler_params=_cparams(("parallel", "parallel")),
        name="ml_conv",
    )(mlf, conv_w, conv_b)


def _log_sigmoid(x):
    return jnp.minimum(x, 0.0) - jnp.log1p(jnp.exp(-jnp.abs(x)))


def _segment_scan(x, seg, reverse):
    n = x.shape[1]
    pos = lax.broadcasted_iota(jnp.int32, x.shape, 1) % seg
    k = 1
    while k < seg:
        if reverse:
            x = x + jnp.where(pos < seg - k, pltpu.roll(x, n - k, 1), 0.0)
        else:
            x = x + jnp.where(pos >= k, pltpu.roll(x, k, 1), 0.0)
        k *= 2
    return x


def _gate_prep_kernel(gt_ref, gbias_ref, g_ref, b_ref, r_ref):
    gt = gt_ref[...] + gbias_ref[...]
    lf = _log_sigmoid(gt)
    row = lax.broadcasted_iota(jnp.int32, gt.shape, 0)
    b_rows = jnp.where(row < GATE_W // 2, _segment_scan(lf, TT, False), _segment_scan(lf, TT, True))
    r_rows = gt - pltpu.roll(b_rows, GATE_W - ML_HEADS, 0)
    g_ref[...] = gt
    b_ref[...] = b_rows
    r_ref[...] = jnp.concatenate([r_rows, jnp.zeros((128 - GATE_W, gt.shape[1]), F32)], axis=0).T


def _gate_prep_call(mlgt, gate_b):
    B, _, T = mlgt.shape
    row_spec = pl.BlockSpec((None, GATE_W, T), lambda b: (b, 0, 0))
    row_sds = jax.ShapeDtypeStruct((B, GATE_W, T), F32)
    return pl.pallas_call(
        _gate_prep_kernel,
        grid=(B,),
        in_specs=[row_spec, pl.BlockSpec((GATE_W, 1), lambda b: (0, 0))],
        out_specs=[row_spec, row_spec, pl.BlockSpec((None, T, 128), lambda b: (b, 0, 0))],
        out_shape=[row_sds, row_sds, jax.ShapeDtypeStruct((B, T, 128), F32)],
        compiler_params=_cparams(("parallel",)),
        name="ml_gates",
    )(mlgt, gate_b.reshape(GATE_W, 1))


def _mlstm_kernel(qkf_ref, qkb_ref, vtf_ref, vtb_ref, gf_ref, gb_ref, bf_ref, bb_ref, rf_ref, rb_ref,
                  hf_ref, hb_ref, c_scr, n_scr, m_scr):
    L = TT

    @pl.when(pl.program_id(1) == 0)
    def _():
        c_scr[...] = jnp.zeros(c_scr.shape, F32)
        n_scr[...] = jnp.zeros(n_scr.shape, F32)
        m_scr[...] = jnp.zeros(m_scr.shape, F32)

    key = lax.broadcasted_iota(jnp.int32, (L, L), 0)
    qry = lax.broadcasted_iota(jnp.int32, (L, L), 1)
    row8 = lax.broadcasted_iota(jnp.int32, (8, L), 0)
    nt_dims = (((1,), (1,)), ((), ()))

    chains = []
    dirs = ((qkf_ref, vtf_ref, gf_ref, bf_ref, rf_ref, hf_ref),
            (qkb_ref, vtb_ref, gb_ref, bb_ref, rb_ref, hb_ref))
    for d, (qk_ref, vt_ref, g_ref, b_ref, r_ref, h_ref) in enumerate(dirs):
        gt = g_ref[...]
        b_rows = b_ref[...]
        r_cols = r_ref[...]
        mask = (key <= qry) if d == 0 else (key >= qry)
        for hd in range(ML_HEADS):
            ci = d * 2 * ML_HEADS + hd
            b_row = b_rows[ci + ML_HEADS:ci + ML_HEADS + 1, :]
            chains.append(dict(
                si=d * ML_HEADS + hd, hd=hd, h_ref=h_ref, mask=mask,
                q=qk_ref[:, hd * ML_DH:(hd + 1) * ML_DH],
                k=qk_ref[:, ML_W + hd * ML_DH:ML_W + (hd + 1) * ML_DH],
                vt=vt_ref[hd * ML_DH:(hd + 1) * ML_DH, :],
                li_row=gt[ci:ci + 1, :], b_row=b_row, r_col=r_cols[:, ci:ci + 1],
                btot=b_row[:, L - 1:L] if d == 0 else b_row[:, 0:1]))

    for ch in chains:
        m = m_scr[ch["si"], 0:1, 0:1]
        dmat = jnp.where(ch["mask"], ch["r_col"] + ch["b_row"], -jnp.inf)
        inter = ch["b_row"] + m
        m_t = jnp.maximum(inter, jnp.max(dmat, axis=0, keepdims=True))
        ch.update(m=m, m_t=m_t, w=jnp.exp(dmat - m_t), s_inter=jnp.exp(inter - m_t),
                  s=lax.dot_general(ch["k"], ch["q"], nt_dims, preferred_element_type=F32))

    for ch in chains:
        si, hd, q = ch["si"], ch["hd"], ch["q"]
        cmat = c_scr[si]
        n8 = n_scr[si]
        qkw = ch["s"] * ch["w"]
        num = (jnp.dot(ch["vt"].astype(BF16), qkw.astype(BF16), preferred_element_type=F32)
               + ch["s_inter"] * lax.dot_general(cmat.astype(BF16), q, nt_dims,
                                                 preferred_element_type=F32))
        nq = lax.dot_general(n8.astype(BF16), q, nt_dims, preferred_element_type=F32)[0:1, :]
        den = jnp.sum(qkw, axis=0, keepdims=True) + ch["s_inter"] * nq
        ht = num / jnp.maximum(jnp.abs(den), jnp.exp(-ch["m_t"]))
        ch["h_ref"][:, hd * ML_DH:(hd + 1) * ML_DH] = ht.T

    for ch in chains:
        si, m, btot = ch["si"], ch["m"], ch["btot"]
        wk = btot - ch["b_row"] + ch["li_row"]
        m_new = jnp.maximum(btot + m, jnp.max(wk, axis=1, keepdims=True))
        decay = jnp.exp(btot + m - m_new)
        wk = jnp.exp(wk - m_new)
        c_scr[si] = decay * c_scr[si] + jnp.dot((ch["vt"] * wk).astype(BF16), ch["k"],
                                                preferred_element_type=F32)
        wk8 = jnp.where(row8 == 0, wk, 0.0).astype(BF16)
        n_scr[si] = decay * n_scr[si] + jnp.dot(wk8, ch["k"], preferred_element_type=F32)
        m_scr[si] = jnp.broadcast_to(m_new, m_scr.shape[1:])


def _mlstm_call(qk, mlvt, gates, b_rows, r_cols):
    B, T, _ = qk.shape
    nt = T // TT

    def bwd_t(t):
        return jnp.where(t == 0, 0, nt - t)

    def spec(n, back):
        if back:
            return pl.BlockSpec((None, TT, n), lambda b, t: (b, bwd_t(t), 0))
        return pl.BlockSpec((None, TT, n), lambda b, t: (b, t, 0))

    def tspec(n, back):
        if back:
            return pl.BlockSpec((None, n, TT), lambda b, t: (b, 0, bwd_t(t)))
        return pl.BlockSpec((None, n, TT), lambda b, t: (b, 0, t))

    nch = 2 * ML_HEADS
    out_sds = jax.ShapeDtypeStruct((B, T, ML_W), F32)
    return pl.pallas_call(
        _mlstm_kernel,
        grid=(B, nt),
        in_specs=[
            spec(2 * ML_W, False), spec(2 * ML_W, True),
            tspec(ML_W, False), tspec(ML_W, True),
            tspec(GATE_W, False), tspec(GATE_W, True),
            tspec(GATE_W, False), tspec(GATE_W, True),
            spec(128, False), spec(128, True),
        ],
        out_specs=[spec(ML_W, False), spec(ML_W, True)],
        out_shape=[out_sds, out_sds],
        scratch_shapes=[
            pltpu.VMEM((nch, ML_DH, ML_DH), F32),
            pltpu.VMEM((nch, 8, ML_DH), F32),
            pltpu.VMEM((nch, 8, 128), F32),
        ],
        compiler_params=_cparams(("parallel", "arbitrary")),
        name="mlstm_scan",
    )(qk, qk, mlvt, mlvt, gates, gates, b_rows, b_rows, r_cols, r_cols)


def _diff_kernel(q_ref, k_ref, vx_ref, lam_ref, g_ref, o_ref, sa_scr, sb_scr, ma_scr, mb_scr,
                 acca_scr, accb_scr, *, lam_init, n_ctx):
    t = pl.program_id(2)
    nt = pl.num_programs(2) - 2
    nk = k_ref.shape[0]
    dn = (((1,), (1,)), ((), ()))
    all_chunks = [(c, min(c + DF_KCHUNK, nk)) for c in range(0, nk, DF_KCHUNK)]
    ctx_chunks = [ch for ch in all_chunks if ch[0] < n_ctx]
    slots = ((sa_scr, ma_scr), (sb_scr, mb_scr))
    accs = (acca_scr, accb_scr)

    def step(out_acc, finish_chunks, rd, acc_wr, score_chunks, wr):
        if out_acc is not None:
            lv = lam_ref[...]
            lam = (jnp.exp(jnp.sum(lv[0:1, :] * lv[1:2, :], axis=1, keepdims=True))
                   - jnp.exp(jnp.sum(lv[2:3, :] * lv[3:4, :], axis=1, keepdims=True)) + lam_init)
            outs = [out_acc[j, 0:DF_DV, :] / out_acc[j, DF_DV:DF_DV + 1, :] for j in range(2)]
            o = outs[0] - lam * outs[1]
            var = jnp.mean(o * o, axis=0, keepdims=True)
            on = o * lax.rsqrt(var + EPS) * g_ref[...] * (1.0 - lam_init)
            o_ref[...] = on.T.astype(o_ref.dtype)
        acc = [None, None]
        m_new = [None, None]
        if score_chunks:
            q = q_ref[...]
            lane = lax.broadcasted_iota(jnp.int32, q.shape, 1)
            zero = jnp.zeros(q.shape, q.dtype)
            qms = (jnp.where(lane < DF_DQK, q, zero), jnp.where(lane < DF_DQK, zero, q))
        if finish_chunks:
            m_old = [rd[1][j, 0:1, :] for j in range(2)]
        for i in range(max(len(finish_chunks), len(score_chunks))):
            for j in range(2):
                if i < len(finish_chunks):
                    c0, c1 = finish_chunks[i]
                    e = jnp.exp2(rd[0][j, c0:c1, :] - m_old[j]).astype(BF16)
                    part = jnp.dot(vx_ref[:, c0:c1], e, preferred_element_type=F32)
                    acc[j] = part if acc[j] is None else acc[j] + part
                if i < len(score_chunks):
                    c0, c1 = score_chunks[i]
                    s = lax.dot_general(k_ref[c0:c1, :], qms[j], dn, preferred_element_type=F32)
                    wr[0][j, c0:c1, :] = s
                    cm = jnp.max(s, axis=0, keepdims=True)
                    m_new[j] = cm if m_new[j] is None else jnp.maximum(m_new[j], cm)
        if score_chunks:
            for j in range(2):
                wr[1][j] = jnp.broadcast_to(m_new[j], wr[1].shape[1:])
        if finish_chunks:
            for j in range(2):
                acc_wr[j] = acc[j]

    @pl.when(t == 0)
    def _():
        step(None, [], None, None, ctx_chunks, slots[0])

    @pl.when(t == 1)
    def _():
        step(None, ctx_chunks, slots[0], accs[0], all_chunks, slots[1])

    for p in range(2):
        @pl.when((t % 2 == p) & (t >= 2) & (t < nt))
        def _():
            step(accs[p], all_chunks, slots[1 - p], accs[1 - p], all_chunks, slots[p])

        @pl.when((t % 2 == p) & (t == nt))
        def _():
            step(accs[p], all_chunks, slots[1 - p], accs[1 - p], [], None)

        @pl.when((t % 2 == p) & (t == nt + 1))
        def _():
            step(accs[p], [], None, None, [], None)


def _diff_call(dfq, dfk, dfvx, lam_vecs, norm_g, lam_init, n_ctx):
    B, T, _ = dfq.shape
    nt = T // TT
    assert n_ctx % DF_KCHUNK == 0 and nt >= 2
    return pl.pallas_call(
        functools.partial(_diff_kernel, lam_init=lam_init, n_ctx=n_ctx),
        grid=(B, DF_HEADS, nt + 2),
        in_specs=[
            pl.BlockSpec((None, TT, DF_DV), lambda b, h, t: (b, jnp.minimum(t, nt - 1), h)),
            pl.BlockSpec((None, T, DF_DV), lambda b, h, t: (b, 0, h)),
            pl.BlockSpec((None, VEXT_ROWS, T), lambda b, h, t: (b, h, 0)),
            pl.BlockSpec((4, DF_DQK), lambda b, h, t: (0, 0)),
            pl.BlockSpec((DF_DV, 1), lambda b, h, t: (h, 0)),
        ],
        out_specs=pl.BlockSpec((None, TT, DF_DV), lambda b, h, t: (b, jnp.maximum(t - 2, 0), h)),
        out_shape=jax.ShapeDtypeStruct((B, T, DF_W), BF16),
        scratch_shapes=[pltpu.VMEM((2, T, TT), F32), pltpu.VMEM((2, T, TT), F32),
                        pltpu.VMEM((2, 8, TT), F32), pltpu.VMEM((2, 8, TT), F32),
                        pltpu.VMEM((2, VEXT_ROWS, TT), F32), pltpu.VMEM((2, VEXT_ROWS, TT), F32)],
        compiler_params=_cparams(("parallel", "parallel", "arbitrary")),
        name="diff_attn",
    )(dfq, dfk, dfvx, lam_vecs, norm_g.reshape(DF_W, 1))


def _na_tables(rows):
    kh = min(NA_KH, rows)
    assert kh == NA_KH and rows >= NA_KROWS and rows % NA_QROWS == 0
    n_tiles = rows // NA_QROWS
    qc = np.arange(GRID_W)
    qstart = np.clip(qc - NA_KW // 2, 0, GRID_W - NA_KW)
    col_ok = (qc[None, :] >= qstart[:, None]) & (qc[None, :] < qstart[:, None] + NA_KW)
    drs, valids = [], []
    for tile in (0, 1, n_tiles - 1):
        r0 = tile * NA_QROWS
        start = int(np.clip(r0 - kh // 2, 0, rows - NA_KROWS))
        r = r0 + np.arange(NA_QROWS)
        rs = np.clip(r - kh // 2, 0, rows - kh)
        akr = start + np.arange(NA_KROWS)
        row_ok = (akr[None, :] >= rs[:, None]) & (akr[None, :] < rs[:, None] + kh)
        drs.append(np.clip(akr[None, :] - r[:, None] + NA_KH - 1, 0, 2 * NA_KH - 2))
        ok = row_ok[:, None, :, None] & col_ok[None, :, None, :]
        valids.append(ok.reshape(TT, NA_KROWS * GRID_W))
    return np.stack(drs), np.stack(valids)


def _na_bias_tables(rel_bias, rows):
    dr, valid = _na_tables(rows)
    L, H = rel_bias.shape[:2]
    pad = GRID_W - NA_KW
    flipped = jnp.flip(jnp.pad(rel_bias.astype(F32) * LOG2E, ((0, 0), (0, 0), (0, 0), (pad, pad)),
                               mode="edge"), axis=-1)
    off = flipped.shape[-1] - 1 - (NA_KW - 1 + pad)
    by_col = jnp.stack([flipped[..., off - kc:off - kc + GRID_W] for kc in range(GRID_W)], axis=-2)
    tabs = []
    for ty in range(3):
        per_kr = [jnp.stack([by_col[:, :, int(dr[ty][qr, kr])] for qr in range(NA_QROWS)], axis=3)
                  for kr in range(NA_KROWS)]
        tab = jnp.stack(per_kr, axis=2).reshape(L, H, NA_KROWS * GRID_W, TT)
        tabs.append(jnp.where(valid[ty].T, tab, NEG))
    return jnp.stack(tabs, axis=1)


def _na_kernel(q_ref, k_ref, vx_ref, bias_ref, o_ref, *, n_ctx, rows):
    t = pl.program_id(1)
    dn = (((1,), (1,)), ((), ()))
    tile_type = jnp.where(t <= 1, 0, jnp.where(t == pl.num_programs(1) - 1, 2, 1))

    def run(key_slices):
        def scores(hd):
            sl = slice(hd * NA_DH, (hd + 1) * NA_DH)
            q = q_ref[:, sl]
            parts = []
            for ks, add_bias in key_slices:
                s = lax.dot_general(k_ref[ks, sl], q, dn, preferred_element_type=F32)
                parts.append(s + bias_ref[tile_type, hd] if add_bias else s)
            m = jnp.max(parts[0], axis=0, keepdims=True)
            for p in parts[1:]:
                m = jnp.maximum(m, jnp.max(p, axis=0, keepdims=True))
            return parts, m

        def finish(hd, parts, m):
            o = None
            for (ks, _), s in zip(key_slices, parts):
                e = jnp.exp2(s - m).astype(BF16)
                part = jnp.dot(vx_ref[hd * NA_VROWS:(hd + 1) * NA_VROWS, ks], e,
                               preferred_element_type=F32)
                o = part if o is None else o + part
            return o[0:NA_DH, :] / o[NA_DH:NA_DH + 1, :]

        prev = None
        done = []
        for hd in range(NA_HEADS + 1):
            cur = scores(hd) if hd < NA_HEADS else None
            if prev is not None:
                done.append(finish(hd - 1, *prev))
                if len(done) == 2:
                    c0 = (hd - 2) * NA_DH
                    o_ref[:, c0:c0 + 2 * NA_DH] = jnp.concatenate(done, axis=0).T.astype(o_ref.dtype)
                    done = []
            prev = cur

    @pl.when(t == 0)
    def _():
        run([(slice(0, n_ctx), False)])

    @pl.when(t != 0)
    def _():
        r0 = (t - 1) * NA_QROWS
        start = jnp.clip(r0 - NA_KH // 2, 0, rows - NA_KROWS)
        tok0 = pl.multiple_of(n_ctx + start * GRID_W, TT)
        run([(pl.ds(tok0, NA_KROWS * GRID_W), True), (slice(0, n_ctx), False)])


def _na_call(naq, nak, navx, bias_tab, n_ctx, l):
    B, T, _ = naq.shape
    nt = T // TT
    rows = (T - n_ctx) // GRID_W
    nk = NA_KROWS * GRID_W
    assert n_ctx % TT == 0 and (NA_KH // 2) % NA_QROWS == 0 and (rows - NA_KROWS) % NA_QROWS == 0

    return pl.pallas_call(
        functools.partial(_na_kernel, n_ctx=n_ctx, rows=rows),
        grid=(B, nt),
        in_specs=[
            pl.BlockSpec((None, TT, NA_W), lambda b, t: (b, t, 0)),
            pl.BlockSpec((None, T, NA_W), lambda b, t: (b, 0, 0)),
            pl.BlockSpec((None, NA_HEADS * NA_VROWS, T), lambda b, t: (b, 0, 0)),
            pl.BlockSpec((None, 3, NA_HEADS, nk, TT), lambda b, t: (l, 0, 0, 0, 0),
                         pipeline_mode=pl.Buffered(1)),
        ],
        out_specs=pl.BlockSpec((None, TT, NA_W), lambda b, t: (b, t, 0)),
        out_shape=jax.ShapeDtypeStruct((B, T, NA_W), BF16),
        compiler_params=_cparams(("parallel", "arbitrary")),
        name="na_attn",
    )(naq, nak, navx, bias_tab)


def _merge_kernel(x_ref, mod_ref, g_ref, wg_ref, wb_ref, wo_ref, hf_ref, hb_ref, o_ref_in,
                  mlg_ref, bd_ref, cn_ref, out_ref):
    x = x_ref[...]
    D = x.shape[1]
    hn = _norm_mod(x, g_ref[...], mod_ref[3:4, :], mod_ref[4:5, :]).astype(BF16)
    hsum = hf_ref[...] + hb_ref[...]
    parts = []
    for hd in range(ML_HEADS):
        hh = hsum[:, hd * ML_DH:(hd + 1) * ML_DH]
        parts.append(hh * lax.rsqrt(jnp.mean(hh * hh, axis=1, keepdims=True) + EPS))
    a = (_sigmoid(o_ref_in[...]) * (jnp.concatenate(parts, axis=1) * mlg_ref[...])).astype(BF16)
    y = None
    for i, br in enumerate((a, bd_ref[...], cn_ref[...])):
        gate = _sigmoid(jnp.dot(hn, wg_ref[:, i * D:(i + 1) * D], preferred_element_type=F32))
        term = gate * jnp.dot(br, wb_ref[i], preferred_element_type=F32)
        y = term if y is None else y + term
    z = jnp.dot(y.astype(BF16), wo_ref[...], preferred_element_type=F32)
    out_ref[...] = x + mod_ref[5:6, :] * z


def _merge_call(x, modtab, g, w_gate, w_branch, w_out, hf, hb, mlf, ml_norm_g, bd, cn, l):
    B, T, D = x.shape

    def tile_spec(n, cblk=0):
        return pl.BlockSpec((None, TT, n), lambda b, t: (b, t, cblk))

    return pl.pallas_call(
        _merge_kernel,
        grid=(B, T // TT),
        in_specs=[
            tile_spec(D),
            _mod_spec(),
            pl.BlockSpec((1, D), lambda b, t: (0, 0)),
            pl.BlockSpec((None, D, N_BRANCH * D), lambda b, t: (l, 0, 0)),
            pl.BlockSpec((None, N_BRANCH, BRANCH_W, D), lambda b, t: (l, 0, 0, 0)),
            pl.BlockSpec((None, D, D), lambda b, t: (l, 0, 0)),
            tile_spec(ML_W), tile_spec(ML_W),
            tile_spec(ML_W, 2),
            pl.BlockSpec((1, ML_W), lambda b, t: (0, 0)),
            tile_spec(DF_W), tile_spec(NA_W),
        ],
        out_specs=tile_spec(D),
        out_shape=jax.ShapeDtypeStruct(x.shape, F32),
        compiler_params=_cparams(("parallel", "parallel")),
        name="merge",
    )(x, modtab, g, w_gate, w_branch, w_out, hf, hb, mlf, ml_norm_g, bd, cn)


def _rope_tables(S, n_ctx):
    t = jnp.arange(S)
    rowp = (t // GRID_W).astype(F32)
    colp = (t % GRID_W).astype(F32)
    half = DF_DQK // 2
    freqs = ROPE_BASE ** (-jnp.arange(0, half, 2, dtype=F32) / half)
    ang = jnp.concatenate([rowp[:, None] * freqs, colp[:, None] * freqs], axis=-1)
    cos, sin = jnp.cos(ang), jnp.sin(ang)
    cos = jnp.concatenate([jnp.ones((n_ctx, half), F32), cos], axis=0)
    sin = jnp.concatenate([jnp.zeros((n_ctx, half), F32), sin], axis=0)
    cos_t = jnp.tile(cos, (1, 4))
    sin_t = jnp.tile(jnp.concatenate([-sin, sin], axis=1), (1, 2))
    return cos_t, sin_t


def kernel(x, c, ctx, c_ctx, w_ada, b_ada, norm_g, ffn_w1, ffn_w2, w_in, ml_conv_w, ml_conv_b,
           ml_gate_b, ml_norm_g, df_lambda, df_norm_g, na_rel_bias, w_branch, w_out, final_g):
    B, S, D = x.shape
    n_ctx = ctx.shape[1]
    depth = w_ada.shape[0]
    assert D == D_MODEL and n_ctx == TT and S % TT == 0 and S % GRID_W == 0
    T = n_ctx + S
    rows = S // GRID_W

    w1 = ffn_w1.astype(BF16)
    w2 = ffn_w2.astype(BF16)
    w_tok, w_tr, w_gate = _split_w_in(w_in)
    wb = w_branch.astype(BF16)
    wo = w_out.astype(BF16)

    cvec = jnp.zeros((16, D), F32).at[0].set(c_ctx).at[1:B + 1].set(c)
    mod = _ada_call(cvec, w_ada, b_ada).reshape(depth, 16, 9, D)
    modtab = jnp.stack([jnp.broadcast_to(mod[:, 0:1], (depth, B, 9, D)), mod[:, 1:B + 1]], axis=2)

    cos_t, sin_t = _rope_tables(S, n_ctx)
    bias_tab = _na_bias_tables(na_rel_bias, rows)

    h = x
    for l in range(depth):
        lam_init = 0.8 - 0.6 * math.exp(-0.3 * l)
        mt = modtab[l]
        h = _ffn_call(h, mt, norm_g[l, 0][None], w1, w2, l, 0, 0,
                      ctx_in=ctx if l == 0 else None, n_ctx=n_ctx)
        mlf, mlvt, mlgt, dfq, dfk, dfvx, naq, nak, navx = _inproj_call(
            h, mt, norm_g[l, 1][None], w_tok, w_tr, cos_t, sin_t, l)
        qk = _conv_call(mlf, ml_conv_w[l], ml_conv_b[l][None], n_ctx)
        hf, hb = _mlstm_call(qk, mlvt, *_gate_prep_call(mlgt, ml_gate_b[l]))
        bd = _diff_call(dfq, dfk, dfvx, df_lambda[l], df_norm_g[l], lam_init, n_ctx)
        cn = _na_call(naq, nak, navx, bias_tab, n_ctx, l)
        h = _merge_call(h, mt, norm_g[l, 1][None], w_gate, wb, wo, hf, hb, mlf,
                        ml_norm_g[l][None], bd, cn, l)
        last = l == depth - 1
        h = _ffn_call(h, mt, norm_g[l, 2][None], w1, w2, l, 1, 6,
                      final_g=final_g[None] if last else None, n_ctx=n_ctx)
    return h
```

```python
import functools
import math

import numpy as np
import jax
import jax.numpy as jnp
from jax import lax
from jax.experimental import pallas as pl
from jax.experimental.pallas import tpu as pltpu

F32 = jnp.float32
BF16 = jnp.bfloat16

D_MODEL = 1024
GRID_W = 64
N_BRANCH = 3
BRANCH_W = 512
ML_HEADS = 4
ML_DH = 128
ML_W = ML_HEADS * ML_DH
DF_HEADS = 4
DF_DQK = 64
DF_DV = 128
DF_W = DF_HEADS * DF_DV
ROPE_BASE = 10000.0
NA_HEADS = 8
NA_DH = 64
NA_W = NA_HEADS * NA_DH
NA_KH = 8
NA_KW = 16
D_FF = 2816
EPS = 1e-6
GATE_W = 4 * ML_HEADS

DENSE_NB = 2
TT = 256
NA_QROWS = 4
NA_KROWS = 12
MXU_N = 256
FF_CHUNKS = ((0, 6 * MXU_N), (6 * MXU_N, D_FF))
DF_KCHUNK = MXU_N
ONES_ROWS = 16
VEXT_ROWS = DF_DV + ONES_ROWS
NA_VROWS = NA_DH + ONES_ROWS
NEG = -1e30
LOG2E = math.log2(math.e)
VMEM_LIMIT = 56 * 1024 * 1024


def _cparams(sem):
    return pltpu.CompilerParams(dimension_semantics=sem, vmem_limit_bytes=VMEM_LIMIT)


def _norm_mod(x, g, shift, scale):
    var = jnp.mean(x * x, axis=-1, keepdims=True)
    return x * lax.rsqrt(var + EPS) * g * (1.0 + scale) + shift


def _sigmoid(x):
    return 1.0 / (1.0 + jnp.exp(-x))


def _ada_kernel(c_ref, w_ref, b_ref, o_ref):
    c = c_ref[...]
    h = (c * _sigmoid(c)).astype(BF16)
    o_ref[...] = jnp.dot(h, w_ref[...].astype(BF16), preferred_element_type=F32) + b_ref[...]


def _ada_call(cvec, w_ada, b_ada):
    L, D, N = w_ada.shape
    tn = N // 8
    return pl.pallas_call(
        _ada_kernel,
        grid=(L, N // tn),
        in_specs=[
            pl.BlockSpec(cvec.shape, lambda l, j: (0, 0)),
            pl.BlockSpec((None, D, tn), lambda l, j: (l, 0, j)),
            pl.BlockSpec((None, 1, tn), lambda l, j: (l, 0, j)),
        ],
        out_specs=pl.BlockSpec((None, cvec.shape[0], tn), lambda l, j: (l, 0, j)),
        out_shape=jax.ShapeDtypeStruct((L, cvec.shape[0], N), F32),
        compiler_params=_cparams(("parallel", "parallel")),
        name="adaln",
    )(cvec, w_ada, b_ada.reshape(L, 1, N))


def _ffn_kernel(x_ref, mod_ref, g_ref, w1_ref, w2_ref, *rest, k0, final, split_in):
    o_ref = rest[-1]
    for i in range(DENSE_NB):
        x = x_ref[i]
        if split_in:
            x = jnp.where(pl.program_id(1) == 0, rest[0][i], x)
        h = _norm_mod(x, g_ref[...], mod_ref[i, k0:k0 + 1, :], mod_ref[i, k0 + 1:k0 + 2, :]).astype(BF16)
        acc = jnp.zeros(x.shape, F32)
        for c0, c1 in FF_CHUNKS:
            a = jnp.dot(h, w1_ref[:, c0:c1], preferred_element_type=F32)
            b = jnp.dot(h, w1_ref[:, D_FF + c0:D_FF + c1], preferred_element_type=F32)
            u = (a * _sigmoid(a) * b).astype(BF16)
            acc = acc + jnp.dot(u, w2_ref[c0:c1, :], preferred_element_type=F32)
        y = x + 0.5 * mod_ref[i, k0 + 2:k0 + 3, :] * acc
        if final:
            y = y * lax.rsqrt(jnp.mean(y * y, axis=-1, keepdims=True) + EPS) * rest[-2][...]
        o_ref[i] = y


def _mod_spec(nb=None):
    return pl.BlockSpec((nb, None, 9, D_MODEL), lambda b, t: (b, jnp.minimum(t, 1), 0, 0))


def _ffn_call(x, modtab, g, w1, w2, l, i, k0, final_g=None, ctx_in=None, n_ctx=0):
    B, _, D = x.shape
    T = x.shape[1] + (n_ctx if ctx_in is not None else 0)
    final = final_g is not None
    off = n_ctx // TT
    if ctx_in is not None:
        x_spec = pl.BlockSpec((DENSE_NB, TT, D), lambda b, t: (b, jnp.maximum(t - off, 0), 0))
    else:
        x_spec = pl.BlockSpec((DENSE_NB, TT, D), lambda b, t: (b, t, 0))
    assert B % DENSE_NB == 0
    in_specs = [
        x_spec,
        _mod_spec(DENSE_NB),
        pl.BlockSpec((1, D), lambda b, t: (0, 0)),
        pl.BlockSpec((None, None, D, 2 * D_FF), lambda b, t: (l, i, 0, 0)),
        pl.BlockSpec((None, None, D_FF, D), lambda b, t: (l, i, 0, 0)),
    ]
    args = [x, modtab, g, w1, w2]
    if ctx_in is not None:
        assert n_ctx == TT
        in_specs.append(pl.BlockSpec((DENSE_NB, TT, D), lambda b, t: (b, 0, 0)))
        args.append(ctx_in)
    if final:
        in_specs.append(pl.BlockSpec((1, D), lambda b, t: (0, 0)))
        args.append(final_g)
        out_spec = pl.BlockSpec((DENSE_NB, TT, D), lambda b, t: (b, jnp.maximum(t - off, 0), 0))
        out_shape = jax.ShapeDtypeStruct((B, T - n_ctx, D), F32)
    else:
        out_spec = pl.BlockSpec((DENSE_NB, TT, D), lambda b, t: (b, t, 0))
        out_shape = jax.ShapeDtypeStruct((B, T, D), F32)
    return pl.pallas_call(
        functools.partial(_ffn_kernel, k0=k0, final=final, split_in=ctx_in is not None),
        grid=(B // DENSE_NB, T // TT),
        in_specs=in_specs,
        out_specs=out_spec,
        out_shape=out_shape,
        compiler_params=_cparams(("parallel", "arbitrary")),
        name="ffn_half",
    )(*args)


def _per_batch_element(tile_fn, n_in, shared):
    def kern(*refs):
        for i in range(DENSE_NB):
            tile_fn(*[r if (j < n_in and j in shared) else r.at[i] for j, r in enumerate(refs)])
    return kern


def _inproj_tile(x_ref, mod_ref, g_ref, w_ref, wt_ref, cos_ref, sin_ref,
                 mlf_ref, mlvt_ref, mlgt_ref, dfq_ref, dfk_ref, dfvx_ref, naq_ref, nak_ref, navx_ref):
    x = x_ref[...]
    h = _norm_mod(x, g_ref[...], mod_ref[3:4, :], mod_ref[4:5, :]).astype(BF16)

    def proj(c0, n):
        return jnp.dot(h, w_ref[:, c0:c0 + n], preferred_element_type=F32)

    def proj_t(r0, n):
        return lax.dot_general(wt_ref[r0:r0 + n, :], h, (((1,), (1,)), ((), ())),
                               preferred_element_type=F32)

    mlf_ref[...] = proj(0, 3 * ML_W)
    mlvt_ref[...] = proj_t(0, ML_W)
    mlgt_ref[...] = proj_t(ML_W + DF_W, GATE_W)
    ones_rows = jnp.where(lax.broadcasted_iota(jnp.int32, (ONES_ROWS, TT), 0) == 0, 1.0, 0.0).astype(BF16)
    for vx_ref, r0, heads, dh in ((dfvx_ref, ML_W, DF_HEADS, DF_DV),
                                  (navx_ref, ML_W + DF_W + GATE_W, NA_HEADS, NA_DH)):
        vt = proj_t(r0, heads * dh).astype(BF16)
        for hd in range(heads):
            r = hd * (dh + ONES_ROWS)
            vx_ref[r:r + dh, :] = vt[hd * dh:(hd + 1) * dh, :]
            vx_ref[r + dh:r + dh + ONES_ROWS, :] = ones_rows
    cos = jnp.tile(cos_ref[...], (1, DF_HEADS))
    sin = jnp.tile(sin_ref[...], (1, DF_HEADS))
    lane = lax.broadcasted_iota(jnp.int32, (TT, DF_W), 1)
    first_half = (lane % DF_DQK) < (DF_DQK // 2)

    def rope(v):
        partner = jnp.where(first_half, pltpu.roll(v, DF_W - DF_DQK // 2, 1),
                            pltpu.roll(v, DF_DQK // 2, 1))
        return v * cos + partner * sin

    c0 = 3 * ML_W
    dfq_ref[...] = (rope(proj(c0, DF_W)) * (DF_DQK ** -0.5 * LOG2E)).astype(BF16)
    dfk_ref[...] = rope(proj(c0 + DF_W, DF_W)).astype(BF16)
    c0 = c0 + 2 * DF_W
    naq_ref[...] = (proj(c0, NA_W) * (NA_DH ** -0.5 * LOG2E)).astype(BF16)
    nak_ref[...] = proj(c0 + NA_W, NA_W).astype(BF16)


def _split_w_in(w_in):
    o_mlv, o_mlo, o_mlg = 2 * ML_W, 3 * ML_W, 4 * ML_W
    o_dfq = o_mlg + GATE_W
    o_dfv = o_dfq + 2 * DF_W
    o_na = o_dfv + DF_W
    o_nav = o_na + 2 * NA_W
    o_gate = o_na + 3 * NA_W
    w_tok = jnp.concatenate([w_in[:, :, 0:o_mlv], w_in[:, :, o_mlo:o_mlg], w_in[:, :, o_dfq:o_dfv],
                             w_in[:, :, o_na:o_nav]], axis=-1)
    w_tr = jnp.concatenate([w_in[:, :, o_mlv:o_mlo], w_in[:, :, o_dfv:o_na], w_in[:, :, o_mlg:o_dfq],
                            w_in[:, :, o_nav:o_gate]], axis=-1)
    return (w_tok.astype(BF16), jnp.swapaxes(w_tr, 1, 2).astype(BF16), w_in[:, :, o_gate:].astype(BF16))


def _inproj_call(x, modtab, g, w_tok, w_tr, cos_t, sin_t, l):
    B, T, D = x.shape

    def tile_spec(n):
        return pl.BlockSpec((DENSE_NB, TT, n), lambda b, t: (b, t, 0))

    def sds(n, dt):
        return jax.ShapeDtypeStruct((B, T, n), dt)

    def tspec(n):
        return pl.BlockSpec((DENSE_NB, n, TT), lambda b, t: (b, 0, t))

    def tsds(n, dt):
        return jax.ShapeDtypeStruct((B, n, T), dt)

    return pl.pallas_call(
        _per_batch_element(_inproj_tile, n_in=7, shared=(2, 3, 4, 5, 6)),
        grid=(B // DENSE_NB, T // TT),
        in_specs=[
            tile_spec(D),
            _mod_spec(DENSE_NB),
            pl.BlockSpec((1, D), lambda b, t: (0, 0)),
            pl.BlockSpec((None, D, w_tok.shape[-1]), lambda b, t: (l, 0, 0)),
            pl.BlockSpec((None, w_tr.shape[1], D), lambda b, t: (l, 0, 0)),
            pl.BlockSpec((TT, 2 * DF_DQK), lambda b, t: (t, 0)),
            pl.BlockSpec((TT, 2 * DF_DQK), lambda b, t: (t, 0)),
        ],
        out_specs=[tile_spec(3 * ML_W), tspec(ML_W), tspec(GATE_W),
                   tile_spec(DF_W), tile_spec(DF_W), tspec(DF_HEADS * VEXT_ROWS),
                   tile_spec(NA_W), tile_spec(NA_W), tspec(NA_HEADS * NA_VROWS)],
        out_shape=[sds(3 * ML_W, F32), tsds(ML_W, F32), tsds(GATE_W, F32),
                   sds(DF_W, BF16), sds(DF_W, BF16), tsds(DF_HEADS * VEXT_ROWS, BF16),
                   sds(NA_W, BF16), sds(NA_W, BF16), tsds(NA_HEADS * NA_VROWS, BF16)],
        compiler_params=_cparams(("parallel", "parallel")),
        name="in_proj",
    )(x, modtab, g, w_tok, w_tr, cos_t, sin_t)


def _conv_kernel(x_ref, w_ref, b_ref, o_ref, *, n_ctx):
    x = x_ref[...]
    T = x.shape[0]
    row = lax.broadcasted_iota(jnp.int32, x.shape, 0)
    prev = jnp.where((row == 0) | (row == n_ctx), 0.0, pltpu.roll(x, 1, 0))
    nxt = jnp.where((row == n_ctx - 1) | (row == T - 1), 0.0, pltpu.roll(x, T - 1, 0))
    y = prev * w_ref[0:1, :] + x * w_ref[1:2, :] + nxt * w_ref[2:3, :] + b_ref[...]
    y = y * _sigmoid(y)
    scale = jnp.where(pl.program_id(1) < ML_W // x.shape[1], ML_DH ** -0.5, 1.0)
    o_ref[...] = (y * scale).astype(BF16)


def _conv_call(mlf, conv_w, conv_b, n_ctx):
    B, T, _ = mlf.shape
    ct = 256
    return pl.pallas_call(
        functools.partial(_conv_kernel, n_ctx=n_ctx),
        grid=(B, 2 * ML_W // ct),
        in_specs=[
            pl.BlockSpec((None, T, ct), lambda b, c: (b, 0, c)),
            pl.BlockSpec((3, ct), lambda b, c: (0, c)),
            pl.BlockSpec((1, ct), lambda b, c: (0, c)),
        ],
        out_specs=pl.BlockSpec((None, T, ct), lambda b, c: (b, 0, c)),
        out_shape=jax.ShapeDtypeStruct((B, T, 2 * ML_W), BF16),
        compiler_params=_cparams(("parallel", "parallel")),
        name="ml_conv",
    )(mlf, conv_w, conv_b)


def _log_sigmoid(x):
    return jnp.minimum(x, 0.0) - jnp.log1p(jnp.exp(-jnp.abs(x)))


def _segment_scan(x, seg, reverse):
    n = x.shape[1]
    pos = lax.broadcasted_iota(jnp.int32, x.shape, 1) % seg
    k = 1
    while k < seg:
        if reverse:
            x = x + jnp.where(pos < seg - k, pltpu.roll(x, n - k, 1), 0.0)
        else:
            x = x + jnp.where(pos >= k, pltpu.roll(x, k, 1), 0.0)
        k *= 2
    return x


def _gate_prep_kernel(gt_ref, gbias_ref, g_ref, b_ref, r_ref):
    gt = gt_ref[...] + gbias_ref[...]
    lf = _log_sigmoid(gt)
    row = lax.broadcasted_iota(jnp.int32, gt.shape, 0)
    b_rows = jnp.where(row < GATE_W // 2, _segment_scan(lf, TT, False), _segment_scan(lf, TT, True))
    r_rows = gt - pltpu.roll(b_rows, GATE_W - ML_HEADS, 0)
    g_ref[...] = gt
    b_ref[...] = b_rows
    r_ref[...] = jnp.concatenate([r_rows, jnp.zeros((128 - GATE_W, gt.shape[1]), F32)], axis=0).T


def _gate_prep_call(mlgt, gate_b):
    B, _, T = mlgt.shape
    row_spec = pl.BlockSpec((None, GATE_W, T), lambda b: (b, 0, 0))
    row_sds = jax.ShapeDtypeStruct((B, GATE_W, T), F32)
    return pl.pallas_call(
        _gate_prep_kernel,
        grid=(B,),
        in_specs=[row_spec, pl.BlockSpec((GATE_W, 1), lambda b: (0, 0))],
        out_specs=[row_spec, row_spec, pl.BlockSpec((None, T, 128), lambda b: (b, 0, 0))],
        out_shape=[row_sds, row_sds, jax.ShapeDtypeStruct((B, T, 128), F32)],
        compiler_params=_cparams(("parallel",)),
        name="ml_gates",
    )(mlgt, gate_b.reshape(GATE_W, 1))


def _mlstm_kernel(qkf_ref, qkb_ref, vtf_ref, vtb_ref, gf_ref, gb_ref, bf_ref, bb_ref, rf_ref, rb_ref,
                  hf_ref, hb_ref, c_scr, n_scr, m_scr):
    L = TT

    @pl.when(pl.program_id(1) == 0)
    def _():
        c_scr[...] = jnp.zeros(c_scr.shape, F32)
        n_scr[...] = jnp.zeros(n_scr.shape, F32)
        m_scr[...] = jnp.zeros(m_scr.shape, F32)

    key = lax.broadcasted_iota(jnp.int32, (L, L), 0)
    qry = lax.broadcasted_iota(jnp.int32, (L, L), 1)
    row8 = lax.broadcasted_iota(jnp.int32, (8, L), 0)
    nt_dims = (((1,), (1,)), ((), ()))

    chains = []
    dirs = ((qkf_ref, vtf_ref, gf_ref, bf_ref, rf_ref, hf_ref),
            (qkb_ref, vtb_ref, gb_ref, bb_ref, rb_ref, hb_ref))
    for d, (qk_ref, vt_ref, g_ref, b_ref, r_ref, h_ref) in enumerate(dirs):
        gt = g_ref[...]
        b_rows = b_ref[...]
        r_cols = r_ref[...]
        mask = (key <= qry) if d == 0 else (key >= qry)
        for hd in range(ML_HEADS):
            ci = d * 2 * ML_HEADS + hd
            b_row = b_rows[ci + ML_HEADS:ci + ML_HEADS + 1, :]
            chains.append(dict(
                si=d * ML_HEADS + hd, hd=hd, h_ref=h_ref, mask=mask,
                q=qk_ref[:, hd * ML_DH:(hd + 1) * ML_DH],
                k=qk_ref[:, ML_W + hd * ML_DH:ML_W + (hd + 1) * ML_DH],
                vt=vt_ref[hd * ML_DH:(hd + 1) * ML_DH, :],
                li_row=gt[ci:ci + 1, :], b_row=b_row, r_col=r_cols[:, ci:ci + 1],
                btot=b_row[:, L - 1:L] if d == 0 else b_row[:, 0:1]))

    for ch in chains:
        m = m_scr[ch["si"], 0:1, 0:1]
        dmat = jnp.where(ch["mask"], ch["r_col"] + ch["b_row"], -jnp.inf)
        inter = ch["b_row"] + m
        m_t = jnp.maximum(inter, jnp.max(dmat, axis=0, keepdims=True))
        ch.update(m=m, m_t=m_t, w=jnp.exp(dmat - m_t), s_inter=jnp.exp(inter - m_t),
                  s=lax.dot_general(ch["k"], ch["q"], nt_dims, preferred_element_type=F32))

    for ch in chains:
        si, hd, q = ch["si"], ch["hd"], ch["q"]
        cmat = c_scr[si]
        n8 = n_scr[si]
        qkw = ch["s"] * ch["w"]
        num = (jnp.dot(ch["vt"].astype(BF16), qkw.astype(BF16), preferred_element_type=F32)
               + ch["s_inter"] * lax.dot_general(cmat.astype(BF16), q, nt_dims,
                                                 preferred_element_type=F32))
        nq = lax.dot_general(n8.astype(BF16), q, nt_dims, preferred_element_type=F32)[0:1, :]
        den = jnp.sum(qkw, axis=0, keepdims=True) + ch["s_inter"] * nq
        ht = num / jnp.maximum(jnp.abs(den), jnp.exp(-ch["m_t"]))
        ch["h_ref"][:, hd * ML_DH:(hd + 1) * ML_DH] = ht.T

    for ch in chains:
        si, m, btot = ch["si"], ch["m"], ch["btot"]
        wk = btot - ch["b_row"] + ch["li_row"]
        m_new = jnp.maximum(btot + m, jnp.max(wk, axis=1, keepdims=True))
        decay = jnp.exp(btot + m - m_new)
        wk = jnp.exp(wk - m_new)
        c_scr[si] = decay * c_scr[si] + jnp.dot((ch["vt"] * wk).astype(BF16), ch["k"],
                                                preferred_element_type=F32)
        wk8 = jnp.where(row8 == 0, wk, 0.0).astype(BF16)
        n_scr[si] = decay * n_scr[si] + jnp.dot(wk8, ch["k"], preferred_element_type=F32)
        m_scr[si] = jnp.broadcast_to(m_new, m_scr.shape[1:])


def _mlstm_call(qk, mlvt, gates, b_rows, r_cols):
    B, T, _ = qk.shape
    nt = T // TT

    def bwd_t(t):
        return jnp.where(t == 0, 0, nt - t)

    def spec(n, back):
        if back:
            return pl.BlockSpec((None, TT, n), lambda b, t: (b, bwd_t(t), 0))
        return pl.BlockSpec((None, TT, n), lambda b, t: (b, t, 0))

    def tspec(n, back):
        if back:
            return pl.BlockSpec((None, n, TT), lambda b, t: (b, 0, bwd_t(t)))
        return pl.BlockSpec((None, n, TT), lambda b, t: (b, 0, t))

    nch = 2 * ML_HEADS
    out_sds = jax.ShapeDtypeStruct((B, T, ML_W), F32)
    return pl.pallas_call(
        _mlstm_kernel,
        grid=(B, nt),
        in_specs=[
            spec(2 * ML_W, False), spec(2 * ML_W, True),
            tspec(ML_W, False), tspec(ML_W, True),
            tspec(GATE_W, False), tspec(GATE_W, True),
            tspec(GATE_W, False), tspec(GATE_W, True),
            spec(128, False), spec(128, True),
        ],
        out_specs=[spec(ML_W, False), spec(ML_W, True)],
        out_shape=[out_sds, out_sds],
        scratch_shapes=[
            pltpu.VMEM((nch, ML_DH, ML_DH), F32),
            pltpu.VMEM((nch, 8, ML_DH), F32),
            pltpu.VMEM((nch, 8, 128), F32),
        ],
        compiler_params=_cparams(("parallel", "arbitrary")),
        name="mlstm_scan",
    )(qk, qk, mlvt, mlvt, gates, gates, b_rows, b_rows, r_cols, r_cols)


def _diff_kernel(q_ref, k_ref, vx_ref, lam_ref, g_ref, o_ref, sa_scr, sb_scr, ma_scr, mb_scr,
                 acca_scr, accb_scr, *, lam_init, n_ctx):
    t = pl.program_id(2)
    nt = pl.num_programs(2) - 2
    nk = k_ref.shape[0]
    dn = (((1,), (1,)), ((), ()))
    all_chunks = [(c, min(c + DF_KCHUNK, nk)) for c in range(0, nk, DF_KCHUNK)]
    ctx_chunks = [ch for ch in all_chunks if ch[0] < n_ctx]
    slots = ((sa_scr, ma_scr), (sb_scr, mb_scr))
    accs = (acca_scr, accb_scr)

    def step(out_acc, finish_chunks, rd, acc_wr, score_chunks, wr):
        if out_acc is not None:
            lv = lam_ref[...]
            lam = (jnp.exp(jnp.sum(lv[0:1, :] * lv[1:2, :], axis=1, keepdims=True))
                   - jnp.exp(jnp.sum(lv[2:3, :] * lv[3:4, :], axis=1, keepdims=True)) + lam_init)
            outs = [out_acc[j, 0:DF_DV, :] / out_acc[j, DF_DV:DF_DV + 1, :] for j in range(2)]
            o = outs[0] - lam * outs[1]
            var = jnp.mean(o * o, axis=0, keepdims=True)
            on = o * lax.rsqrt(var + EPS) * g_ref[...] * (1.0 - lam_init)
            o_ref[...] = on.T.astype(o_ref.dtype)
        acc = [None, None]
        m_new = [None, None]
        if score_chunks:
            q = q_ref[...]
            lane = lax.broadcasted_iota(jnp.int32, q.shape, 1)
            zero = jnp.zeros(q.shape, q.dtype)
            qms = (jnp.where(lane < DF_DQK, q, zero), jnp.where(lane < DF_DQK, zero, q))
        if finish_chunks:
            m_old = [rd[1][j, 0:1, :] for j in range(2)]
        for i in range(max(len(finish_chunks), len(score_chunks))):
            for j in range(2):
                if i < len(finish_chunks):
                    c0, c1 = finish_chunks[i]
                    e = jnp.exp2(rd[0][j, c0:c1, :] - m_old[j]).astype(BF16)
                    part = jnp.dot(vx_ref[:, c0:c1], e, preferred_element_type=F32)
                    acc[j] = part if acc[j] is None else acc[j] + part
                if i < len(score_chunks):
                    c0, c1 = score_chunks[i]
                    s = lax.dot_general(k_ref[c0:c1, :], qms[j], dn, preferred_element_type=F32)
                    wr[0][j, c0:c1, :] = s
                    cm = jnp.max(s, axis=0, keepdims=True)
                    m_new[j] = cm if m_new[j] is None else jnp.maximum(m_new[j], cm)
        if score_chunks:
            for j in range(2):
                wr[1][j] = jnp.broadcast_to(m_new[j], wr[1].shape[1:])
        if finish_chunks:
            for j in range(2):
                acc_wr[j] = acc[j]

    @pl.when(t == 0)
    def _():
        step(None, [], None, None, ctx_chunks, slots[0])

    @pl.when(t == 1)
    def _():
        step(None, ctx_chunks, slots[0], accs[0], all_chunks, slots[1])

    for p in range(2):
        @pl.when((t % 2 == p) & (t >= 2) & (t < nt))
        def _():
            step(accs[p], all_chunks, slots[1 - p], accs[1 - p], all_chunks, slots[p])

        @pl.when((t % 2 == p) & (t == nt))
        def _():
            step(accs[p], all_chunks, slots[1 - p], accs[1 - p], [], None)

        @pl.when((t % 2 == p) & (t == nt + 1))
        def _():
            step(accs[p], [], None, None, [], None)


def _diff_call(dfq, dfk, dfvx, lam_vecs, norm_g, lam_init, n_ctx):
    B, T, _ = dfq.shape
    nt = T // TT
    assert n_ctx % DF_KCHUNK == 0 and nt >= 2
    return pl.pallas_call(
        functools.partial(_diff_kernel, lam_init=lam_init, n_ctx=n_ctx),
        grid=(B, DF_HEADS, nt + 2),
        in_specs=[
            pl.BlockSpec((None, TT, DF_DV), lambda b, h, t: (b, jnp.minimum(t, nt - 1), h)),
            pl.BlockSpec((None, T, DF_DV), lambda b, h, t: (b, 0, h)),
            pl.BlockSpec((None, VEXT_ROWS, T), lambda b, h, t: (b, h, 0)),
            pl.BlockSpec((4, DF_DQK), lambda b, h, t: (0, 0)),
            pl.BlockSpec((DF_DV, 1), lambda b, h, t: (h, 0)),
        ],
        out_specs=pl.BlockSpec((None, TT, DF_DV), lambda b, h, t: (b, jnp.maximum(t - 2, 0), h)),
        out_shape=jax.ShapeDtypeStruct((B, T, DF_W), BF16),
        scratch_shapes=[pltpu.VMEM((2, T, TT), F32), pltpu.VMEM((2, T, TT), F32),
                        pltpu.VMEM((2, 8, TT), F32), pltpu.VMEM((2, 8, TT), F32),
                        pltpu.VMEM((2, VEXT_ROWS, TT), F32), pltpu.VMEM((2, VEXT_ROWS, TT), F32)],
        compiler_params=_cparams(("parallel", "parallel", "arbitrary")),
        name="diff_attn",
    )(dfq, dfk, dfvx, lam_vecs, norm_g.reshape(DF_W, 1))


def _na_tables(rows):
    kh = min(NA_KH, rows)
    assert kh == NA_KH and rows >= NA_KROWS and rows % NA_QROWS == 0
    n_tiles = rows // NA_QROWS
    qc = np.arange(GRID_W)
    qstart = np.clip(qc - NA_KW // 2, 0, GRID_W - NA_KW)
    col_ok = (qc[None, :] >= qstart[:, None]) & (qc[None, :] < qstart[:, None] + NA_KW)
    drs, valids = [], []
    for tile in (0, 1, n_tiles - 1):
        r0 = tile * NA_QROWS
        start = int(np.clip(r0 - kh // 2, 0, rows - NA_KROWS))
        r = r0 + np.arange(NA_QROWS)
        rs = np.clip(r - kh // 2, 0, rows - kh)
        akr = start + np.arange(NA_KROWS)
        row_ok = (akr[None, :] >= rs[:, None]) & (akr[None, :] < rs[:, None] + kh)
        drs.append(np.clip(akr[None, :] - r[:, None] + NA_KH - 1, 0, 2 * NA_KH - 2))
        ok = row_ok[:, None, :, None] & col_ok[None, :, None, :]
        valids.append(ok.reshape(TT, NA_KROWS * GRID_W))
    return np.stack(drs), np.stack(valids)


def _na_bias_tables(rel_bias, rows):
    dr, valid = _na_tables(rows)
    L, H = rel_bias.shape[:2]
    pad = GRID_W - NA_KW
    flipped = jnp.flip(jnp.pad(rel_bias.astype(F32) * LOG2E, ((0, 0), (0, 0), (0, 0), (pad, pad)),
                               mode="edge"), axis=-1)
    off = flipped.shape[-1] - 1 - (NA_KW - 1 + pad)
    by_col = jnp.stack([flipped[..., off - kc:off - kc + GRID_W] for kc in range(GRID_W)], axis=-2)
    tabs = []
    for ty in range(3):
        per_kr = [jnp.stack([by_col[:, :, int(dr[ty][qr, kr])] for qr in range(NA_QROWS)], axis=3)
                  for kr in range(NA_KROWS)]
        tab = jnp.stack(per_kr, axis=2).reshape(L, H, NA_KROWS * GRID_W, TT)
        tabs.append(jnp.where(valid[ty].T, tab, NEG))
    return jnp.stack(tabs, axis=1)


def _na_kernel(q_ref, k_ref, vx_ref, bias_ref, o_ref, *, n_ctx, rows):
    t = pl.program_id(1)
    dn = (((1,), (1,)), ((), ()))
    tile_type = jnp.where(t <= 1, 0, jnp.where(t == pl.num_programs(1) - 1, 2, 1))

    def run(key_slices):
        def scores(hd):
            sl = slice(hd * NA_DH, (hd + 1) * NA_DH)
            q = q_ref[:, sl]
            parts = []
            for ks, add_bias in key_slices:
                s = lax.dot_general(k_ref[ks, sl], q, dn, preferred_element_type=F32)
                parts.append(s + bias_ref[tile_type, hd] if add_bias else s)
            m = jnp.max(parts[0], axis=0, keepdims=True)
            for p in parts[1:]:
                m = jnp.maximum(m, jnp.max(p, axis=0, keepdims=True))
            return parts, m

        def finish(hd, parts, m):
            o = None
            for (ks, _), s in zip(key_slices, parts):
                e = jnp.exp2(s - m).astype(BF16)
                part = jnp.dot(vx_ref[hd * NA_VROWS:(hd + 1) * NA_VROWS, ks], e,
                               preferred_element_type=F32)
                o = part if o is None else o + part
            return o[0:NA_DH, :] / o[NA_DH:NA_DH + 1, :]

        prev = None
        done = []
        for hd in range(NA_HEADS + 1):
            cur = scores(hd) if hd < NA_HEADS else None
            if prev is not None:
                done.append(finish(hd - 1, *prev))
                if len(done) == 2:
                    c0 = (hd - 2) * NA_DH
                    o_ref[:, c0:c0 + 2 * NA_DH] = jnp.concatenate(done, axis=0).T.astype(o_ref.dtype)
                    done = []
            prev = cur

    @pl.when(t == 0)
    def _():
        run([(slice(0, n_ctx), False)])

    @pl.when(t != 0)
    def _():
        r0 = (t - 1) * NA_QROWS
        start = jnp.clip(r0 - NA_KH // 2, 0, rows - NA_KROWS)
        tok0 = pl.multiple_of(n_ctx + start * GRID_W, TT)
        run([(pl.ds(tok0, NA_KROWS * GRID_W), True), (slice(0, n_ctx), False)])


def _na_call(naq, nak, navx, bias_tab, n_ctx, l):
    B, T, _ = naq.shape
    nt = T // TT
    rows = (T - n_ctx) // GRID_W
    nk = NA_KROWS * GRID_W
    assert n_ctx % TT == 0 and (NA_KH // 2) % NA_QROWS == 0 and (rows - NA_KROWS) % NA_QROWS == 0

    return pl.pallas_call(
        functools.partial(_na_kernel, n_ctx=n_ctx, rows=rows),
        grid=(B, nt),
        in_specs=[
            pl.BlockSpec((None, TT, NA_W), lambda b, t: (b, t, 0)),
            pl.BlockSpec((None, T, NA_W), lambda b, t: (b, 0, 0)),
            pl.BlockSpec((None, NA_HEADS * NA_VROWS, T), lambda b, t: (b, 0, 0)),
            pl.BlockSpec((None, 3, NA_HEADS, nk, TT), lambda b, t: (l, 0, 0, 0, 0),
                         pipeline_mode=pl.Buffered(1)),
        ],
        out_specs=pl.BlockSpec((None, TT, NA_W), lambda b, t: (b, t, 0)),
        out_shape=jax.ShapeDtypeStruct((B, T, NA_W), BF16),
        compiler_params=_cparams(("parallel", "arbitrary")),
        name="na_attn",
    )(naq, nak, navx, bias_tab)


def _merge_tile(x_ref, mod_ref, g_ref, wg_ref, wb_ref, wo_ref, hf_ref, hb_ref, o_ref_in,
                  mlg_ref, bd_ref, cn_ref, out_ref):
    x = x_ref[...]
    D = x.shape[1]
    hn = _norm_mod(x, g_ref[...], mod_ref[3:4, :], mod_ref[4:5, :]).astype(BF16)
    hsum = hf_ref[...] + hb_ref[...]
    parts = []
    for hd in range(ML_HEADS):
        hh = hsum[:, hd * ML_DH:(hd + 1) * ML_DH]
        parts.append(hh * lax.rsqrt(jnp.mean(hh * hh, axis=1, keepdims=True) + EPS))
    a = (_sigmoid(o_ref_in[...]) * (jnp.concatenate(parts, axis=1) * mlg_ref[...])).astype(BF16)
    y = None
    for i, br in enumerate((a, bd_ref[...], cn_ref[...])):
        gate = _sigmoid(jnp.dot(hn, wg_ref[:, i * D:(i + 1) * D], preferred_element_type=F32))
        term = gate * jnp.dot(br, wb_ref[i], preferred_element_type=F32)
        y = term if y is None else y + term
    z = jnp.dot(y.astype(BF16), wo_ref[...], preferred_element_type=F32)
    out_ref[...] = x + mod_ref[5:6, :] * z


def _merge_call(x, modtab, g, w_gate, w_branch, w_out, hf, hb, mlf, ml_norm_g, bd, cn, l):
    B, T, D = x.shape

    def tile_spec(n, cblk=0):
        return pl.BlockSpec((DENSE_NB, TT, n), lambda b, t: (b, t, cblk))

    return pl.pallas_call(
        _per_batch_element(_merge_tile, n_in=12, shared=(2, 3, 4, 5, 9)),
        grid=(B // DENSE_NB, T // TT),
        in_specs=[
            tile_spec(D),
            _mod_spec(DENSE_NB),
            pl.BlockSpec((1, D), lambda b, t: (0, 0)),
            pl.BlockSpec((None, D, N_BRANCH * D), lambda b, t: (l, 0, 0)),
            pl.BlockSpec((None, N_BRANCH, BRANCH_W, D), lambda b, t: (l, 0, 0, 0)),
            pl.BlockSpec((None, D, D), lambda b, t: (l, 0, 0)),
            tile_spec(ML_W), tile_spec(ML_W),
            tile_spec(ML_W, 2),
            pl.BlockSpec((1, ML_W), lambda b, t: (0, 0)),
            tile_spec(DF_W), tile_spec(NA_W),
        ],
        out_specs=tile_spec(D),
        out_shape=jax.ShapeDtypeStruct(x.shape, F32),
        compiler_params=_cparams(("parallel", "parallel")),
        name="merge",
    )(x, modtab, g, w_gate, w_branch, w_out, hf, hb, mlf, ml_norm_g, bd, cn)


def _rope_tables(S, n_ctx):
    t = jnp.arange(S)
    rowp = (t // GRID_W).astype(F32)
    colp = (t % GRID_W).astype(F32)
    half = DF_DQK // 2
    freqs = ROPE_BASE ** (-jnp.arange(0, half, 2, dtype=F32) / half)
    ang = jnp.concatenate([rowp[:, None] * freqs, colp[:, None] * freqs], axis=-1)
    cos, sin = jnp.cos(ang), jnp.sin(ang)
    cos = jnp.concatenate([jnp.ones((n_ctx, half), F32), cos], axis=0)
    sin = jnp.concatenate([jnp.zeros((n_ctx, half), F32), sin], axis=0)
    cos_t = jnp.tile(cos, (1, 4))
    sin_t = jnp.tile(jnp.concatenate([-sin, sin], axis=1), (1, 2))
    return cos_t, sin_t


def kernel(x, c, ctx, c_ctx, w_ada, b_ada, norm_g, ffn_w1, ffn_w2, w_in, ml_conv_w, ml_conv_b,
           ml_gate_b, ml_norm_g, df_lambda, df_norm_g, na_rel_bias, w_branch, w_out, final_g):
    B, S, D = x.shape
    n_ctx = ctx.shape[1]
    depth = w_ada.shape[0]
    assert D == D_MODEL and n_ctx == TT and S % TT == 0 and S % GRID_W == 0
    T = n_ctx + S
    rows = S // GRID_W

    w1 = ffn_w1.astype(BF16)
    w2 = ffn_w2.astype(BF16)
    w_tok, w_tr, w_gate = _split_w_in(w_in)
    wb = w_branch.astype(BF16)
    wo = w_out.astype(BF16)

    cvec = jnp.zeros((16, D), F32).at[0].set(c_ctx).at[1:B + 1].set(c)
    mod = _ada_call(cvec, w_ada, b_ada).reshape(depth, 16, 9, D)
    modtab = jnp.stack([jnp.broadcast_to(mod[:, 0:1], (depth, B, 9, D)), mod[:, 1:B + 1]], axis=2)

    cos_t, sin_t = _rope_tables(S, n_ctx)
    bias_tab = _na_bias_tables(na_rel_bias, rows)

    h = x
    for l in range(depth):
        lam_init = 0.8 - 0.6 * math.exp(-0.3 * l)
        mt = modtab[l]
        h = _ffn_call(h, mt, norm_g[l, 0][None], w1, w2, l, 0, 0,
                      ctx_in=ctx if l == 0 else None, n_ctx=n_ctx)
        mlf, mlvt, mlgt, dfq, dfk, dfvx, naq, nak, navx = _inproj_call(
            h, mt, norm_g[l, 1][None], w_tok, w_tr, cos_t, sin_t, l)
        qk = _conv_call(mlf, ml_conv_w[l], ml_conv_b[l][None], n_ctx)
        hf, hb = _mlstm_call(qk, mlvt, *_gate_prep_call(mlgt, ml_gate_b[l]))
        bd = _diff_call(dfq, dfk, dfvx, df_lambda[l], df_norm_g[l], lam_init, n_ctx)
        cn = _na_call(naq, nak, navx, bias_tab, n_ctx, l)
        h = _merge_call(h, mt, norm_g[l, 1][None], w_gate, wb, wo, hf, hb, mlf,
                        ml_norm_g[l][None], bd, cn, l)
        last = l == depth - 1
        h = _ffn_call(h, mt, norm_g[l, 2][None], w1, w2, l, 1, 6,
                      final_g=final_g[None] if last else None, n_ctx=n_ctx)
    return h
```

```python
import functools
import math

import numpy as np
import jax
import jax.numpy as jnp
from jax import lax
from jax.experimental import pallas as pl
from jax.experimental.pallas import tpu as pltpu

F32 = jnp.float32
BF16 = jnp.bfloat16

D_MODEL = 1024
GRID_W = 64
N_BRANCH = 3
BRANCH_W = 512
ML_HEADS = 4
ML_DH = 128
ML_W = ML_HEADS * ML_DH
DF_HEADS = 4
DF_DQK = 64
DF_DV = 128
DF_W = DF_HEADS * DF_DV
ROPE_BASE = 10000.0
NA_HEADS = 8
NA_DH = 64
NA_W = NA_HEADS * NA_DH
NA_KH = 8
NA_KW = 16
D_FF = 2816
EPS = 1e-6
GATE_W = 4 * ML_HEADS

DENSE_NB = 2
ML_NB = 2
TT = 256
NA_QROWS = 4
NA_KROWS = 12
MXU_N = 256
FF_CHUNKS = ((0, 6 * MXU_N), (6 * MXU_N, D_FF))
DF_KCHUNK = MXU_N
ONES_ROWS = 16
VEXT_ROWS = DF_DV + ONES_ROWS
NA_VROWS = NA_DH + ONES_ROWS
NEG = -1e30
LOG2E = math.log2(math.e)
VMEM_LIMIT = 56 * 1024 * 1024


def _cparams(sem):
    return pltpu.CompilerParams(dimension_semantics=sem, vmem_limit_bytes=VMEM_LIMIT)


def _norm_mod(x, g, shift, scale):
    var = jnp.mean(x * x, axis=-1, keepdims=True)
    return x * lax.rsqrt(var + EPS) * g * (1.0 + scale) + shift


def _sigmoid(x):
    return 1.0 / (1.0 + jnp.exp(-x))


def _ada_kernel(c_ref, w_ref, b_ref, o_ref):
    c = c_ref[...]
    h = (c * _sigmoid(c)).astype(BF16)
    o_ref[...] = jnp.dot(h, w_ref[...].astype(BF16), preferred_element_type=F32) + b_ref[...]


def _ada_call(cvec, w_ada, b_ada):
    L, D, N = w_ada.shape
    tn = N // 8
    return pl.pallas_call(
        _ada_kernel,
        grid=(L, N // tn),
        in_specs=[
            pl.BlockSpec(cvec.shape, lambda l, j: (0, 0)),
            pl.BlockSpec((None, D, tn), lambda l, j: (l, 0, j)),
            pl.BlockSpec((None, 1, tn), lambda l, j: (l, 0, j)),
        ],
        out_specs=pl.BlockSpec((None, cvec.shape[0], tn), lambda l, j: (l, 0, j)),
        out_shape=jax.ShapeDtypeStruct((L, cvec.shape[0], N), F32),
        compiler_params=_cparams(("parallel", "parallel")),
        name="adaln",
    )(cvec, w_ada, b_ada.reshape(L, 1, N))


def _ffn_kernel(x_ref, mod_ref, g_ref, w1_ref, w2_ref, *rest, k0, final, split_in):
    o_ref = rest[-1]
    for i in range(DENSE_NB):
        x = x_ref[i]
        if split_in:
            x = jnp.where(pl.program_id(1) == 0, rest[0][i], x)
        h = _norm_mod(x, g_ref[...], mod_ref[i, k0:k0 + 1, :], mod_ref[i, k0 + 1:k0 + 2, :]).astype(BF16)
        acc = jnp.zeros(x.shape, F32)
        for c0, c1 in FF_CHUNKS:
            a = jnp.dot(h, w1_ref[:, c0:c1], preferred_element_type=F32)
            b = jnp.dot(h, w1_ref[:, D_FF + c0:D_FF + c1], preferred_element_type=F32)
            u = (a * _sigmoid(a) * b).astype(BF16)
            acc = acc + jnp.dot(u, w2_ref[c0:c1, :], preferred_element_type=F32)
        y = x + 0.5 * mod_ref[i, k0 + 2:k0 + 3, :] * acc
        if final:
            y = y * lax.rsqrt(jnp.mean(y * y, axis=-1, keepdims=True) + EPS) * rest[-2][...]
        o_ref[i] = y


def _mod_spec(nb=None):
    return pl.BlockSpec((nb, None, 9, D_MODEL), lambda b, t: (b, jnp.minimum(t, 1), 0, 0))


def _ffn_call(x, modtab, g, w1, w2, l, i, k0, final_g=None, ctx_in=None, n_ctx=0):
    B, _, D = x.shape
    T = x.shape[1] + (n_ctx if ctx_in is not None else 0)
    final = final_g is not None
    off = n_ctx // TT
    if ctx_in is not None:
        x_spec = pl.BlockSpec((DENSE_NB, TT, D), lambda b, t: (b, jnp.maximum(t - off, 0), 0))
    else:
        x_spec = pl.BlockSpec((DENSE_NB, TT, D), lambda b, t: (b, t, 0))
    assert B % DENSE_NB == 0
    in_specs = [
        x_spec,
        _mod_spec(DENSE_NB),
        pl.BlockSpec((1, D), lambda b, t: (0, 0)),
        pl.BlockSpec((None, None, D, 2 * D_FF), lambda b, t: (l, i, 0, 0)),
        pl.BlockSpec((None, None, D_FF, D), lambda b, t: (l, i, 0, 0)),
    ]
    args = [x, modtab, g, w1, w2]
    if ctx_in is not None:
        assert n_ctx == TT
        in_specs.append(pl.BlockSpec((DENSE_NB, TT, D), lambda b, t: (b, 0, 0)))
        args.append(ctx_in)
    if final:
        in_specs.append(pl.BlockSpec((1, D), lambda b, t: (0, 0)))
        args.append(final_g)
        out_spec = pl.BlockSpec((DENSE_NB, TT, D), lambda b, t: (b, jnp.maximum(t - off, 0), 0))
        out_shape = jax.ShapeDtypeStruct((B, T - n_ctx, D), F32)
    else:
        out_spec = pl.BlockSpec((DENSE_NB, TT, D), lambda b, t: (b, t, 0))
        out_shape = jax.ShapeDtypeStruct((B, T, D), F32)
    return pl.pallas_call(
        functools.partial(_ffn_kernel, k0=k0, final=final, split_in=ctx_in is not None),
        grid=(B // DENSE_NB, T // TT),
        in_specs=in_specs,
        out_specs=out_spec,
        out_shape=out_shape,
        compiler_params=_cparams(("parallel", "arbitrary")),
        name="ffn_half",
    )(*args)


def _per_batch_element(tile_fn, n_in, shared):
    def kern(*refs):
        for i in range(DENSE_NB):
            tile_fn(*[r if (j < n_in and j in shared) else r.at[i] for j, r in enumerate(refs)])
    return kern


def _inproj_tile(x_ref, mod_ref, g_ref, w_ref, wt_ref, cos_ref, sin_ref,
                 mlf_ref, mlvt_ref, mlgt_ref, dfq_ref, dfk_ref, dfvx_ref, naq_ref, nak_ref, navx_ref):
    x = x_ref[...]
    h = _norm_mod(x, g_ref[...], mod_ref[3:4, :], mod_ref[4:5, :]).astype(BF16)

    def proj(c0, n):
        return jnp.dot(h, w_ref[:, c0:c0 + n], preferred_element_type=F32)

    def proj_t(r0, n):
        return lax.dot_general(wt_ref[r0:r0 + n, :], h, (((1,), (1,)), ((), ())),
                               preferred_element_type=F32)

    mlf_ref[...] = proj(0, 3 * ML_W)
    mlvt_ref[...] = proj_t(0, ML_W)
    mlgt_ref[...] = proj_t(ML_W + DF_W, GATE_W)
    ones_rows = jnp.where(lax.broadcasted_iota(jnp.int32, (ONES_ROWS, TT), 0) == 0, 1.0, 0.0).astype(BF16)
    for vx_ref, r0, heads, dh in ((dfvx_ref, ML_W, DF_HEADS, DF_DV),
                                  (navx_ref, ML_W + DF_W + GATE_W, NA_HEADS, NA_DH)):
        vt = proj_t(r0, heads * dh).astype(BF16)
        for hd in range(heads):
            r = hd * (dh + ONES_ROWS)
            vx_ref[r:r + dh, :] = vt[hd * dh:(hd + 1) * dh, :]
            vx_ref[r + dh:r + dh + ONES_ROWS, :] = ones_rows
    cos = jnp.tile(cos_ref[...], (1, DF_HEADS))
    sin = jnp.tile(sin_ref[...], (1, DF_HEADS))
    lane = lax.broadcasted_iota(jnp.int32, (TT, DF_W), 1)
    first_half = (lane % DF_DQK) < (DF_DQK // 2)

    def rope(v):
        partner = jnp.where(first_half, pltpu.roll(v, DF_W - DF_DQK // 2, 1),
                            pltpu.roll(v, DF_DQK // 2, 1))
        return v * cos + partner * sin

    c0 = 3 * ML_W
    dfq_ref[...] = (rope(proj(c0, DF_W)) * (DF_DQK ** -0.5 * LOG2E)).astype(BF16)
    dfk_ref[...] = rope(proj(c0 + DF_W, DF_W)).astype(BF16)
    c0 = c0 + 2 * DF_W
    naq_ref[...] = (proj(c0, NA_W) * (NA_DH ** -0.5 * LOG2E)).astype(BF16)
    nak_ref[...] = proj(c0 + NA_W, NA_W).astype(BF16)


def _split_w_in(w_in):
    o_mlv, o_mlo, o_mlg = 2 * ML_W, 3 * ML_W, 4 * ML_W
    o_dfq = o_mlg + GATE_W
    o_dfv = o_dfq + 2 * DF_W
    o_na = o_dfv + DF_W
    o_nav = o_na + 2 * NA_W
    o_gate = o_na + 3 * NA_W
    w_tok = jnp.concatenate([w_in[:, :, 0:o_mlv], w_in[:, :, o_mlo:o_mlg], w_in[:, :, o_dfq:o_dfv],
                             w_in[:, :, o_na:o_nav]], axis=-1)
    w_tr = jnp.concatenate([w_in[:, :, o_mlv:o_mlo], w_in[:, :, o_dfv:o_na], w_in[:, :, o_mlg:o_dfq],
                            w_in[:, :, o_nav:o_gate]], axis=-1)
    return (w_tok.astype(BF16), jnp.swapaxes(w_tr, 1, 2).astype(BF16), w_in[:, :, o_gate:].astype(BF16))


def _inproj_call(x, modtab, g, w_tok, w_tr, cos_t, sin_t, l):
    B, T, D = x.shape

    def tile_spec(n):
        return pl.BlockSpec((DENSE_NB, TT, n), lambda b, t: (b, t, 0))

    def sds(n, dt):
        return jax.ShapeDtypeStruct((B, T, n), dt)

    def tspec(n):
        return pl.BlockSpec((DENSE_NB, n, TT), lambda b, t: (b, 0, t))

    def tsds(n, dt):
        return jax.ShapeDtypeStruct((B, n, T), dt)

    return pl.pallas_call(
        _per_batch_element(_inproj_tile, n_in=7, shared=(2, 3, 4, 5, 6)),
        grid=(B // DENSE_NB, T // TT),
        in_specs=[
            tile_spec(D),
            _mod_spec(DENSE_NB),
            pl.BlockSpec((1, D), lambda b, t: (0, 0)),
            pl.BlockSpec((None, D, w_tok.shape[-1]), lambda b, t: (l, 0, 0)),
            pl.BlockSpec((None, w_tr.shape[1], D), lambda b, t: (l, 0, 0)),
            pl.BlockSpec((TT, 2 * DF_DQK), lambda b, t: (t, 0)),
            pl.BlockSpec((TT, 2 * DF_DQK), lambda b, t: (t, 0)),
        ],
        out_specs=[tile_spec(3 * ML_W), tspec(ML_W), tspec(GATE_W),
                   tile_spec(DF_W), tile_spec(DF_W), tspec(DF_HEADS * VEXT_ROWS),
                   tile_spec(NA_W), tile_spec(NA_W), tspec(NA_HEADS * NA_VROWS)],
        out_shape=[sds(3 * ML_W, F32), tsds(ML_W, F32), tsds(GATE_W, F32),
                   sds(DF_W, BF16), sds(DF_W, BF16), tsds(DF_HEADS * VEXT_ROWS, BF16),
                   sds(NA_W, BF16), sds(NA_W, BF16), tsds(NA_HEADS * NA_VROWS, BF16)],
        compiler_params=_cparams(("parallel", "parallel")),
        name="in_proj",
    )(x, modtab, g, w_tok, w_tr, cos_t, sin_t)


def _conv_kernel(x_ref, w_ref, b_ref, o_ref, *, n_ctx):
    x = x_ref[...]
    T = x.shape[0]
    row = lax.broadcasted_iota(jnp.int32, x.shape, 0)
    prev = jnp.where((row == 0) | (row == n_ctx), 0.0, pltpu.roll(x, 1, 0))
    nxt = jnp.where((row == n_ctx - 1) | (row == T - 1), 0.0, pltpu.roll(x, T - 1, 0))
    y = prev * w_ref[0:1, :] + x * w_ref[1:2, :] + nxt * w_ref[2:3, :] + b_ref[...]
    y = y * _sigmoid(y)
    scale = jnp.where(pl.program_id(1) < ML_W // x.shape[1], ML_DH ** -0.5, 1.0)
    o_ref[...] = (y * scale).astype(BF16)


def _conv_call(mlf, conv_w, conv_b, n_ctx):
    B, T, _ = mlf.shape
    ct = 256
    return pl.pallas_call(
        functools.partial(_conv_kernel, n_ctx=n_ctx),
        grid=(B, 2 * ML_W // ct),
        in_specs=[
            pl.BlockSpec((None, T, ct), lambda b, c: (b, 0, c)),
            pl.BlockSpec((3, ct), lambda b, c: (0, c)),
            pl.BlockSpec((1, ct), lambda b, c: (0, c)),
        ],
        out_specs=pl.BlockSpec((None, T, ct), lambda b, c: (b, 0, c)),
        out_shape=jax.ShapeDtypeStruct((B, T, 2 * ML_W), BF16),
        compiler_params=_cparams(("parallel", "parallel")),
        name="ml_conv",
    )(mlf, conv_w, conv_b)


def _log_sigmoid(x):
    return jnp.minimum(x, 0.0) - jnp.log1p(jnp.exp(-jnp.abs(x)))


def _segment_scan(x, seg, reverse):
    n = x.shape[1]
    pos = lax.broadcasted_iota(jnp.int32, x.shape, 1) % seg
    k = 1
    while k < seg:
        if reverse:
            x = x + jnp.where(pos < seg - k, pltpu.roll(x, n - k, 1), 0.0)
        else:
            x = x + jnp.where(pos >= k, pltpu.roll(x, k, 1), 0.0)
        k *= 2
    return x


def _gate_prep_kernel(gt_ref, gbias_ref, g_ref, b_ref, r_ref):
    gt = gt_ref[...] + gbias_ref[...]
    lf = _log_sigmoid(gt)
    row = lax.broadcasted_iota(jnp.int32, gt.shape, 0)
    b_rows = jnp.where(row < GATE_W // 2, _segment_scan(lf, TT, False), _segment_scan(lf, TT, True))
    r_rows = gt - pltpu.roll(b_rows, GATE_W - ML_HEADS, 0)
    g_ref[...] = gt
    b_ref[...] = b_rows
    r_ref[...] = jnp.concatenate([r_rows, jnp.zeros((128 - GATE_W, gt.shape[1]), F32)], axis=0).T


def _gate_prep_call(mlgt, gate_b):
    B, _, T = mlgt.shape
    row_spec = pl.BlockSpec((None, GATE_W, T), lambda b: (b, 0, 0))
    row_sds = jax.ShapeDtypeStruct((B, GATE_W, T), F32)
    return pl.pallas_call(
        _gate_prep_kernel,
        grid=(B,),
        in_specs=[row_spec, pl.BlockSpec((GATE_W, 1), lambda b: (0, 0))],
        out_specs=[row_spec, row_spec, pl.BlockSpec((None, T, 128), lambda b: (b, 0, 0))],
        out_shape=[row_sds, row_sds, jax.ShapeDtypeStruct((B, T, 128), F32)],
        compiler_params=_cparams(("parallel",)),
        name="ml_gates",
    )(mlgt, gate_b.reshape(GATE_W, 1))


def _mlstm_kernel(qkf_ref, qkb_ref, vtf_ref, vtb_ref, gf_ref, gb_ref, bf_ref, bb_ref, rf_ref, rb_ref,
                  hf_ref, hb_ref, c_scr, n_scr, m_scr):
    L = TT

    @pl.when(pl.program_id(1) == 0)
    def _():
        c_scr[...] = jnp.zeros(c_scr.shape, F32)
        n_scr[...] = jnp.zeros(n_scr.shape, F32)
        m_scr[...] = jnp.zeros(m_scr.shape, F32)

    key = lax.broadcasted_iota(jnp.int32, (L, L), 0)
    qry = lax.broadcasted_iota(jnp.int32, (L, L), 1)
    row8 = lax.broadcasted_iota(jnp.int32, (8, L), 0)
    nt_dims = (((1,), (1,)), ((), ()))

    chains = []
    dirs = [(i, d, refs)
            for i in range(ML_NB)
            for d, refs in enumerate(((qkf_ref, vtf_ref, gf_ref, bf_ref, rf_ref, hf_ref),
                                      (qkb_ref, vtb_ref, gb_ref, bb_ref, rb_ref, hb_ref)))]
    for i, d, refs in dirs:
        qk_ref, vt_ref, g_ref, b_ref, r_ref, h_ref = [r.at[i] for r in refs]
        gt = g_ref[...]
        b_rows = b_ref[...]
        r_cols = r_ref[...]
        mask = (key <= qry) if d == 0 else (key >= qry)
        for hd in range(ML_HEADS):
            ci = d * 2 * ML_HEADS + hd
            b_row = b_rows[ci + ML_HEADS:ci + ML_HEADS + 1, :]
            chains.append(dict(
                si=(i * 2 + d) * ML_HEADS + hd, hd=hd, h_ref=h_ref, mask=mask,
                q=qk_ref[:, hd * ML_DH:(hd + 1) * ML_DH],
                k=qk_ref[:, ML_W + hd * ML_DH:ML_W + (hd + 1) * ML_DH],
                vt=vt_ref[hd * ML_DH:(hd + 1) * ML_DH, :],
                li_row=gt[ci:ci + 1, :], b_row=b_row, r_col=r_cols[:, ci:ci + 1],
                btot=b_row[:, L - 1:L] if d == 0 else b_row[:, 0:1]))

    for ch in chains:
        m = m_scr[ch["si"], 0:1, 0:1]
        dmat = jnp.where(ch["mask"], ch["r_col"] + ch["b_row"], -jnp.inf)
        inter = ch["b_row"] + m
        m_t = jnp.maximum(inter, jnp.max(dmat, axis=0, keepdims=True))
        ch.update(m=m, m_t=m_t, w=jnp.exp(dmat - m_t), s_inter=jnp.exp(inter - m_t),
                  s=lax.dot_general(ch["k"], ch["q"], nt_dims, preferred_element_type=F32))

    for ch in chains:
        si, hd, q = ch["si"], ch["hd"], ch["q"]
        cmat = c_scr[si]
        n8 = n_scr[si]
        qkw = ch["s"] * ch["w"]
        num = (jnp.dot(ch["vt"].astype(BF16), qkw.astype(BF16), preferred_element_type=F32)
               + ch["s_inter"] * lax.dot_general(cmat.astype(BF16), q, nt_dims,
                                                 preferred_element_type=F32))
        nq = lax.dot_general(n8.astype(BF16), q, nt_dims, preferred_element_type=F32)[0:1, :]
        den = jnp.sum(qkw, axis=0, keepdims=True) + ch["s_inter"] * nq
        ht = num / jnp.maximum(jnp.abs(den), jnp.exp(-ch["m_t"]))
        ch["h_ref"][:, hd * ML_DH:(hd + 1) * ML_DH] = ht.T

    for ch in chains:
        si, m, btot = ch["si"], ch["m"], ch["btot"]
        wk = btot - ch["b_row"] + ch["li_row"]
        m_new = jnp.maximum(btot + m, jnp.max(wk, axis=1, keepdims=True))
        decay = jnp.exp(btot + m - m_new)
        wk = jnp.exp(wk - m_new)
        c_scr[si] = decay * c_scr[si] + jnp.dot((ch["vt"] * wk).astype(BF16), ch["k"],
                                                preferred_element_type=F32)
        wk8 = jnp.where(row8 == 0, wk, 0.0).astype(BF16)
        n_scr[si] = decay * n_scr[si] + jnp.dot(wk8, ch["k"], preferred_element_type=F32)
        m_scr[si] = jnp.broadcast_to(m_new, m_scr.shape[1:])


def _mlstm_call(qk, mlvt, gates, b_rows, r_cols):
    B, T, _ = qk.shape
    nt = T // TT

    def bwd_t(t):
        return jnp.where(t == 0, 0, nt - t)

    def spec(n, back):
        if back:
            return pl.BlockSpec((ML_NB, TT, n), lambda b, t: (b, bwd_t(t), 0))
        return pl.BlockSpec((ML_NB, TT, n), lambda b, t: (b, t, 0))

    def tspec(n, back):
        if back:
            return pl.BlockSpec((ML_NB, n, TT), lambda b, t: (b, 0, bwd_t(t)))
        return pl.BlockSpec((ML_NB, n, TT), lambda b, t: (b, 0, t))

    assert B % ML_NB == 0
    nch = ML_NB * 2 * ML_HEADS
    out_sds = jax.ShapeDtypeStruct((B, T, ML_W), F32)
    return pl.pallas_call(
        _mlstm_kernel,
        grid=(B // ML_NB, nt),
        in_specs=[
            spec(2 * ML_W, False), spec(2 * ML_W, True),
            tspec(ML_W, False), tspec(ML_W, True),
            tspec(GATE_W, False), tspec(GATE_W, True),
            tspec(GATE_W, False), tspec(GATE_W, True),
            spec(128, False), spec(128, True),
        ],
        out_specs=[spec(ML_W, False), spec(ML_W, True)],
        out_shape=[out_sds, out_sds],
        scratch_shapes=[
            pltpu.VMEM((nch, ML_DH, ML_DH), F32),
            pltpu.VMEM((nch, 8, ML_DH), F32),
            pltpu.VMEM((nch, 8, 128), F32),
        ],
        compiler_params=_cparams(("parallel", "arbitrary")),
        name="mlstm_scan",
    )(qk, qk, mlvt, mlvt, gates, gates, b_rows, b_rows, r_cols, r_cols)


def _diff_kernel(q_ref, k_ref, vx_ref, lam_ref, g_ref, o_ref, sa_scr, sb_scr, ma_scr, mb_scr,
                 acca_scr, accb_scr, *, lam_init, n_ctx):
    t = pl.program_id(2)
    nt = pl.num_programs(2) - 2
    nk = k_ref.shape[0]
    dn = (((1,), (1,)), ((), ()))
    all_chunks = [(c, min(c + DF_KCHUNK, nk)) for c in range(0, nk, DF_KCHUNK)]
    ctx_chunks = [ch for ch in all_chunks if ch[0] < n_ctx]
    slots = ((sa_scr, ma_scr), (sb_scr, mb_scr))
    accs = (acca_scr, accb_scr)

    def step(out_acc, finish_chunks, rd, acc_wr, score_chunks, wr):
        if out_acc is not None:
            lv = lam_ref[...]
            lam = (jnp.exp(jnp.sum(lv[0:1, :] * lv[1:2, :], axis=1, keepdims=True))
                   - jnp.exp(jnp.sum(lv[2:3, :] * lv[3:4, :], axis=1, keepdims=True)) + lam_init)
            outs = [out_acc[j, 0:DF_DV, :] / out_acc[j, DF_DV:DF_DV + 1, :] for j in range(2)]
            o = outs[0] - lam * outs[1]
            var = jnp.mean(o * o, axis=0, keepdims=True)
            on = o * lax.rsqrt(var + EPS) * g_ref[...] * (1.0 - lam_init)
            o_ref[...] = on.T.astype(o_ref.dtype)
        acc = [None, None]
        m_new = [None, None]
        if score_chunks:
            q = q_ref[...]
            lane = lax.broadcasted_iota(jnp.int32, q.shape, 1)
            zero = jnp.zeros(q.shape, q.dtype)
            qms = (jnp.where(lane < DF_DQK, q, zero), jnp.where(lane < DF_DQK, zero, q))
        if finish_chunks:
            m_old = [rd[1][j, 0:1, :] for j in range(2)]
        for i in range(max(len(finish_chunks), len(score_chunks))):
            for j in range(2):
                if i < len(finish_chunks):
                    c0, c1 = finish_chunks[i]
                    e = jnp.exp2(rd[0][j, c0:c1, :] - m_old[j]).astype(BF16)
                    part = jnp.dot(vx_ref[:, c0:c1], e, preferred_element_type=F32)
                    acc[j] = part if acc[j] is None else acc[j] + part
                if i < len(score_chunks):
                    c0, c1 = score_chunks[i]
                    s = lax.dot_general(k_ref[c0:c1, :], qms[j], dn, preferred_element_type=F32)
                    wr[0][j, c0:c1, :] = s
                    cm = jnp.max(s, axis=0, keepdims=True)
                    m_new[j] = cm if m_new[j] is None else jnp.maximum(m_new[j], cm)
        if score_chunks:
            for j in range(2):
                wr[1][j] = jnp.broadcast_to(m_new[j], wr[1].shape[1:])
        if finish_chunks:
            for j in range(2):
                acc_wr[j] = acc[j]

    @pl.when(t == 0)
    def _():
        step(None, [], None, None, ctx_chunks, slots[0])

    @pl.when(t == 1)
    def _():
        step(None, ctx_chunks, slots[0], accs[0], all_chunks, slots[1])

    for p in range(2):
        @pl.when((t % 2 == p) & (t >= 2) & (t < nt))
        def _():
            step(accs[p], all_chunks, slots[1 - p], accs[1 - p], all_chunks, slots[p])

        @pl.when((t % 2 == p) & (t == nt))
        def _():
            step(accs[p], all_chunks, slots[1 - p], accs[1 - p], [], None)

        @pl.when((t % 2 == p) & (t == nt + 1))
        def _():
            step(accs[p], [], None, None, [], None)


def _diff_call(dfq, dfk, dfvx, lam_vecs, norm_g, lam_init, n_ctx):
    B, T, _ = dfq.shape
    nt = T // TT
    assert n_ctx % DF_KCHUNK == 0 and nt >= 2
    return pl.pallas_call(
        functools.partial(_diff_kernel, lam_init=lam_init, n_ctx=n_ctx),
        grid=(B, DF_HEADS, nt + 2),
        in_specs=[
            pl.BlockSpec((None, TT, DF_DV), lambda b, h, t: (b, jnp.minimum(t, nt - 1), h)),
            pl.BlockSpec((None, T, DF_DV), lambda b, h, t: (b, 0, h)),
            pl.BlockSpec((None, VEXT_ROWS, T), lambda b, h, t: (b, h, 0)),
            pl.BlockSpec((4, DF_DQK), lambda b, h, t: (0, 0)),
            pl.BlockSpec((DF_DV, 1), lambda b, h, t: (h, 0)),
        ],
        out_specs=pl.BlockSpec((None, TT, DF_DV), lambda b, h, t: (b, jnp.maximum(t - 2, 0), h)),
        out_shape=jax.ShapeDtypeStruct((B, T, DF_W), BF16),
        scratch_shapes=[pltpu.VMEM((2, T, TT), F32), pltpu.VMEM((2, T, TT), F32),
                        pltpu.VMEM((2, 8, TT), F32), pltpu.VMEM((2, 8, TT), F32),
                        pltpu.VMEM((2, VEXT_ROWS, TT), F32), pltpu.VMEM((2, VEXT_ROWS, TT), F32)],
        compiler_params=_cparams(("parallel", "parallel", "arbitrary")),
        name="diff_attn",
    )(dfq, dfk, dfvx, lam_vecs, norm_g.reshape(DF_W, 1))


def _na_tables(rows):
    kh = min(NA_KH, rows)
    assert kh == NA_KH and rows >= NA_KROWS and rows % NA_QROWS == 0
    n_tiles = rows // NA_QROWS
    qc = np.arange(GRID_W)
    qstart = np.clip(qc - NA_KW // 2, 0, GRID_W - NA_KW)
    col_ok = (qc[None, :] >= qstart[:, None]) & (qc[None, :] < qstart[:, None] + NA_KW)
    drs, valids = [], []
    for tile in (0, 1, n_tiles - 1):
        r0 = tile * NA_QROWS
        start = int(np.clip(r0 - kh // 2, 0, rows - NA_KROWS))
        r = r0 + np.arange(NA_QROWS)
        rs = np.clip(r - kh // 2, 0, rows - kh)
        akr = start + np.arange(NA_KROWS)
        row_ok = (akr[None, :] >= rs[:, None]) & (akr[None, :] < rs[:, None] + kh)
        drs.append(np.clip(akr[None, :] - r[:, None] + NA_KH - 1, 0, 2 * NA_KH - 2))
        ok = row_ok[:, None, :, None] & col_ok[None, :, None, :]
        valids.append(ok.reshape(TT, NA_KROWS * GRID_W))
    return np.stack(drs), np.stack(valids)


def _na_bias_tables(rel_bias, rows):
    dr, valid = _na_tables(rows)
    L, H = rel_bias.shape[:2]
    pad = GRID_W - NA_KW
    flipped = jnp.flip(jnp.pad(rel_bias.astype(F32) * LOG2E, ((0, 0), (0, 0), (0, 0), (pad, pad)),
                               mode="edge"), axis=-1)
    off = flipped.shape[-1] - 1 - (NA_KW - 1 + pad)
    by_col = jnp.stack([flipped[..., off - kc:off - kc + GRID_W] for kc in range(GRID_W)], axis=-2)
    tabs = []
    for ty in range(3):
        per_kr = [jnp.stack([by_col[:, :, int(dr[ty][qr, kr])] for qr in range(NA_QROWS)], axis=3)
                  for kr in range(NA_KROWS)]
        tab = jnp.stack(per_kr, axis=2).reshape(L, H, NA_KROWS * GRID_W, TT)
        tabs.append(jnp.where(valid[ty].T, tab, NEG))
    return jnp.stack(tabs, axis=1)


def _na_kernel(q_ref, k_ref, vx_ref, bias_ref, o_ref, *, n_ctx, rows):
    t = pl.program_id(1)
    dn = (((1,), (1,)), ((), ()))
    tile_type = jnp.where(t <= 1, 0, jnp.where(t == pl.num_programs(1) - 1, 2, 1))

    def run(key_slices):
        def scores(hd):
            sl = slice(hd * NA_DH, (hd + 1) * NA_DH)
            q = q_ref[:, sl]
            parts = []
            for ks, add_bias in key_slices:
                s = lax.dot_general(k_ref[ks, sl], q, dn, preferred_element_type=F32)
                parts.append(s + bias_ref[tile_type, hd] if add_bias else s)
            m = jnp.max(parts[0], axis=0, keepdims=True)
            for p in parts[1:]:
                m = jnp.maximum(m, jnp.max(p, axis=0, keepdims=True))
            return parts, m

        def finish(hd, parts, m):
            o = None
            for (ks, _), s in zip(key_slices, parts):
                e = jnp.exp2(s - m).astype(BF16)
                part = jnp.dot(vx_ref[hd * NA_VROWS:(hd + 1) * NA_VROWS, ks], e,
                               preferred_element_type=F32)
                o = part if o is None else o + part
            return o[0:NA_DH, :] / o[NA_DH:NA_DH + 1, :]

        prev = None
        done = []
        for hd in range(NA_HEADS + 1):
            cur = scores(hd) if hd < NA_HEADS else None
            if prev is not None:
                done.append(finish(hd - 1, *prev))
                if len(done) == 2:
                    c0 = (hd - 2) * NA_DH
                    o_ref[:, c0:c0 + 2 * NA_DH] = jnp.concatenate(done, axis=0).T.astype(o_ref.dtype)
                    done = []
            prev = cur

    @pl.when(t == 0)
    def _():
        run([(slice(0, n_ctx), False)])

    @pl.when(t != 0)
    def _():
        r0 = (t - 1) * NA_QROWS
        start = jnp.clip(r0 - NA_KH // 2, 0, rows - NA_KROWS)
        tok0 = pl.multiple_of(n_ctx + start * GRID_W, TT)
        run([(pl.ds(tok0, NA_KROWS * GRID_W), True), (slice(0, n_ctx), False)])


def _na_call(naq, nak, navx, bias_tab, n_ctx, l):
    B, T, _ = naq.shape
    nt = T // TT
    rows = (T - n_ctx) // GRID_W
    nk = NA_KROWS * GRID_W
    assert n_ctx % TT == 0 and (NA_KH // 2) % NA_QROWS == 0 and (rows - NA_KROWS) % NA_QROWS == 0

    return pl.pallas_call(
        functools.partial(_na_kernel, n_ctx=n_ctx, rows=rows),
        grid=(B, nt),
        in_specs=[
            pl.BlockSpec((None, TT, NA_W), lambda b, t: (b, t, 0)),
            pl.BlockSpec((None, T, NA_W), lambda b, t: (b, 0, 0)),
            pl.BlockSpec((None, NA_HEADS * NA_VROWS, T), lambda b, t: (b, 0, 0)),
            pl.BlockSpec((None, 3, NA_HEADS, nk, TT), lambda b, t: (l, 0, 0, 0, 0),
                         pipeline_mode=pl.Buffered(1)),
        ],
        out_specs=pl.BlockSpec((None, TT, NA_W), lambda b, t: (b, t, 0)),
        out_shape=jax.ShapeDtypeStruct((B, T, NA_W), BF16),
        compiler_params=_cparams(("parallel", "arbitrary")),
        name="na_attn",
    )(naq, nak, navx, bias_tab)


def _merge_tile(x_ref, mod_ref, g_ref, wg_ref, wb_ref, wo_ref, hf_ref, hb_ref, o_ref_in,
                  mlg_ref, bd_ref, cn_ref, out_ref):
    x = x_ref[...]
    D = x.shape[1]
    hn = _norm_mod(x, g_ref[...], mod_ref[3:4, :], mod_ref[4:5, :]).astype(BF16)
    hsum = hf_ref[...] + hb_ref[...]
    parts = []
    for hd in range(ML_HEADS):
        hh = hsum[:, hd * ML_DH:(hd + 1) * ML_DH]
        parts.append(hh * lax.rsqrt(jnp.mean(hh * hh, axis=1, keepdims=True) + EPS))
    a = (_sigmoid(o_ref_in[...]) * (jnp.concatenate(parts, axis=1) * mlg_ref[...])).astype(BF16)
    y = None
    for i, br in enumerate((a, bd_ref[...], cn_ref[...])):
        gate = _sigmoid(jnp.dot(hn, wg_ref[:, i * D:(i + 1) * D], preferred_element_type=F32))
        term = gate * jnp.dot(br, wb_ref[i], preferred_element_type=F32)
        y = term if y is None else y + term
    z = jnp.dot(y.astype(BF16), wo_ref[...], preferred_element_type=F32)
    out_ref[...] = x + mod_ref[5:6, :] * z


def _merge_call(x, modtab, g, w_gate, w_branch, w_out, hf, hb, mlf, ml_norm_g, bd, cn, l):
    B, T, D = x.shape

    def tile_spec(n, cblk=0):
        return pl.BlockSpec((DENSE_NB, TT, n), lambda b, t: (b, t, cblk))

    return pl.pallas_call(
        _per_batch_element(_merge_tile, n_in=12, shared=(2, 3, 4, 5, 9)),
        grid=(B // DENSE_NB, T // TT),
        in_specs=[
            tile_spec(D),
            _mod_spec(DENSE_NB),
            pl.BlockSpec((1, D), lambda b, t: (0, 0)),
            pl.BlockSpec((None, D, N_BRANCH * D), lambda b, t: (l, 0, 0)),
            pl.BlockSpec((None, N_BRANCH, BRANCH_W, D), lambda b, t: (l, 0, 0, 0)),
            pl.BlockSpec((None, D, D), lambda b, t: (l, 0, 0)),
            tile_spec(ML_W), tile_spec(ML_W),
            tile_spec(ML_W, 2),
            pl.BlockSpec((1, ML_W), lambda b, t: (0, 0)),
            tile_spec(DF_W), tile_spec(NA_W),
        ],
        out_specs=tile_spec(D),
        out_shape=jax.ShapeDtypeStruct(x.shape, F32),
        compiler_params=_cparams(("parallel", "parallel")),
        name="merge",
    )(x, modtab, g, w_gate, w_branch, w_out, hf, hb, mlf, ml_norm_g, bd, cn)


def _rope_tables(S, n_ctx):
    t = jnp.arange(S)
    rowp = (t // GRID_W).astype(F32)
    colp = (t % GRID_W).astype(F32)
    half = DF_DQK // 2
    freqs = ROPE_BASE ** (-jnp.arange(0, half, 2, dtype=F32) / half)
    ang = jnp.concatenate([rowp[:, None] * freqs, colp[:, None] * freqs], axis=-1)
    cos, sin = jnp.cos(ang), jnp.sin(ang)
    cos = jnp.concatenate([jnp.ones((n_ctx, half), F32), cos], axis=0)
    sin = jnp.concatenate([jnp.zeros((n_ctx, half), F32), sin], axis=0)
    cos_t = jnp.tile(cos, (1, 4))
    sin_t = jnp.tile(jnp.concatenate([-sin, sin], axis=1), (1, 2))
    return cos_t, sin_t


def kernel(x, c, ctx, c_ctx, w_ada, b_ada, norm_g, ffn_w1, ffn_w2, w_in, ml_conv_w, ml_conv_b,
           ml_gate_b, ml_norm_g, df_lambda, df_norm_g, na_rel_bias, w_branch, w_out, final_g):
    B, S, D = x.shape
    n_ctx = ctx.shape[1]
    depth = w_ada.shape[0]
    assert D == D_MODEL and n_ctx == TT and S % TT == 0 and S % GRID_W == 0
    T = n_ctx + S
    rows = S // GRID_W

    w1 = ffn_w1.astype(BF16)
    w2 = ffn_w2.astype(BF16)
    w_tok, w_tr, w_gate = _split_w_in(w_in)
    wb = w_branch.astype(BF16)
    wo = w_out.astype(BF16)

    cvec = jnp.zeros((16, D), F32).at[0].set(c_ctx).at[1:B + 1].set(c)
    mod = _ada_call(cvec, w_ada, b_ada).reshape(depth, 16, 9, D)
    modtab = jnp.stack([jnp.broadcast_to(mod[:, 0:1], (depth, B, 9, D)), mod[:, 1:B + 1]], axis=2)

    cos_t, sin_t = _rope_tables(S, n_ctx)
    bias_tab = _na_bias_tables(na_rel_bias, rows)

    h = x
    for l in range(depth):
        lam_init = 0.8 - 0.6 * math.exp(-0.3 * l)
        mt = modtab[l]
        h = _ffn_call(h, mt, norm_g[l, 0][None], w1, w2, l, 0, 0,
                      ctx_in=ctx if l == 0 else None, n_ctx=n_ctx)
        mlf, mlvt, mlgt, dfq, dfk, dfvx, naq, nak, navx = _inproj_call(
            h, mt, norm_g[l, 1][None], w_tok, w_tr, cos_t, sin_t, l)
        qk = _conv_call(mlf, ml_conv_w[l], ml_conv_b[l][None], n_ctx)
        hf, hb = _mlstm_call(qk, mlvt, *_gate_prep_call(mlgt, ml_gate_b[l]))
        bd = _diff_call(dfq, dfk, dfvx, df_lambda[l], df_norm_g[l], lam_init, n_ctx)
        cn = _na_call(naq, nak, navx, bias_tab, n_ctx, l)
        h = _merge_call(h, mt, norm_g[l, 1][None], w_gate, wb, wo, hf, hb, mlf,
                        ml_norm_g[l][None], bd, cn, l)
        last = l == depth - 1
        h = _ffn_call(h, mt, norm_g[l, 2][None], w1, w2, l, 1, 6,
                      final_g=final_g[None] if last else None, n_ctx=n_ctx)
    return h
```

```python
import functools
import math

import numpy as np
import jax
import jax.numpy as jnp
from jax import lax
from jax.experimental import pallas as pl
from jax.experimental.pallas import tpu as pltpu

F32 = jnp.float32
BF16 = jnp.bfloat16

D_MODEL = 1024
GRID_W = 64
N_BRANCH = 3
BRANCH_W = 512
ML_HEADS = 4
ML_DH = 128
ML_W = ML_HEADS * ML_DH
DF_HEADS = 4
DF_DQK = 64
DF_DV = 128
DF_W = DF_HEADS * DF_DV
ROPE_BASE = 10000.0
NA_HEADS = 8
NA_DH = 64
NA_W = NA_HEADS * NA_DH
NA_KH = 8
NA_KW = 16
D_FF = 2816
EPS = 1e-6
GATE_W = 4 * ML_HEADS

DENSE_NB = 2
ML_NB = 4
TT = 256
NA_QROWS = 4
NA_KROWS = 12
MXU_N = 256
FF_CHUNKS = ((0, 6 * MXU_N), (6 * MXU_N, D_FF))
DF_KCHUNK = MXU_N
ONES_ROWS = 16
VEXT_ROWS = DF_DV + ONES_ROWS
NA_VROWS = NA_DH + ONES_ROWS
NEG = -1e30
LOG2E = math.log2(math.e)
VMEM_LIMIT = 56 * 1024 * 1024


def _cparams(sem):
    return pltpu.CompilerParams(dimension_semantics=sem, vmem_limit_bytes=VMEM_LIMIT)


def _norm_mod(x, g, shift, scale):
    var = jnp.mean(x * x, axis=-1, keepdims=True)
    return x * lax.rsqrt(var + EPS) * g * (1.0 + scale) + shift


def _sigmoid(x):
    return 1.0 / (1.0 + jnp.exp(-x))


def _ada_kernel(c_ref, w_ref, b_ref, o_ref):
    c = c_ref[...]
    h = (c * _sigmoid(c)).astype(BF16)
    o_ref[...] = jnp.dot(h, w_ref[...].astype(BF16), preferred_element_type=F32) + b_ref[...]


def _ada_call(cvec, w_ada, b_ada):
    L, D, N = w_ada.shape
    tn = N // 8
    return pl.pallas_call(
        _ada_kernel,
        grid=(L, N // tn),
        in_specs=[
            pl.BlockSpec(cvec.shape, lambda l, j: (0, 0)),
            pl.BlockSpec((None, D, tn), lambda l, j: (l, 0, j)),
            pl.BlockSpec((None, 1, tn), lambda l, j: (l, 0, j)),
        ],
        out_specs=pl.BlockSpec((None, cvec.shape[0], tn), lambda l, j: (l, 0, j)),
        out_shape=jax.ShapeDtypeStruct((L, cvec.shape[0], N), F32),
        compiler_params=_cparams(("parallel", "parallel")),
        name="adaln",
    )(cvec, w_ada, b_ada.reshape(L, 1, N))


def _ffn_kernel(x_ref, mod_ref, g_ref, w1_ref, w2_ref, *rest, k0, final, split_in):
    o_ref = rest[-1]
    for i in range(DENSE_NB):
        x = x_ref[i]
        if split_in:
            x = jnp.where(pl.program_id(1) == 0, rest[0][i], x)
        h = _norm_mod(x, g_ref[...], mod_ref[i, k0:k0 + 1, :], mod_ref[i, k0 + 1:k0 + 2, :]).astype(BF16)
        acc = jnp.zeros(x.shape, F32)
        for c0, c1 in FF_CHUNKS:
            a = jnp.dot(h, w1_ref[:, c0:c1], preferred_element_type=F32)
            b = jnp.dot(h, w1_ref[:, D_FF + c0:D_FF + c1], preferred_element_type=F32)
            u = (a * _sigmoid(a) * b).astype(BF16)
            acc = acc + jnp.dot(u, w2_ref[c0:c1, :], preferred_element_type=F32)
        y = x + 0.5 * mod_ref[i, k0 + 2:k0 + 3, :] * acc
        if final:
            y = y * lax.rsqrt(jnp.mean(y * y, axis=-1, keepdims=True) + EPS) * rest[-2][...]
        o_ref[i] = y


def _mod_spec(nb=None):
    return pl.BlockSpec((nb, None, 9, D_MODEL), lambda b, t: (b, jnp.minimum(t, 1), 0, 0))


def _ffn_call(x, modtab, g, w1, w2, l, i, k0, final_g=None, ctx_in=None, n_ctx=0):
    B, _, D = x.shape
    T = x.shape[1] + (n_ctx if ctx_in is not None else 0)
    final = final_g is not None
    off = n_ctx // TT
    if ctx_in is not None:
        x_spec = pl.BlockSpec((DENSE_NB, TT, D), lambda b, t: (b, jnp.maximum(t - off, 0), 0))
    else:
        x_spec = pl.BlockSpec((DENSE_NB, TT, D), lambda b, t: (b, t, 0))
    assert B % DENSE_NB == 0
    in_specs = [
        x_spec,
        _mod_spec(DENSE_NB),
        pl.BlockSpec((1, D), lambda b, t: (0, 0)),
        pl.BlockSpec((None, None, D, 2 * D_FF), lambda b, t: (l, i, 0, 0)),
        pl.BlockSpec((None, None, D_FF, D), lambda b, t: (l, i, 0, 0)),
    ]
    args = [x, modtab, g, w1, w2]
    if ctx_in is not None:
        assert n_ctx == TT
        in_specs.append(pl.BlockSpec((DENSE_NB, TT, D), lambda b, t: (b, 0, 0)))
        args.append(ctx_in)
    if final:
        in_specs.append(pl.BlockSpec((1, D), lambda b, t: (0, 0)))
        args.append(final_g)
        out_spec = pl.BlockSpec((DENSE_NB, TT, D), lambda b, t: (b, jnp.maximum(t - off, 0), 0))
        out_shape = jax.ShapeDtypeStruct((B, T - n_ctx, D), F32)
    else:
        out_spec = pl.BlockSpec((DENSE_NB, TT, D), lambda b, t: (b, t, 0))
        out_shape = jax.ShapeDtypeStruct((B, T, D), F32)
    return pl.pallas_call(
        functools.partial(_ffn_kernel, k0=k0, final=final, split_in=ctx_in is not None),
        grid=(B // DENSE_NB, T // TT),
        in_specs=in_specs,
        out_specs=out_spec,
        out_shape=out_shape,
        compiler_params=_cparams(("parallel", "arbitrary")),
        name="ffn_half",
    )(*args)


def _per_batch_element(tile_fn, n_in, shared):
    def kern(*refs):
        for i in range(DENSE_NB):
            tile_fn(*[r if (j < n_in and j in shared) else r.at[i] for j, r in enumerate(refs)])
    return kern


def _inproj_tile(x_ref, mod_ref, g_ref, w_ref, wt_ref, cos_ref, sin_ref,
                 mlf_ref, mlvt_ref, mlgt_ref, dfq_ref, dfk_ref, dfvx_ref, naq_ref, nak_ref, navx_ref):
    x = x_ref[...]
    h = _norm_mod(x, g_ref[...], mod_ref[3:4, :], mod_ref[4:5, :]).astype(BF16)

    def proj(c0, n):
        return jnp.dot(h, w_ref[:, c0:c0 + n], preferred_element_type=F32)

    def proj_t(r0, n):
        return lax.dot_general(wt_ref[r0:r0 + n, :], h, (((1,), (1,)), ((), ())),
                               preferred_element_type=F32)

    mlf_ref[...] = proj(0, 3 * ML_W)
    mlvt_ref[...] = proj_t(0, ML_W)
    mlgt_ref[...] = proj_t(ML_W + DF_W, GATE_W)
    ones_rows = jnp.where(lax.broadcasted_iota(jnp.int32, (ONES_ROWS, TT), 0) == 0, 1.0, 0.0).astype(BF16)
    for vx_ref, r0, heads, dh in ((dfvx_ref, ML_W, DF_HEADS, DF_DV),
                                  (navx_ref, ML_W + DF_W + GATE_W, NA_HEADS, NA_DH)):
        vt = proj_t(r0, heads * dh).astype(BF16)
        for hd in range(heads):
            r = hd * (dh + ONES_ROWS)
            vx_ref[r:r + dh, :] = vt[hd * dh:(hd + 1) * dh, :]
            vx_ref[r + dh:r + dh + ONES_ROWS, :] = ones_rows
    cos = jnp.tile(cos_ref[...], (1, DF_HEADS))
    sin = jnp.tile(sin_ref[...], (1, DF_HEADS))
    lane = lax.broadcasted_iota(jnp.int32, (TT, DF_W), 1)
    first_half = (lane % DF_DQK) < (DF_DQK // 2)

    def rope(v):
        partner = jnp.where(first_half, pltpu.roll(v, DF_W - DF_DQK // 2, 1),
                            pltpu.roll(v, DF_DQK // 2, 1))
        return v * cos + partner * sin

    c0 = 3 * ML_W
    dfq_ref[...] = (rope(proj(c0, DF_W)) * (DF_DQK ** -0.5 * LOG2E)).astype(BF16)
    dfk_ref[...] = rope(proj(c0 + DF_W, DF_W)).astype(BF16)
    c0 = c0 + 2 * DF_W
    naq_ref[...] = (proj(c0, NA_W) * (NA_DH ** -0.5 * LOG2E)).astype(BF16)
    nak_ref[...] = proj(c0 + NA_W, NA_W).astype(BF16)


def _split_w_in(w_in):
    o_mlv, o_mlo, o_mlg = 2 * ML_W, 3 * ML_W, 4 * ML_W
    o_dfq = o_mlg + GATE_W
    o_dfv = o_dfq + 2 * DF_W
    o_na = o_dfv + DF_W
    o_nav = o_na + 2 * NA_W
    o_gate = o_na + 3 * NA_W
    w_tok = jnp.concatenate([w_in[:, :, 0:o_mlv], w_in[:, :, o_mlo:o_mlg], w_in[:, :, o_dfq:o_dfv],
                             w_in[:, :, o_na:o_nav]], axis=-1)
    w_tr = jnp.concatenate([w_in[:, :, o_mlv:o_mlo], w_in[:, :, o_dfv:o_na], w_in[:, :, o_mlg:o_dfq],
                            w_in[:, :, o_nav:o_gate]], axis=-1)
    return (w_tok.astype(BF16), jnp.swapaxes(w_tr, 1, 2).astype(BF16), w_in[:, :, o_gate:].astype(BF16))


def _inproj_call(x, modtab, g, w_tok, w_tr, cos_t, sin_t, l):
    B, T, D = x.shape

    def tile_spec(n):
        return pl.BlockSpec((DENSE_NB, TT, n), lambda b, t: (b, t, 0))

    def sds(n, dt):
        return jax.ShapeDtypeStruct((B, T, n), dt)

    def tspec(n):
        return pl.BlockSpec((DENSE_NB, n, TT), lambda b, t: (b, 0, t))

    def tsds(n, dt):
        return jax.ShapeDtypeStruct((B, n, T), dt)

    return pl.pallas_call(
        _per_batch_element(_inproj_tile, n_in=7, shared=(2, 3, 4, 5, 6)),
        grid=(B // DENSE_NB, T // TT),
        in_specs=[
            tile_spec(D),
            _mod_spec(DENSE_NB),
            pl.BlockSpec((1, D), lambda b, t: (0, 0)),
            pl.BlockSpec((None, D, w_tok.shape[-1]), lambda b, t: (l, 0, 0)),
            pl.BlockSpec((None, w_tr.shape[1], D), lambda b, t: (l, 0, 0)),
            pl.BlockSpec((TT, 2 * DF_DQK), lambda b, t: (t, 0)),
            pl.BlockSpec((TT, 2 * DF_DQK), lambda b, t: (t, 0)),
        ],
        out_specs=[tile_spec(3 * ML_W), tspec(ML_W), tspec(GATE_W),
                   tile_spec(DF_W), tile_spec(DF_W), tspec(DF_HEADS * VEXT_ROWS),
                   tile_spec(NA_W), tile_spec(NA_W), tspec(NA_HEADS * NA_VROWS)],
        out_shape=[sds(3 * ML_W, F32), tsds(ML_W, F32), tsds(GATE_W, F32),
                   sds(DF_W, BF16), sds(DF_W, BF16), tsds(DF_HEADS * VEXT_ROWS, BF16),
                   sds(NA_W, BF16), sds(NA_W, BF16), tsds(NA_HEADS * NA_VROWS, BF16)],
        compiler_params=_cparams(("parallel", "parallel")),
        name="in_proj",
    )(x, modtab, g, w_tok, w_tr, cos_t, sin_t)


def _conv_kernel(x_ref, w_ref, b_ref, o_ref, *, n_ctx):
    x = x_ref[...]
    T = x.shape[0]
    row = lax.broadcasted_iota(jnp.int32, x.shape, 0)
    prev = jnp.where((row == 0) | (row == n_ctx), 0.0, pltpu.roll(x, 1, 0))
    nxt = jnp.where((row == n_ctx - 1) | (row == T - 1), 0.0, pltpu.roll(x, T - 1, 0))
    y = prev * w_ref[0:1, :] + x * w_ref[1:2, :] + nxt * w_ref[2:3, :] + b_ref[...]
    y = y * _sigmoid(y)
    scale = jnp.where(pl.program_id(1) < ML_W // x.shape[1], ML_DH ** -0.5, 1.0)
    o_ref[...] = (y * scale).astype(BF16)


def _conv_call(mlf, conv_w, conv_b, n_ctx):
    B, T, _ = mlf.shape
    ct = 256
    return pl.pallas_call(
        functools.partial(_conv_kernel, n_ctx=n_ctx),
        grid=(B, 2 * ML_W // ct),
        in_specs=[
            pl.BlockSpec((None, T, ct), lambda b, c: (b, 0, c)),
            pl.BlockSpec((3, ct), lambda b, c: (0, c)),
            pl.BlockSpec((1, ct), lambda b, c: (0, c)),
        ],
        out_specs=pl.BlockSpec((None, T, ct), lambda b, c: (b, 0, c)),
        out_shape=jax.ShapeDtypeStruct((B, T, 2 * ML_W), BF16),
        compiler_params=_cparams(("parallel", "parallel")),
        name="ml_conv",
    )(mlf, conv_w, conv_b)


def _log_sigmoid(x):
    return jnp.minimum(x, 0.0) - jnp.log1p(jnp.exp(-jnp.abs(x)))


def _segment_scan(x, seg, reverse):
    n = x.shape[1]
    pos = lax.broadcasted_iota(jnp.int32, x.shape, 1) % seg
    k = 1
    while k < seg:
        if reverse:
            x = x + jnp.where(pos < seg - k, pltpu.roll(x, n - k, 1), 0.0)
        else:
            x = x + jnp.where(pos >= k, pltpu.roll(x, k, 1), 0.0)
        k *= 2
    return x


def _gate_prep_kernel(gt_ref, gbias_ref, g_ref, b_ref, r_ref):
    gt = gt_ref[...] + gbias_ref[...]
    lf = _log_sigmoid(gt)
    row = lax.broadcasted_iota(jnp.int32, gt.shape, 0)
    b_rows = jnp.where(row < GATE_W // 2, _segment_scan(lf, TT, False), _segment_scan(lf, TT, True))
    r_rows = gt - pltpu.roll(b_rows, GATE_W - ML_HEADS, 0)
    g_ref[...] = gt
    b_ref[...] = b_rows
    r_ref[...] = jnp.concatenate([r_rows, jnp.zeros((128 - GATE_W, gt.shape[1]), F32)], axis=0).T


def _gate_prep_call(mlgt, gate_b):
    B, _, T = mlgt.shape
    row_spec = pl.BlockSpec((None, GATE_W, T), lambda b: (b, 0, 0))
    row_sds = jax.ShapeDtypeStruct((B, GATE_W, T), F32)
    return pl.pallas_call(
        _gate_prep_kernel,
        grid=(B,),
        in_specs=[row_spec, pl.BlockSpec((GATE_W, 1), lambda b: (0, 0))],
        out_specs=[row_spec, row_spec, pl.BlockSpec((None, T, 128), lambda b: (b, 0, 0))],
        out_shape=[row_sds, row_sds, jax.ShapeDtypeStruct((B, T, 128), F32)],
        compiler_params=_cparams(("parallel",)),
        name="ml_gates",
    )(mlgt, gate_b.reshape(GATE_W, 1))


def _mlstm_kernel(qkf_ref, qkb_ref, vtf_ref, vtb_ref, gf_ref, gb_ref, bf_ref, bb_ref, rf_ref, rb_ref,
                  hf_ref, hb_ref, c_scr, n_scr, m_scr):
    L = TT

    @pl.when(pl.program_id(1) == 0)
    def _():
        c_scr[...] = jnp.zeros(c_scr.shape, F32)
        n_scr[...] = jnp.zeros(n_scr.shape, F32)
        m_scr[...] = jnp.zeros(m_scr.shape, F32)

    key = lax.broadcasted_iota(jnp.int32, (L, L), 0)
    qry = lax.broadcasted_iota(jnp.int32, (L, L), 1)
    row8 = lax.broadcasted_iota(jnp.int32, (8, L), 0)
    nt_dims = (((1,), (1,)), ((), ()))

    chains = []
    dirs = [(i, d, refs)
            for i in range(ML_NB)
            for d, refs in enumerate(((qkf_ref, vtf_ref, gf_ref, bf_ref, rf_ref, hf_ref),
                                      (qkb_ref, vtb_ref, gb_ref, bb_ref, rb_ref, hb_ref)))]
    for i, d, refs in dirs:
        qk_ref, vt_ref, g_ref, b_ref, r_ref, h_ref = [r.at[i] for r in refs]
        gt = g_ref[...]
        b_rows = b_ref[...]
        r_cols = r_ref[...]
        mask = (key <= qry) if d == 0 else (key >= qry)
        for hd in range(ML_HEADS):
            ci = d * 2 * ML_HEADS + hd
            b_row = b_rows[ci + ML_HEADS:ci + ML_HEADS + 1, :]
            chains.append(dict(
                si=(i * 2 + d) * ML_HEADS + hd, hd=hd, h_ref=h_ref, mask=mask,
                q=qk_ref[:, hd * ML_DH:(hd + 1) * ML_DH],
                k=qk_ref[:, ML_W + hd * ML_DH:ML_W + (hd + 1) * ML_DH],
                vt=vt_ref[hd * ML_DH:(hd + 1) * ML_DH, :],
                li_row=gt[ci:ci + 1, :], b_row=b_row, r_col=r_cols[:, ci:ci + 1],
                btot=b_row[:, L - 1:L] if d == 0 else b_row[:, 0:1]))

    for ch in chains:
        m = m_scr[ch["si"], 0:1, 0:1]
        dmat = jnp.where(ch["mask"], ch["r_col"] + ch["b_row"], -jnp.inf)
        inter = ch["b_row"] + m
        m_t = jnp.maximum(inter, jnp.max(dmat, axis=0, keepdims=True))
        ch.update(m=m, m_t=m_t, w=jnp.exp(dmat - m_t), s_inter=jnp.exp(inter - m_t),
                  s=lax.dot_general(ch["k"], ch["q"], nt_dims, preferred_element_type=F32))

    for ch in chains:
        si, hd, q = ch["si"], ch["hd"], ch["q"]
        cmat = c_scr[si]
        n8 = n_scr[si]
        qkw = ch["s"] * ch["w"]
        num = (jnp.dot(ch["vt"].astype(BF16), qkw.astype(BF16), preferred_element_type=F32)
               + ch["s_inter"] * lax.dot_general(cmat.astype(BF16), q, nt_dims,
                                                 preferred_element_type=F32))
        nq = lax.dot_general(n8.astype(BF16), q, nt_dims, preferred_element_type=F32)[0:1, :]
        den = jnp.sum(qkw, axis=0, keepdims=True) + ch["s_inter"] * nq
        ht = num / jnp.maximum(jnp.abs(den), jnp.exp(-ch["m_t"]))
        ch["h_ref"][:, hd * ML_DH:(hd + 1) * ML_DH] = ht.T

    for ch in chains:
        si, m, btot = ch["si"], ch["m"], ch["btot"]
        wk = btot - ch["b_row"] + ch["li_row"]
        m_new = jnp.maximum(btot + m, jnp.max(wk, axis=1, keepdims=True))
        decay = jnp.exp(btot + m - m_new)
        wk = jnp.exp(wk - m_new)
        c_scr[si] = decay * c_scr[si] + jnp.dot((ch["vt"] * wk).astype(BF16), ch["k"],
                                                preferred_element_type=F32)
        wk8 = jnp.where(row8 == 0, wk, 0.0).astype(BF16)
        n_scr[si] = decay * n_scr[si] + jnp.dot(wk8, ch["k"], preferred_element_type=F32)
        m_scr[si] = jnp.broadcast_to(m_new, m_scr.shape[1:])


def _mlstm_call(qk, mlvt, gates, b_rows, r_cols):
    B, T, _ = qk.shape
    nt = T // TT

    def bwd_t(t):
        return jnp.where(t == 0, 0, nt - t)

    def spec(n, back):
        if back:
            return pl.BlockSpec((ML_NB, TT, n), lambda b, t: (b, bwd_t(t), 0))
        return pl.BlockSpec((ML_NB, TT, n), lambda b, t: (b, t, 0))

    def tspec(n, back):
        if back:
            return pl.BlockSpec((ML_NB, n, TT), lambda b, t: (b, 0, bwd_t(t)))
        return pl.BlockSpec((ML_NB, n, TT), lambda b, t: (b, 0, t))

    assert B % ML_NB == 0
    nch = ML_NB * 2 * ML_HEADS
    out_sds = jax.ShapeDtypeStruct((B, T, ML_W), F32)
    return pl.pallas_call(
        _mlstm_kernel,
        grid=(B // ML_NB, nt),
        in_specs=[
            spec(2 * ML_W, False), spec(2 * ML_W, True),
            tspec(ML_W, False), tspec(ML_W, True),
            tspec(GATE_W, False), tspec(GATE_W, True),
            tspec(GATE_W, False), tspec(GATE_W, True),
            spec(128, False), spec(128, True),
        ],
        out_specs=[spec(ML_W, False), spec(ML_W, True)],
        out_shape=[out_sds, out_sds],
        scratch_shapes=[
            pltpu.VMEM((nch, ML_DH, ML_DH), F32),
            pltpu.VMEM((nch, 8, ML_DH), F32),
            pltpu.VMEM((nch, 8, 128), F32),
        ],
        compiler_params=_cparams(("parallel", "arbitrary")),
        name="mlstm_scan",
    )(qk, qk, mlvt, mlvt, gates, gates, b_rows, b_rows, r_cols, r_cols)


def _diff_kernel(q_ref, k_ref, vx_ref, lam_ref, g_ref, o_ref, sa_scr, sb_scr, ma_scr, mb_scr,
                 acca_scr, accb_scr, *, lam_init, n_ctx):
    t = pl.program_id(2)
    nt = pl.num_programs(2) - 2
    nk = k_ref.shape[0]
    dn = (((1,), (1,)), ((), ()))
    all_chunks = [(c, min(c + DF_KCHUNK, nk)) for c in range(0, nk, DF_KCHUNK)]
    ctx_chunks = [ch for ch in all_chunks if ch[0] < n_ctx]
    slots = ((sa_scr, ma_scr), (sb_scr, mb_scr))
    accs = (acca_scr, accb_scr)

    def step(out_acc, finish_chunks, rd, acc_wr, score_chunks, wr):
        if out_acc is not None:
            lv = lam_ref[...]
            lam = (jnp.exp(jnp.sum(lv[0:1, :] * lv[1:2, :], axis=1, keepdims=True))
                   - jnp.exp(jnp.sum(lv[2:3, :] * lv[3:4, :], axis=1, keepdims=True)) + lam_init)
            outs = [out_acc[j, 0:DF_DV, :] / out_acc[j, DF_DV:DF_DV + 1, :] for j in range(2)]
            o = outs[0] - lam * outs[1]
            var = jnp.mean(o * o, axis=0, keepdims=True)
            on = o * lax.rsqrt(var + EPS) * g_ref[...] * (1.0 - lam_init)
            o_ref[...] = on.T.astype(o_ref.dtype)
        acc = [None, None]
        m_new = [None, None]
        if score_chunks:
            q = q_ref[...]
            lane = lax.broadcasted_iota(jnp.int32, q.shape, 1)
            zero = jnp.zeros(q.shape, q.dtype)
            qms = (jnp.where(lane < DF_DQK, q, zero), jnp.where(lane < DF_DQK, zero, q))
        if finish_chunks:
            m_old = [rd[1][j, 0:1, :] for j in range(2)]
        for i in range(max(len(finish_chunks), len(score_chunks))):
            for j in range(2):
                if i < len(finish_chunks):
                    c0, c1 = finish_chunks[i]
                    e = jnp.exp2(rd[0][j, c0:c1, :] - m_old[j]).astype(BF16)
                    part = jnp.dot(vx_ref[:, c0:c1], e, preferred_element_type=F32)
                    acc[j] = part if acc[j] is None else acc[j] + part
                if i < len(score_chunks):
                    c0, c1 = score_chunks[i]
                    s = lax.dot_general(k_ref[c0:c1, :], qms[j], dn, preferred_element_type=F32)
                    wr[0][j, c0:c1, :] = s
                    cm = jnp.max(s, axis=0, keepdims=True)
                    m_new[j] = cm if m_new[j] is None else jnp.maximum(m_new[j], cm)
        if score_chunks:
            for j in range(2):
                wr[1][j] = jnp.broadcast_to(m_new[j], wr[1].shape[1:])
        if finish_chunks:
            for j in range(2):
                acc_wr[j] = acc[j]

    @pl.when(t == 0)
    def _():
        step(None, [], None, None, ctx_chunks, slots[0])

    @pl.when(t == 1)
    def _():
        step(None, ctx_chunks, slots[0], accs[0], all_chunks, slots[1])

    for p in range(2):
        @pl.when((t % 2 == p) & (t >= 2) & (t < nt))
        def _():
            step(accs[p], all_chunks, slots[1 - p], accs[1 - p], all_chunks, slots[p])

        @pl.when((t % 2 == p) & (t == nt))
        def _():
            step(accs[p], all_chunks, slots[1 - p], accs[1 - p], [], None)

        @pl.when((t % 2 == p) & (t == nt + 1))
        def _():
            step(accs[p], [], None, None, [], None)


def _diff_call(dfq, dfk, dfvx, lam_vecs, norm_g, lam_init, n_ctx):
    B, T, _ = dfq.shape
    nt = T // TT
    assert n_ctx % DF_KCHUNK == 0 and nt >= 2
    return pl.pallas_call(
        functools.partial(_diff_kernel, lam_init=lam_init, n_ctx=n_ctx),
        grid=(B, DF_HEADS, nt + 2),
        in_specs=[
            pl.BlockSpec((None, TT, DF_DV), lambda b, h, t: (b, jnp.minimum(t, nt - 1), h)),
            pl.BlockSpec((None, T, DF_DV), lambda b, h, t: (b, 0, h)),
            pl.BlockSpec((None, VEXT_ROWS, T), lambda b, h, t: (b, h, 0)),
            pl.BlockSpec((4, DF_DQK), lambda b, h, t: (0, 0)),
            pl.BlockSpec((DF_DV, 1), lambda b, h, t: (h, 0)),
        ],
        out_specs=pl.BlockSpec((None, TT, DF_DV), lambda b, h, t: (b, jnp.maximum(t - 2, 0), h)),
        out_shape=jax.ShapeDtypeStruct((B, T, DF_W), BF16),
        scratch_shapes=[pltpu.VMEM((2, T, TT), F32), pltpu.VMEM((2, T, TT), F32),
                        pltpu.VMEM((2, 8, TT), F32), pltpu.VMEM((2, 8, TT), F32),
                        pltpu.VMEM((2, VEXT_ROWS, TT), F32), pltpu.VMEM((2, VEXT_ROWS, TT), F32)],
        compiler_params=_cparams(("parallel", "parallel", "arbitrary")),
        name="diff_attn",
    )(dfq, dfk, dfvx, lam_vecs, norm_g.reshape(DF_W, 1))


def _na_tables(rows):
    kh = min(NA_KH, rows)
    assert kh == NA_KH and rows >= NA_KROWS and rows % NA_QROWS == 0
    n_tiles = rows // NA_QROWS
    qc = np.arange(GRID_W)
    qstart = np.clip(qc - NA_KW // 2, 0, GRID_W - NA_KW)
    col_ok = (qc[None, :] >= qstart[:, None]) & (qc[None, :] < qstart[:, None] + NA_KW)
    drs, valids = [], []
    for tile in (0, 1, n_tiles - 1):
        r0 = tile * NA_QROWS
        start = int(np.clip(r0 - kh // 2, 0, rows - NA_KROWS))
        r = r0 + np.arange(NA_QROWS)
        rs = np.clip(r - kh // 2, 0, rows - kh)
        akr = start + np.arange(NA_KROWS)
        row_ok = (akr[None, :] >= rs[:, None]) & (akr[None, :] < rs[:, None] + kh)
        drs.append(np.clip(akr[None, :] - r[:, None] + NA_KH - 1, 0, 2 * NA_KH - 2))
        ok = row_ok[:, None, :, None] & col_ok[None, :, None, :]
        valids.append(ok.reshape(TT, NA_KROWS * GRID_W))
    return np.stack(drs), np.stack(valids)


def _na_bias_tables(rel_bias, rows):
    dr, valid = _na_tables(rows)
    L, H = rel_bias.shape[:2]
    pad = GRID_W - NA_KW
    flipped = jnp.flip(jnp.pad(rel_bias.astype(F32) * LOG2E, ((0, 0), (0, 0), (0, 0), (pad, pad)),
                               mode="edge"), axis=-1)
    off = flipped.shape[-1] - 1 - (NA_KW - 1 + pad)
    by_col = jnp.stack([flipped[..., off - kc:off - kc + GRID_W] for kc in range(GRID_W)], axis=-2)
    tabs = []
    for ty in range(3):
        per_kr = [jnp.stack([by_col[:, :, int(dr[ty][qr, kr])] for qr in range(NA_QROWS)], axis=3)
                  for kr in range(NA_KROWS)]
        tab = jnp.stack(per_kr, axis=2).reshape(L, H, NA_KROWS * GRID_W, TT)
        tabs.append(jnp.where(valid[ty].T, tab, NEG))
    return jnp.stack(tabs, axis=1)


def _na_kernel(q_ref, k_ref, vx_ref, bias_ref, o_ref, *, n_ctx, rows):
    t = pl.program_id(1)
    dn = (((1,), (1,)), ((), ()))
    tile_type = jnp.where(t <= 1, 0, jnp.where(t == pl.num_programs(1) - 1, 2, 1))

    def run(key_slices):
        def scores(hd):
            sl = slice(hd * NA_DH, (hd + 1) * NA_DH)
            q = q_ref[:, sl]
            parts = []
            for ks, add_bias in key_slices:
                s = lax.dot_general(k_ref[ks, sl], q, dn, preferred_element_type=F32)
                parts.append(s + bias_ref[tile_type, hd] if add_bias else s)
            m = jnp.max(parts[0], axis=0, keepdims=True)
            for p in parts[1:]:
                m = jnp.maximum(m, jnp.max(p, axis=0, keepdims=True))
            return parts, m

        def finish(hd, parts, m):
            o = None
            for (ks, _), s in zip(key_slices, parts):
                e = jnp.exp2(s - m).astype(BF16)
                part = jnp.dot(vx_ref[hd * NA_VROWS:(hd + 1) * NA_VROWS, ks], e,
                               preferred_element_type=F32)
                o = part if o is None else o + part
            return o[0:NA_DH, :] / o[NA_DH:NA_DH + 1, :]

        prev = None
        done = []
        for hd in range(NA_HEADS + 1):
            cur = scores(hd) if hd < NA_HEADS else None
            if prev is not None:
                done.append(finish(hd - 1, *prev))
                if len(done) == 2:
                    c0 = (hd - 2) * NA_DH
                    o_ref[:, c0:c0 + 2 * NA_DH] = jnp.concatenate(done, axis=0).T.astype(o_ref.dtype)
                    done = []
            prev = cur

    @pl.when(t == 0)
    def _():
        run([(slice(0, n_ctx), False)])

    @pl.when(t != 0)
    def _():
        r0 = (t - 1) * NA_QROWS
        start = jnp.clip(r0 - NA_KH // 2, 0, rows - NA_KROWS)
        tok0 = pl.multiple_of(n_ctx + start * GRID_W, TT)
        run([(pl.ds(tok0, NA_KROWS * GRID_W), True), (slice(0, n_ctx), False)])


def _na_call(naq, nak, navx, bias_tab, n_ctx, l):
    B, T, _ = naq.shape
    nt = T // TT
    rows = (T - n_ctx) // GRID_W
    nk = NA_KROWS * GRID_W
    assert n_ctx % TT == 0 and (NA_KH // 2) % NA_QROWS == 0 and (rows - NA_KROWS) % NA_QROWS == 0

    return pl.pallas_call(
        functools.partial(_na_kernel, n_ctx=n_ctx, rows=rows),
        grid=(B, nt),
        in_specs=[
            pl.BlockSpec((None, TT, NA_W), lambda b, t: (b, t, 0)),
            pl.BlockSpec((None, T, NA_W), lambda b, t: (b, 0, 0)),
            pl.BlockSpec((None, NA_HEADS * NA_VROWS, T), lambda b, t: (b, 0, 0)),
            pl.BlockSpec((None, 3, NA_HEADS, nk, TT), lambda b, t: (l, 0, 0, 0, 0),
                         pipeline_mode=pl.Buffered(1)),
        ],
        out_specs=pl.BlockSpec((None, TT, NA_W), lambda b, t: (b, t, 0)),
        out_shape=jax.ShapeDtypeStruct((B, T, NA_W), BF16),
        compiler_params=_cparams(("parallel", "arbitrary")),
        name="na_attn",
    )(naq, nak, navx, bias_tab)


def _merge_tile(x_ref, mod_ref, g_ref, wg_ref, wb_ref, wo_ref, hf_ref, hb_ref, o_ref_in,
                  mlg_ref, bd_ref, cn_ref, out_ref):
    x = x_ref[...]
    D = x.shape[1]
    hn = _norm_mod(x, g_ref[...], mod_ref[3:4, :], mod_ref[4:5, :]).astype(BF16)
    hsum = hf_ref[...] + hb_ref[...]
    parts = []
    for hd in range(ML_HEADS):
        hh = hsum[:, hd * ML_DH:(hd + 1) * ML_DH]
        parts.append(hh * lax.rsqrt(jnp.mean(hh * hh, axis=1, keepdims=True) + EPS))
    a = (_sigmoid(o_ref_in[...]) * (jnp.concatenate(parts, axis=1) * mlg_ref[...])).astype(BF16)
    y = None
    for i, br in enumerate((a, bd_ref[...], cn_ref[...])):
        gate = _sigmoid(jnp.dot(hn, wg_ref[:, i * D:(i + 1) * D], preferred_element_type=F32))
        term = gate * jnp.dot(br, wb_ref[i], preferred_element_type=F32)
        y = term if y is None else y + term
    z = jnp.dot(y.astype(BF16), wo_ref[...], preferred_element_type=F32)
    out_ref[...] = x + mod_ref[5:6, :] * z


def _merge_call(x, modtab, g, w_gate, w_branch, w_out, hf, hb, mlf, ml_norm_g, bd, cn, l):
    B, T, D = x.shape

    def tile_spec(n, cblk=0):
        return pl.BlockSpec((DENSE_NB, TT, n), lambda b, t: (b, t, cblk))

    return pl.pallas_call(
        _per_batch_element(_merge_tile, n_in=12, shared=(2, 3, 4, 5, 9)),
        grid=(B // DENSE_NB, T // TT),
        in_specs=[
            tile_spec(D),
            _mod_spec(DENSE_NB),
            pl.BlockSpec((1, D), lambda b, t: (0, 0)),
            pl.BlockSpec((None, D, N_BRANCH * D), lambda b, t: (l, 0, 0)),
            pl.BlockSpec((None, N_BRANCH, BRANCH_W, D), lambda b, t: (l, 0, 0, 0)),
            pl.BlockSpec((None, D, D), lambda b, t: (l, 0, 0)),
            tile_spec(ML_W), tile_spec(ML_W),
            tile_spec(ML_W, 2),
            pl.BlockSpec((1, ML_W), lambda b, t: (0, 0)),
            tile_spec(DF_W), tile_spec(NA_W),
        ],
        out_specs=tile_spec(D),
        out_shape=jax.ShapeDtypeStruct(x.shape, F32),
        compiler_params=_cparams(("parallel", "parallel")),
        name="merge",
    )(x, modtab, g, w_gate, w_branch, w_out, hf, hb, mlf, ml_norm_g, bd, cn)


def _rope_tables(S, n_ctx):
    t = jnp.arange(S)
    rowp = (t // GRID_W).astype(F32)
    colp = (t % GRID_W).astype(F32)
    half = DF_DQK // 2
    freqs = ROPE_BASE ** (-jnp.arange(0, half, 2, dtype=F32) / half)
    ang = jnp.concatenate([rowp[:, None] * freqs, colp[:, None] * freqs], axis=-1)
    cos, sin = jnp.cos(ang), jnp.sin(ang)
    cos = jnp.concatenate([jnp.ones((n_ctx, half), F32), cos], axis=0)
    sin = jnp.concatenate([jnp.zeros((n_ctx, half), F32), sin], axis=0)
    cos_t = jnp.tile(cos, (1, 4))
    sin_t = jnp.tile(jnp.concatenate([-sin, sin], axis=1), (1, 2))
    return cos_t, sin_t


def kernel(x, c, ctx, c_ctx, w_ada, b_ada, norm_g, ffn_w1, ffn_w2, w_in, ml_conv_w, ml_conv_b,
           ml_gate_b, ml_norm_g, df_lambda, df_norm_g, na_rel_bias, w_branch, w_out, final_g):
    B, S, D = x.shape
    n_ctx = ctx.shape[1]
    depth = w_ada.shape[0]
    assert D == D_MODEL and n_ctx == TT and S % TT == 0 and S % GRID_W == 0
    T = n_ctx + S
    rows = S // GRID_W

    w1 = ffn_w1.astype(BF16)
    w2 = ffn_w2.astype(BF16)
    w_tok, w_tr, w_gate = _split_w_in(w_in)
    wb = w_branch.astype(BF16)
    wo = w_out.astype(BF16)

    cvec = jnp.zeros((16, D), F32).at[0].set(c_ctx).at[1:B + 1].set(c)
    mod = _ada_call(cvec, w_ada, b_ada).reshape(depth, 16, 9, D)
    modtab = jnp.stack([jnp.broadcast_to(mod[:, 0:1], (depth, B, 9, D)), mod[:, 1:B + 1]], axis=2)

    cos_t, sin_t = _rope_tables(S, n_ctx)
    bias_tab = _na_bias_tables(na_rel_bias, rows)

    h = x
    for l in range(depth):
        lam_init = 0.8 - 0.6 * math.exp(-0.3 * l)
        mt = modtab[l]
        h = _ffn_call(h, mt, norm_g[l, 0][None], w1, w2, l, 0, 0,
                      ctx_in=ctx if l == 0 else None, n_ctx=n_ctx)
        mlf, mlvt, mlgt, dfq, dfk, dfvx, naq, nak, navx = _inproj_call(
            h, mt, norm_g[l, 1][None], w_tok, w_tr, cos_t, sin_t, l)
        qk = _conv_call(mlf, ml_conv_w[l], ml_conv_b[l][None], n_ctx)
        hf, hb = _mlstm_call(qk, mlvt, *_gate_prep_call(mlgt, ml_gate_b[l]))
        bd = _diff_call(dfq, dfk, dfvx, df_lambda[l], df_norm_g[l], lam_init, n_ctx)
        cn = _na_call(naq, nak, navx, bias_tab, n_ctx, l)
        h = _merge_call(h, mt, norm_g[l, 1][None], w_gate, wb, wo, hf, hb, mlf,
                        ml_norm_g[l][None], bd, cn, l)
        last = l == depth - 1
        h = _ffn_call(h, mt, norm_g[l, 2][None], w1, w2, l, 1, 6,
                      final_g=final_g[None] if last else None, n_ctx=n_ctx)
    return h
```

```python
import functools
import math

import numpy as np
import jax
import jax.numpy as jnp
from jax import lax
from jax.experimental import pallas as pl
from jax.experimental.pallas import tpu as pltpu

F32 = jnp.float32
BF16 = jnp.bfloat16

D_MODEL = 1024
GRID_W = 64
N_BRANCH = 3
BRANCH_W = 512
ML_HEADS = 4
ML_DH = 128
ML_W = ML_HEADS * ML_DH
DF_HEADS = 4
DF_DQK = 64
DF_DV = 128
DF_W = DF_HEADS * DF_DV
ROPE_BASE = 10000.0
NA_HEADS = 8
NA_DH = 64
NA_W = NA_HEADS * NA_DH
NA_KH = 8
NA_KW = 16
D_FF = 2816
EPS = 1e-6
GATE_W = 4 * ML_HEADS

DENSE_NB = 2
ML_NB = 2
TT = 256
NA_QROWS = 4
NA_KROWS = 12
MXU_N = 256
FF_CHUNKS = ((0, 6 * MXU_N), (6 * MXU_N, D_FF))
DF_KCHUNK = MXU_N
ONES_ROWS = 16
VEXT_ROWS = DF_DV + ONES_ROWS
NA_VROWS = NA_DH + ONES_ROWS
NEG = -1e30
LOG2E = math.log2(math.e)
VMEM_LIMIT = 56 * 1024 * 1024


def _cparams(sem):
    return pltpu.CompilerParams(dimension_semantics=sem, vmem_limit_bytes=VMEM_LIMIT)


def _norm_mod(x, g, shift, scale):
    var = jnp.mean(x * x, axis=-1, keepdims=True)
    return x * lax.rsqrt(var + EPS) * g * (1.0 + scale) + shift


def _sigmoid(x):
    return 1.0 / (1.0 + jnp.exp(-x))


def _ada_kernel(c_ref, w_ref, b_ref, o_ref):
    c = c_ref[...]
    h = (c * _sigmoid(c)).astype(BF16)
    o_ref[...] = jnp.dot(h, w_ref[...].astype(BF16), preferred_element_type=F32) + b_ref[...]


def _ada_call(cvec, w_ada, b_ada):
    L, D, N = w_ada.shape
    tn = N // 8
    return pl.pallas_call(
        _ada_kernel,
        grid=(L, N // tn),
        in_specs=[
            pl.BlockSpec(cvec.shape, lambda l, j: (0, 0)),
            pl.BlockSpec((None, D, tn), lambda l, j: (l, 0, j)),
            pl.BlockSpec((None, 1, tn), lambda l, j: (l, 0, j)),
        ],
        out_specs=pl.BlockSpec((None, cvec.shape[0], tn), lambda l, j: (l, 0, j)),
        out_shape=jax.ShapeDtypeStruct((L, cvec.shape[0], N), F32),
        compiler_params=_cparams(("parallel", "parallel")),
        name="adaln",
    )(cvec, w_ada, b_ada.reshape(L, 1, N))


def _ffn_kernel(x_ref, mod_ref, g_ref, w1_ref, w2_ref, *rest, k0, final, split_in):
    o_ref = rest[-1]
    for i in range(DENSE_NB):
        x = x_ref[i]
        if split_in:
            x = jnp.where(pl.program_id(1) == 0, rest[0][i], x)
        h = _norm_mod(x, g_ref[...], mod_ref[i, k0:k0 + 1, :], mod_ref[i, k0 + 1:k0 + 2, :]).astype(BF16)
        acc = jnp.zeros(x.shape, F32)
        for c0, c1 in FF_CHUNKS:
            a = jnp.dot(h, w1_ref[:, c0:c1], preferred_element_type=F32)
            b = jnp.dot(h, w1_ref[:, D_FF + c0:D_FF + c1], preferred_element_type=F32)
            u = (a * _sigmoid(a) * b).astype(BF16)
            acc = acc + jnp.dot(u, w2_ref[c0:c1, :], preferred_element_type=F32)
        y = x + 0.5 * mod_ref[i, k0 + 2:k0 + 3, :] * acc
        if final:
            y = y * lax.rsqrt(jnp.mean(y * y, axis=-1, keepdims=True) + EPS) * rest[-2][...]
        o_ref[i] = y


def _mod_spec(nb=None):
    return pl.BlockSpec((nb, None, 9, D_MODEL), lambda b, t: (b, jnp.minimum(t, 1), 0, 0))


def _ffn_call(x, modtab, g, w1, w2, l, i, k0, final_g=None, ctx_in=None, n_ctx=0):
    B, _, D = x.shape
    T = x.shape[1] + (n_ctx if ctx_in is not None else 0)
    final = final_g is not None
    off = n_ctx // TT
    if ctx_in is not None:
        x_spec = pl.BlockSpec((DENSE_NB, TT, D), lambda b, t: (b, jnp.maximum(t - off, 0), 0))
    else:
        x_spec = pl.BlockSpec((DENSE_NB, TT, D), lambda b, t: (b, t, 0))
    assert B % DENSE_NB == 0
    in_specs = [
        x_spec,
        _mod_spec(DENSE_NB),
        pl.BlockSpec((1, D), lambda b, t: (0, 0)),
        pl.BlockSpec((None, None, D, 2 * D_FF), lambda b, t: (l, i, 0, 0)),
        pl.BlockSpec((None, None, D_FF, D), lambda b, t: (l, i, 0, 0)),
    ]
    args = [x, modtab, g, w1, w2]
    if ctx_in is not None:
        assert n_ctx == TT
        in_specs.append(pl.BlockSpec((DENSE_NB, TT, D), lambda b, t: (b, 0, 0)))
        args.append(ctx_in)
    if final:
        in_specs.append(pl.BlockSpec((1, D), lambda b, t: (0, 0)))
        args.append(final_g)
        out_spec = pl.BlockSpec((DENSE_NB, TT, D), lambda b, t: (b, jnp.maximum(t - off, 0), 0))
        out_shape = jax.ShapeDtypeStruct((B, T - n_ctx, D), F32)
    else:
        out_spec = pl.BlockSpec((DENSE_NB, TT, D), lambda b, t: (b, t, 0))
        out_shape = jax.ShapeDtypeStruct((B, T, D), F32)
    return pl.pallas_call(
        functools.partial(_ffn_kernel, k0=k0, final=final, split_in=ctx_in is not None),
        grid=(B // DENSE_NB, T // TT),
        in_specs=in_specs,
        out_specs=out_spec,
        out_shape=out_shape,
        compiler_params=_cparams(("parallel", "arbitrary")),
        name="ffn_half",
    )(*args)


def _per_batch_element(tile_fn, n_in, shared):
    def kern(*refs):
        for i in range(DENSE_NB):
            tile_fn(*[r if (j < n_in and j in shared) else r.at[i] for j, r in enumerate(refs)])
    return kern


def _inproj_tile(x_ref, mod_ref, g_ref, w_ref, wt_ref, cos_ref, sin_ref,
                 mlf_ref, mlvt_ref, mlgt_ref, dfq_ref, dfk_ref, dfvx_ref, naq_ref, nak_ref, navx_ref):
    x = x_ref[...]
    h = _norm_mod(x, g_ref[...], mod_ref[3:4, :], mod_ref[4:5, :]).astype(BF16)

    def proj(c0, n):
        return jnp.dot(h, w_ref[:, c0:c0 + n], preferred_element_type=F32)

    def proj_t(r0, n):
        return lax.dot_general(wt_ref[r0:r0 + n, :], h, (((1,), (1,)), ((), ())),
                               preferred_element_type=F32)

    mlf_ref[...] = proj(0, 3 * ML_W)
    mlvt_ref[...] = proj_t(0, ML_W).astype(BF16)
    mlgt_ref[...] = proj_t(ML_W + DF_W, GATE_W)
    ones_rows = jnp.where(lax.broadcasted_iota(jnp.int32, (ONES_ROWS, TT), 0) == 0, 1.0, 0.0).astype(BF16)
    for vx_ref, r0, heads, dh in ((dfvx_ref, ML_W, DF_HEADS, DF_DV),
                                  (navx_ref, ML_W + DF_W + GATE_W, NA_HEADS, NA_DH)):
        vt = proj_t(r0, heads * dh).astype(BF16)
        for hd in range(heads):
            r = hd * (dh + ONES_ROWS)
            vx_ref[r:r + dh, :] = vt[hd * dh:(hd + 1) * dh, :]
            vx_ref[r + dh:r + dh + ONES_ROWS, :] = ones_rows
    cos = jnp.tile(cos_ref[...], (1, DF_HEADS))
    sin = jnp.tile(sin_ref[...], (1, DF_HEADS))
    lane = lax.broadcasted_iota(jnp.int32, (TT, DF_W), 1)
    first_half = (lane % DF_DQK) < (DF_DQK // 2)

    def rope(v):
        partner = jnp.where(first_half, pltpu.roll(v, DF_W - DF_DQK // 2, 1),
                            pltpu.roll(v, DF_DQK // 2, 1))
        return v * cos + partner * sin

    c0 = 3 * ML_W
    dfq_ref[...] = (rope(proj(c0, DF_W)) * (DF_DQK ** -0.5 * LOG2E)).astype(BF16)
    dfk_ref[...] = rope(proj(c0 + DF_W, DF_W)).astype(BF16)
    c0 = c0 + 2 * DF_W
    naq_ref[...] = (proj(c0, NA_W) * (NA_DH ** -0.5 * LOG2E)).astype(BF16)
    nak_ref[...] = proj(c0 + NA_W, NA_W).astype(BF16)


def _split_w_in(w_in):
    o_mlv, o_mlo, o_mlg = 2 * ML_W, 3 * ML_W, 4 * ML_W
    o_dfq = o_mlg + GATE_W
    o_dfv = o_dfq + 2 * DF_W
    o_na = o_dfv + DF_W
    o_nav = o_na + 2 * NA_W
    o_gate = o_na + 3 * NA_W
    w_tok = jnp.concatenate([w_in[:, :, 0:o_mlv], w_in[:, :, o_mlo:o_mlg], w_in[:, :, o_dfq:o_dfv],
                             w_in[:, :, o_na:o_nav]], axis=-1)
    w_tr = jnp.concatenate([w_in[:, :, o_mlv:o_mlo], w_in[:, :, o_dfv:o_na], w_in[:, :, o_mlg:o_dfq],
                            w_in[:, :, o_nav:o_gate]], axis=-1)
    return (w_tok.astype(BF16), jnp.swapaxes(w_tr, 1, 2).astype(BF16), w_in[:, :, o_gate:].astype(BF16))


def _inproj_call(x, modtab, g, w_tok, w_tr, cos_t, sin_t, l):
    B, T, D = x.shape

    def tile_spec(n):
        return pl.BlockSpec((DENSE_NB, TT, n), lambda b, t: (b, t, 0))

    def sds(n, dt):
        return jax.ShapeDtypeStruct((B, T, n), dt)

    def tspec(n):
        return pl.BlockSpec((DENSE_NB, n, TT), lambda b, t: (b, 0, t))

    def tsds(n, dt):
        return jax.ShapeDtypeStruct((B, n, T), dt)

    return pl.pallas_call(
        _per_batch_element(_inproj_tile, n_in=7, shared=(2, 3, 4, 5, 6)),
        grid=(B // DENSE_NB, T // TT),
        in_specs=[
            tile_spec(D),
            _mod_spec(DENSE_NB),
            pl.BlockSpec((1, D), lambda b, t: (0, 0)),
            pl.BlockSpec((None, D, w_tok.shape[-1]), lambda b, t: (l, 0, 0)),
            pl.BlockSpec((None, w_tr.shape[1], D), lambda b, t: (l, 0, 0)),
            pl.BlockSpec((TT, 2 * DF_DQK), lambda b, t: (t, 0)),
            pl.BlockSpec((TT, 2 * DF_DQK), lambda b, t: (t, 0)),
        ],
        out_specs=[tile_spec(3 * ML_W), tspec(ML_W), tspec(GATE_W),
                   tile_spec(DF_W), tile_spec(DF_W), tspec(DF_HEADS * VEXT_ROWS),
                   tile_spec(NA_W), tile_spec(NA_W), tspec(NA_HEADS * NA_VROWS)],
        out_shape=[sds(3 * ML_W, F32), tsds(ML_W, BF16), tsds(GATE_W, F32),
                   sds(DF_W, BF16), sds(DF_W, BF16), tsds(DF_HEADS * VEXT_ROWS, BF16),
                   sds(NA_W, BF16), sds(NA_W, BF16), tsds(NA_HEADS * NA_VROWS, BF16)],
        compiler_params=_cparams(("parallel", "parallel")),
        name="in_proj",
    )(x, modtab, g, w_tok, w_tr, cos_t, sin_t)


def _conv_kernel(x_ref, w_ref, b_ref, o_ref, *, n_ctx):
    x = x_ref[...]
    T = x.shape[0]
    row = lax.broadcasted_iota(jnp.int32, x.shape, 0)
    prev = jnp.where((row == 0) | (row == n_ctx), 0.0, pltpu.roll(x, 1, 0))
    nxt = jnp.where((row == n_ctx - 1) | (row == T - 1), 0.0, pltpu.roll(x, T - 1, 0))
    y = prev * w_ref[0:1, :] + x * w_ref[1:2, :] + nxt * w_ref[2:3, :] + b_ref[...]
    y = y * _sigmoid(y)
    scale = jnp.where(pl.program_id(1) < ML_W // x.shape[1], ML_DH ** -0.5, 1.0)
    o_ref[...] = (y * scale).astype(BF16)


def _conv_call(mlf, conv_w, conv_b, n_ctx):
    B, T, _ = mlf.shape
    ct = 256
    return pl.pallas_call(
        functools.partial(_conv_kernel, n_ctx=n_ctx),
        grid=(B, 2 * ML_W // ct),
        in_specs=[
            pl.BlockSpec((None, T, ct), lambda b, c: (b, 0, c)),
            pl.BlockSpec((3, ct), lambda b, c: (0, c)),
            pl.BlockSpec((1, ct), lambda b, c: (0, c)),
        ],
        out_specs=pl.BlockSpec((None, T, ct), lambda b, c: (b, 0, c)),
        out_shape=jax.ShapeDtypeStruct((B, T, 2 * ML_W), BF16),
        compiler_params=_cparams(("parallel", "parallel")),
        name="ml_conv",
    )(mlf, conv_w, conv_b)


def _log_sigmoid(x):
    return jnp.minimum(x, 0.0) - jnp.log1p(jnp.exp(-jnp.abs(x)))


def _segment_scan(x, seg, reverse):
    n = x.shape[1]
    pos = lax.broadcasted_iota(jnp.int32, x.shape, 1) % seg
    k = 1
    while k < seg:
        if reverse:
            x = x + jnp.where(pos < seg - k, pltpu.roll(x, n - k, 1), 0.0)
        else:
            x = x + jnp.where(pos >= k, pltpu.roll(x, k, 1), 0.0)
        k *= 2
    return x


def _gate_prep_kernel(gt_ref, gbias_ref, g_ref, b_ref, r_ref):
    gt = gt_ref[...] + gbias_ref[...]
    lf = _log_sigmoid(gt)
    row = lax.broadcasted_iota(jnp.int32, gt.shape, 0)
    b_rows = jnp.where(row < GATE_W // 2, _segment_scan(lf, TT, False), _segment_scan(lf, TT, True))
    r_rows = gt - pltpu.roll(b_rows, GATE_W - ML_HEADS, 0)
    g_ref[...] = gt
    b_ref[...] = b_rows
    r_ref[...] = jnp.concatenate([r_rows, jnp.zeros((128 - GATE_W, gt.shape[1]), F32)], axis=0).T


def _gate_prep_call(mlgt, gate_b):
    B, _, T = mlgt.shape
    row_spec = pl.BlockSpec((None, GATE_W, T), lambda b: (b, 0, 0))
    row_sds = jax.ShapeDtypeStruct((B, GATE_W, T), F32)
    return pl.pallas_call(
        _gate_prep_kernel,
        grid=(B,),
        in_specs=[row_spec, pl.BlockSpec((GATE_W, 1), lambda b: (0, 0))],
        out_specs=[row_spec, row_spec, pl.BlockSpec((None, T, 128), lambda b: (b, 0, 0))],
        out_shape=[row_sds, row_sds, jax.ShapeDtypeStruct((B, T, 128), F32)],
        compiler_params=_cparams(("parallel",)),
        name="ml_gates",
    )(mlgt, gate_b.reshape(GATE_W, 1))


def _mlstm_kernel(qkf_ref, qkb_ref, vtf_ref, vtb_ref, gf_ref, gb_ref, bf_ref, bb_ref, rf_ref, rb_ref,
                  hf_ref, hb_ref, c_scr, n_scr, m_scr):
    L = TT

    @pl.when(pl.program_id(1) == 0)
    def _():
        c_scr[...] = jnp.zeros(c_scr.shape, F32)
        n_scr[...] = jnp.zeros(n_scr.shape, F32)
        m_scr[...] = jnp.zeros(m_scr.shape, F32)

    key = lax.broadcasted_iota(jnp.int32, (L, L), 0)
    qry = lax.broadcasted_iota(jnp.int32, (L, L), 1)
    row8 = lax.broadcasted_iota(jnp.int32, (8, L), 0)
    nt_dims = (((1,), (1,)), ((), ()))

    chains = []
    dirs = [(i, d, refs)
            for i in range(ML_NB)
            for d, refs in enumerate(((qkf_ref, vtf_ref, gf_ref, bf_ref, rf_ref, hf_ref),
                                      (qkb_ref, vtb_ref, gb_ref, bb_ref, rb_ref, hb_ref)))]
    for i, d, refs in dirs:
        qk_ref, vt_ref, g_ref, b_ref, r_ref, h_ref = [r.at[i] for r in refs]
        gt = g_ref[...]
        b_rows = b_ref[...]
        r_cols = r_ref[...]
        mask = (key <= qry) if d == 0 else (key >= qry)
        for hd in range(ML_HEADS):
            ci = d * 2 * ML_HEADS + hd
            b_row = b_rows[ci + ML_HEADS:ci + ML_HEADS + 1, :]
            chains.append(dict(
                si=(i * 2 + d) * ML_HEADS + hd, hd=hd, h_ref=h_ref, mask=mask,
                q=qk_ref[:, hd * ML_DH:(hd + 1) * ML_DH],
                k=qk_ref[:, ML_W + hd * ML_DH:ML_W + (hd + 1) * ML_DH],
                vt=vt_ref[hd * ML_DH:(hd + 1) * ML_DH, :],
                li_row=gt[ci:ci + 1, :], b_row=b_row, r_col=r_cols[:, ci:ci + 1],
                btot=b_row[:, L - 1:L] if d == 0 else b_row[:, 0:1]))

    for ch in chains:
        m = m_scr[ch["si"], 0:1, 0:1]
        dmat = jnp.where(ch["mask"], ch["r_col"] + ch["b_row"], -jnp.inf)
        inter = ch["b_row"] + m
        m_t = jnp.maximum(inter, jnp.max(dmat, axis=0, keepdims=True))
        ch.update(m=m, m_t=m_t, w=jnp.exp(dmat - m_t), s_inter=jnp.exp(inter - m_t),
                  s=lax.dot_general(ch["k"], ch["q"], nt_dims, preferred_element_type=F32))

    for ch in chains:
        si, hd, q = ch["si"], ch["hd"], ch["q"]
        cmat = c_scr[si]
        n8 = n_scr[si]
        qkw = ch["s"] * ch["w"]
        num = (jnp.dot(ch["vt"].astype(BF16), qkw.astype(BF16), preferred_element_type=F32)
               + ch["s_inter"] * lax.dot_general(cmat.astype(BF16), q, nt_dims,
                                                 preferred_element_type=F32))
        nq = lax.dot_general(n8.astype(BF16), q, nt_dims, preferred_element_type=F32)[0:1, :]
        den = jnp.sum(qkw, axis=0, keepdims=True) + ch["s_inter"] * nq
        ht = num / jnp.maximum(jnp.abs(den), jnp.exp(-ch["m_t"]))
        ch["h_ref"][:, hd * ML_DH:(hd + 1) * ML_DH] = ht.T

    for ch in chains:
        si, m, btot = ch["si"], ch["m"], ch["btot"]
        wk = btot - ch["b_row"] + ch["li_row"]
        m_new = jnp.maximum(btot + m, jnp.max(wk, axis=1, keepdims=True))
        decay = jnp.exp(btot + m - m_new)
        wk = jnp.exp(wk - m_new)
        c_scr[si] = decay * c_scr[si] + jnp.dot((ch["vt"] * wk).astype(BF16), ch["k"],
                                                preferred_element_type=F32)
        wk8 = jnp.where(row8 == 0, wk, 0.0).astype(BF16)
        n_scr[si] = decay * n_scr[si] + jnp.dot(wk8, ch["k"], preferred_element_type=F32)
        m_scr[si] = jnp.broadcast_to(m_new, m_scr.shape[1:])


def _mlstm_call(qk, mlvt, gates, b_rows, r_cols):
    B, T, _ = qk.shape
    nt = T // TT

    def bwd_t(t):
        return jnp.where(t == 0, 0, nt - t)

    def spec(n, back):
        if back:
            return pl.BlockSpec((ML_NB, TT, n), lambda b, t: (b, bwd_t(t), 0))
        return pl.BlockSpec((ML_NB, TT, n), lambda b, t: (b, t, 0))

    def tspec(n, back):
        if back:
            return pl.BlockSpec((ML_NB, n, TT), lambda b, t: (b, 0, bwd_t(t)))
        return pl.BlockSpec((ML_NB, n, TT), lambda b, t: (b, 0, t))

    assert B % ML_NB == 0
    nch = ML_NB * 2 * ML_HEADS
    out_sds = jax.ShapeDtypeStruct((B, T, ML_W), F32)
    return pl.pallas_call(
        _mlstm_kernel,
        grid=(B // ML_NB, nt),
        in_specs=[
            spec(2 * ML_W, False), spec(2 * ML_W, True),
            tspec(ML_W, False), tspec(ML_W, True),
            tspec(GATE_W, False), tspec(GATE_W, True),
            tspec(GATE_W, False), tspec(GATE_W, True),
            spec(128, False), spec(128, True),
        ],
        out_specs=[spec(ML_W, False), spec(ML_W, True)],
        out_shape=[out_sds, out_sds],
        scratch_shapes=[
            pltpu.VMEM((nch, ML_DH, ML_DH), F32),
            pltpu.VMEM((nch, 8, ML_DH), F32),
            pltpu.VMEM((nch, 8, 128), F32),
        ],
        compiler_params=_cparams(("parallel", "arbitrary")),
        name="mlstm_scan",
    )(qk, qk, mlvt, mlvt, gates, gates, b_rows, b_rows, r_cols, r_cols)


def _diff_kernel(q_ref, k_ref, vx_ref, lam_ref, g_ref, o_ref, sa_scr, sb_scr, ma_scr, mb_scr,
                 acca_scr, accb_scr, *, lam_init, n_ctx):
    t = pl.program_id(2)
    nt = pl.num_programs(2) - 2
    nk = k_ref.shape[0]
    dn = (((1,), (1,)), ((), ()))
    all_chunks = [(c, min(c + DF_KCHUNK, nk)) for c in range(0, nk, DF_KCHUNK)]
    ctx_chunks = [ch for ch in all_chunks if ch[0] < n_ctx]
    slots = ((sa_scr, ma_scr), (sb_scr, mb_scr))
    accs = (acca_scr, accb_scr)

    def step(out_acc, finish_chunks, rd, acc_wr, score_chunks, wr):
        if out_acc is not None:
            lv = lam_ref[...]
            lam = (jnp.exp(jnp.sum(lv[0:1, :] * lv[1:2, :], axis=1, keepdims=True))
                   - jnp.exp(jnp.sum(lv[2:3, :] * lv[3:4, :], axis=1, keepdims=True)) + lam_init)
            outs = [out_acc[j, 0:DF_DV, :] / out_acc[j, DF_DV:DF_DV + 1, :] for j in range(2)]
            o = outs[0] - lam * outs[1]
            var = jnp.mean(o * o, axis=0, keepdims=True)
            on = o * lax.rsqrt(var + EPS) * g_ref[...] * (1.0 - lam_init)
            o_ref[...] = on.T.astype(o_ref.dtype)
        acc = [None, None]
        m_new = [None, None]
        if score_chunks:
            q = q_ref[...]
            lane = lax.broadcasted_iota(jnp.int32, q.shape, 1)
            zero = jnp.zeros(q.shape, q.dtype)
            qms = (jnp.where(lane < DF_DQK, q, zero), jnp.where(lane < DF_DQK, zero, q))
        if finish_chunks:
            m_old = [rd[1][j, 0:1, :] for j in range(2)]
        for i in range(max(len(finish_chunks), len(score_chunks))):
            for j in range(2):
                if i < len(finish_chunks):
                    c0, c1 = finish_chunks[i]
                    e = jnp.exp2(rd[0][j, c0:c1, :] - m_old[j]).astype(BF16)
                    part = jnp.dot(vx_ref[:, c0:c1], e, preferred_element_type=F32)
                    acc[j] = part if acc[j] is None else acc[j] + part
                if i < len(score_chunks):
                    c0, c1 = score_chunks[i]
                    s = lax.dot_general(k_ref[c0:c1, :], qms[j], dn, preferred_element_type=F32)
                    wr[0][j, c0:c1, :] = s
                    cm = jnp.max(s, axis=0, keepdims=True)
                    m_new[j] = cm if m_new[j] is None else jnp.maximum(m_new[j], cm)
        if score_chunks:
            for j in range(2):
                wr[1][j] = jnp.broadcast_to(m_new[j], wr[1].shape[1:])
        if finish_chunks:
            for j in range(2):
                acc_wr[j] = acc[j]

    @pl.when(t == 0)
    def _():
        step(None, [], None, None, ctx_chunks, slots[0])

    @pl.when(t == 1)
    def _():
        step(None, ctx_chunks, slots[0], accs[0], all_chunks, slots[1])

    for p in range(2):
        @pl.when((t % 2 == p) & (t >= 2) & (t < nt))
        def _():
            step(accs[p], all_chunks, slots[1 - p], accs[1 - p], all_chunks, slots[p])

        @pl.when((t % 2 == p) & (t == nt))
        def _():
            step(accs[p], all_chunks, slots[1 - p], accs[1 - p], [], None)

        @pl.when((t % 2 == p) & (t == nt + 1))
        def _():
            step(accs[p], [], None, None, [], None)


def _diff_call(dfq, dfk, dfvx, lam_vecs, norm_g, lam_init, n_ctx):
    B, T, _ = dfq.shape
    nt = T // TT
    assert n_ctx % DF_KCHUNK == 0 and nt >= 2
    return pl.pallas_call(
        functools.partial(_diff_kernel, lam_init=lam_init, n_ctx=n_ctx),
        grid=(B, DF_HEADS, nt + 2),
        in_specs=[
            pl.BlockSpec((None, TT, DF_DV), lambda b, h, t: (b, jnp.minimum(t, nt - 1), h)),
            pl.BlockSpec((None, T, DF_DV), lambda b, h, t: (b, 0, h)),
            pl.BlockSpec((None, VEXT_ROWS, T), lambda b, h, t: (b, h, 0)),
            pl.BlockSpec((4, DF_DQK), lambda b, h, t: (0, 0)),
            pl.BlockSpec((DF_DV, 1), lambda b, h, t: (h, 0)),
        ],
        out_specs=pl.BlockSpec((None, TT, DF_DV), lambda b, h, t: (b, jnp.maximum(t - 2, 0), h)),
        out_shape=jax.ShapeDtypeStruct((B, T, DF_W), BF16),
        scratch_shapes=[pltpu.VMEM((2, T, TT), F32), pltpu.VMEM((2, T, TT), F32),
                        pltpu.VMEM((2, 8, TT), F32), pltpu.VMEM((2, 8, TT), F32),
                        pltpu.VMEM((2, VEXT_ROWS, TT), F32), pltpu.VMEM((2, VEXT_ROWS, TT), F32)],
        compiler_params=_cparams(("parallel", "parallel", "arbitrary")),
        name="diff_attn",
    )(dfq, dfk, dfvx, lam_vecs, norm_g.reshape(DF_W, 1))


def _na_tables(rows):
    kh = min(NA_KH, rows)
    assert kh == NA_KH and rows >= NA_KROWS and rows % NA_QROWS == 0
    n_tiles = rows // NA_QROWS
    qc = np.arange(GRID_W)
    qstart = np.clip(qc - NA_KW // 2, 0, GRID_W - NA_KW)
    col_ok = (qc[None, :] >= qstart[:, None]) & (qc[None, :] < qstart[:, None] + NA_KW)
    drs, valids = [], []
    for tile in (0, 1, n_tiles - 1):
        r0 = tile * NA_QROWS
        start = int(np.clip(r0 - kh // 2, 0, rows - NA_KROWS))
        r = r0 + np.arange(NA_QROWS)
        rs = np.clip(r - kh // 2, 0, rows - kh)
        akr = start + np.arange(NA_KROWS)
        row_ok = (akr[None, :] >= rs[:, None]) & (akr[None, :] < rs[:, None] + kh)
        drs.append(np.clip(akr[None, :] - r[:, None] + NA_KH - 1, 0, 2 * NA_KH - 2))
        ok = row_ok[:, None, :, None] & col_ok[None, :, None, :]
        valids.append(ok.reshape(TT, NA_KROWS * GRID_W))
    return np.stack(drs), np.stack(valids)


def _na_bias_tables(rel_bias, rows):
    dr, valid = _na_tables(rows)
    L, H = rel_bias.shape[:2]
    pad = GRID_W - NA_KW
    flipped = jnp.flip(jnp.pad(rel_bias.astype(F32) * LOG2E, ((0, 0), (0, 0), (0, 0), (pad, pad)),
                               mode="edge"), axis=-1)
    off = flipped.shape[-1] - 1 - (NA_KW - 1 + pad)
    by_col = jnp.stack([flipped[..., off - kc:off - kc + GRID_W] for kc in range(GRID_W)], axis=-2)
    tabs = []
    for ty in range(3):
        per_kr = [jnp.stack([by_col[:, :, int(dr[ty][qr, kr])] for qr in range(NA_QROWS)], axis=3)
                  for kr in range(NA_KROWS)]
        tab = jnp.stack(per_kr, axis=2).reshape(L, H, NA_KROWS * GRID_W, TT)
        tabs.append(jnp.where(valid[ty].T, tab, NEG))
    return jnp.stack(tabs, axis=1)


def _na_kernel(q_ref, k_ref, vx_ref, bias_ref, o_ref, *, n_ctx, rows):
    t = pl.program_id(1)
    dn = (((1,), (1,)), ((), ()))
    tile_type = jnp.where(t <= 1, 0, jnp.where(t == pl.num_programs(1) - 1, 2, 1))

    def run(key_slices):
        def scores(hd):
            sl = slice(hd * NA_DH, (hd + 1) * NA_DH)
            q = q_ref[:, sl]
            parts = []
            for ks, add_bias in key_slices:
                s = lax.dot_general(k_ref[ks, sl], q, dn, preferred_element_type=F32)
                parts.append(s + bias_ref[tile_type, hd] if add_bias else s)
            m = jnp.max(parts[0], axis=0, keepdims=True)
            for p in parts[1:]:
                m = jnp.maximum(m, jnp.max(p, axis=0, keepdims=True))
            return parts, m

        def finish(hd, parts, m):
            o = None
            for (ks, _), s in zip(key_slices, parts):
                e = jnp.exp2(s - m).astype(BF16)
                part = jnp.dot(vx_ref[hd * NA_VROWS:(hd + 1) * NA_VROWS, ks], e,
                               preferred_element_type=F32)
                o = part if o is None else o + part
            return o[0:NA_DH, :] / o[NA_DH:NA_DH + 1, :]

        prev = None
        done = []
        for hd in range(NA_HEADS + 1):
            cur = scores(hd) if hd < NA_HEADS else None
            if prev is not None:
                done.append(finish(hd - 1, *prev))
                if len(done) == 2:
                    c0 = (hd - 2) * NA_DH
                    o_ref[:, c0:c0 + 2 * NA_DH] = jnp.concatenate(done, axis=0).T.astype(o_ref.dtype)
                    done = []
            prev = cur

    @pl.when(t == 0)
    def _():
        run([(slice(0, n_ctx), False)])

    @pl.when(t != 0)
    def _():
        r0 = (t - 1) * NA_QROWS
        start = jnp.clip(r0 - NA_KH // 2, 0, rows - NA_KROWS)
        tok0 = pl.multiple_of(n_ctx + start * GRID_W, TT)
        run([(pl.ds(tok0, NA_KROWS * GRID_W), True), (slice(0, n_ctx), False)])


def _na_call(naq, nak, navx, bias_tab, n_ctx, l):
    B, T, _ = naq.shape
    nt = T // TT
    rows = (T - n_ctx) // GRID_W
    nk = NA_KROWS * GRID_W
    assert n_ctx % TT == 0 and (NA_KH // 2) % NA_QROWS == 0 and (rows - NA_KROWS) % NA_QROWS == 0

    return pl.pallas_call(
        functools.partial(_na_kernel, n_ctx=n_ctx, rows=rows),
        grid=(B, nt),
        in_specs=[
            pl.BlockSpec((None, TT, NA_W), lambda b, t: (b, t, 0)),
            pl.BlockSpec((None, T, NA_W), lambda b, t: (b, 0, 0)),
            pl.BlockSpec((None, NA_HEADS * NA_VROWS, T), lambda b, t: (b, 0, 0)),
            pl.BlockSpec((None, 3, NA_HEADS, nk, TT), lambda b, t: (l, 0, 0, 0, 0),
                         pipeline_mode=pl.Buffered(1)),
        ],
        out_specs=pl.BlockSpec((None, TT, NA_W), lambda b, t: (b, t, 0)),
        out_shape=jax.ShapeDtypeStruct((B, T, NA_W), BF16),
        compiler_params=_cparams(("parallel", "arbitrary")),
        name="na_attn",
    )(naq, nak, navx, bias_tab)


def _merge_tile(x_ref, mod_ref, g_ref, wg_ref, wb_ref, wo_ref, hf_ref, hb_ref, o_ref_in,
                  mlg_ref, bd_ref, cn_ref, out_ref):
    x = x_ref[...]
    D = x.shape[1]
    hn = _norm_mod(x, g_ref[...], mod_ref[3:4, :], mod_ref[4:5, :]).astype(BF16)
    hsum = hf_ref[...] + hb_ref[...]
    parts = []
    for hd in range(ML_HEADS):
        hh = hsum[:, hd * ML_DH:(hd + 1) * ML_DH]
        parts.append(hh * lax.rsqrt(jnp.mean(hh * hh, axis=1, keepdims=True) + EPS))
    a = (_sigmoid(o_ref_in[...]) * (jnp.concatenate(parts, axis=1) * mlg_ref[...])).astype(BF16)
    y = None
    for i, br in enumerate((a, bd_ref[...], cn_ref[...])):
        gate = _sigmoid(jnp.dot(hn, wg_ref[:, i * D:(i + 1) * D], preferred_element_type=F32))
        term = gate * jnp.dot(br, wb_ref[i], preferred_element_type=F32)
        y = term if y is None else y + term
    z = jnp.dot(y.astype(BF16), wo_ref[...], preferred_element_type=F32)
    out_ref[...] = x + mod_ref[5:6, :] * z


def _merge_call(x, modtab, g, w_gate, w_branch, w_out, hf, hb, mlf, ml_norm_g, bd, cn, l):
    B, T, D = x.shape

    def tile_spec(n, cblk=0):
        return pl.BlockSpec((DENSE_NB, TT, n), lambda b, t: (b, t, cblk))

    return pl.pallas_call(
        _per_batch_element(_merge_tile, n_in=12, shared=(2, 3, 4, 5, 9)),
        grid=(B // DENSE_NB, T // TT),
        in_specs=[
            tile_spec(D),
            _mod_spec(DENSE_NB),
            pl.BlockSpec((1, D), lambda b, t: (0, 0)),
            pl.BlockSpec((None, D, N_BRANCH * D), lambda b, t: (l, 0, 0)),
            pl.BlockSpec((None, N_BRANCH, BRANCH_W, D), lambda b, t: (l, 0, 0, 0)),
            pl.BlockSpec((None, D, D), lambda b, t: (l, 0, 0)),
            tile_spec(ML_W), tile_spec(ML_W),
            tile_spec(ML_W, 2),
            pl.BlockSpec((1, ML_W), lambda b, t: (0, 0)),
            tile_spec(DF_W), tile_spec(NA_W),
        ],
        out_specs=tile_spec(D),
        out_shape=jax.ShapeDtypeStruct(x.shape, F32),
        compiler_params=_cparams(("parallel", "parallel")),
        name="merge",
    )(x, modtab, g, w_gate, w_branch, w_out, hf, hb, mlf, ml_norm_g, bd, cn)


def _rope_tables(S, n_ctx):
    t = jnp.arange(S)
    rowp = (t // GRID_W).astype(F32)
    colp = (t % GRID_W).astype(F32)
    half = DF_DQK // 2
    freqs = ROPE_BASE ** (-jnp.arange(0, half, 2, dtype=F32) / half)
    ang = jnp.concatenate([rowp[:, None] * freqs, colp[:, None] * freqs], axis=-1)
    cos, sin = jnp.cos(ang), jnp.sin(ang)
    cos = jnp.concatenate([jnp.ones((n_ctx, half), F32), cos], axis=0)
    sin = jnp.concatenate([jnp.zeros((n_ctx, half), F32), sin], axis=0)
    cos_t = jnp.tile(cos, (1, 4))
    sin_t = jnp.tile(jnp.concatenate([-sin, sin], axis=1), (1, 2))
    return cos_t, sin_t


def kernel(x, c, ctx, c_ctx, w_ada, b_ada, norm_g, ffn_w1, ffn_w2, w_in, ml_conv_w, ml_conv_b,
           ml_gate_b, ml_norm_g, df_lambda, df_norm_g, na_rel_bias, w_branch, w_out, final_g):
    B, S, D = x.shape
    n_ctx = ctx.shape[1]
    depth = w_ada.shape[0]
    assert D == D_MODEL and n_ctx == TT and S % TT == 0 and S % GRID_W == 0
    T = n_ctx + S
    rows = S // GRID_W

    w1 = ffn_w1.astype(BF16)
    w2 = ffn_w2.astype(BF16)
    w_tok, w_tr, w_gate = _split_w_in(w_in)
    wb = w_branch.astype(BF16)
    wo = w_out.astype(BF16)

    cvec = jnp.zeros((16, D), F32).at[0].set(c_ctx).at[1:B + 1].set(c)
    mod = _ada_call(cvec, w_ada, b_ada).reshape(depth, 16, 9, D)
    modtab = jnp.stack([jnp.broadcast_to(mod[:, 0:1], (depth, B, 9, D)), mod[:, 1:B + 1]], axis=2)

    cos_t, sin_t = _rope_tables(S, n_ctx)
    bias_tab = _na_bias_tables(na_rel_bias, rows)

    h = x
    for l in range(depth):
        lam_init = 0.8 - 0.6 * math.exp(-0.3 * l)
        mt = modtab[l]
        h = _ffn_call(h, mt, norm_g[l, 0][None], w1, w2, l, 0, 0,
                      ctx_in=ctx if l == 0 else None, n_ctx=n_ctx)
        mlf, mlvt, mlgt, dfq, dfk, dfvx, naq, nak, navx = _inproj_call(
            h, mt, norm_g[l, 1][None], w_tok, w_tr, cos_t, sin_t, l)
        qk = _conv_call(mlf, ml_conv_w[l], ml_conv_b[l][None], n_ctx)
        hf, hb = _mlstm_call(qk, mlvt, *_gate_prep_call(mlgt, ml_gate_b[l]))
        bd = _diff_call(dfq, dfk, dfvx, df_lambda[l], df_norm_g[l], lam_init, n_ctx)
        cn = _na_call(naq, nak, navx, bias_tab, n_ctx, l)
        h = _merge_call(h, mt, norm_g[l, 1][None], w_gate, wb, wo, hf, hb, mlf,
                        ml_norm_g[l][None], bd, cn, l)
        last = l == depth - 1
        h = _ffn_call(h, mt, norm_g[l, 2][None], w1, w2, l, 1, 6,
                      final_g=final_g[None] if last else None, n_ctx=n_ctx)
    return h
```
